```python
import math
import jax, jax.numpy as jnp
from jax import lax
import numpy as np

D_MODEL = 1024
BATCH = 8
SEQ = 4096
DEPTH = 1

D_MIX = D_MODEL
MLA_HEADS = 8
MLA_NOPE = 64
MLA_ROPE = 32
MLA_V = 64
MLA_Q_RANK = 256
MLA_KV_RANK = 128
MLA_WIDTH = MLA_HEADS * MLA_V
HG_HEADS = 4
HG_EXPAND = 128
HG_WIDTH = D_MIX - MLA_WIDTH
HG_HEAD_V = HG_WIDTH // HG_HEADS
HG_FDIM = HG_HEADS * HG_EXPAND
HG_CHUNK = 64
Q_BLOCK = 128
ROPE_THETA = 10000.0
EPS = 1e-6
IN_SPLITS = (MLA_Q_RANK, MLA_KV_RANK, MLA_ROPE, MLA_WIDTH,
             HG_FDIM, HG_FDIM, HG_WIDTH, HG_WIDTH)
D_IN = MLA_Q_RANK + MLA_KV_RANK + MLA_ROPE + MLA_WIDTH + 2 * HG_FDIM + 2 * HG_WIDTH

kernel_name = "hybrid_mla_hgrn2_parallel_heads"


def rms_norm(x, g):
    xf = x.astype(jnp.float32)
    y = xf * lax.rsqrt(jnp.mean(xf * xf, axis=-1, keepdims=True) + EPS)
    return (y * g.astype(jnp.float32)).astype(x.dtype)


def rope(x, positions):
    half = x.shape[-1] // 2
    inv = ROPE_THETA ** (-jnp.arange(half, dtype=jnp.float32) / half)
    ang = positions.astype(jnp.float32)[..., None] * inv
    ang = ang.reshape(ang.shape[:2] + (1,) * (x.ndim - 3) + (half,))
    cos, sin = jnp.cos(ang), jnp.sin(ang)
    xf = x.astype(jnp.float32)
    x1, x2 = xf[..., :half], xf[..., half:]
    out = jnp.concatenate([x1 * cos - x2 * sin, x2 * cos + x1 * sin], axis=-1)
    return out.astype(x.dtype)


def mla_group(q_lat, kv_lat, k_rope, positions, q_a_norm_g, w_q_b, kv_a_norm_g, w_kv_b):
    B, S, _ = q_lat.shape
    q = rms_norm(q_lat, q_a_norm_g) @ w_q_b
    q = q.reshape(B, S, MLA_HEADS, MLA_NOPE + MLA_ROPE)
    q_nope, q_pe = q[..., :MLA_NOPE], rope(q[..., MLA_NOPE:], positions)
    kv = (rms_norm(kv_lat, kv_a_norm_g) @ w_kv_b).reshape(B, S, MLA_HEADS, MLA_NOPE + MLA_V)
    k_nope, v = kv[..., :MLA_NOPE], kv[..., MLA_NOPE:]
    k_pe = rope(k_rope, positions)
    scale = 1.0 / math.sqrt(MLA_NOPE + MLA_ROPE)

    nb = S // Q_BLOCK
    qn_b = q_nope.reshape(B, nb, Q_BLOCK, MLA_HEADS, MLA_NOPE).transpose(1, 0, 2, 3, 4)
    qp_b = q_pe.reshape(B, nb, Q_BLOCK, MLA_HEADS, MLA_ROPE).transpose(1, 0, 2, 3, 4)
    kpos = jnp.arange(S)

    def block(args):
        qn, qp, bi = args
        s = (jnp.einsum('bqhd,bkhd->bhqk', qn, k_nope)
             + jnp.einsum('bqhr,bkr->bhqk', qp, k_pe)).astype(jnp.float32) * scale
        qpos = bi * Q_BLOCK + jnp.arange(Q_BLOCK)
        mask = kpos[None, :] <= qpos[:, None]
        s = jnp.where(mask, s, -jnp.inf)
        p = jax.nn.softmax(s, axis=-1).astype(v.dtype)
        return jnp.einsum('bhqk,bkhd->bqhd', p, v)

    out = lax.map(block, (qn_b, qp_b, jnp.arange(nb)))
    return out.transpose(1, 0, 2, 3, 4).reshape(B, S, MLA_WIDTH)


def hgrn2_group(q_in, f_in, i_in, lb, norm_g):
    B, S, _ = q_in.shape
    nc = S // HG_CHUNK
    lbf = lb.astype(jnp.float32)
    f = lbf + (1.0 - lbf) * jax.nn.sigmoid(f_in.astype(jnp.float32))
    log_f = jnp.log(f)
    k = 1.0 - f
    q = jax.nn.silu(q_in.astype(jnp.float32))
    v = i_in.astype(jnp.float32)

    def chunks(t, d):
        return t.reshape(B, nc, HG_CHUNK, HG_HEADS, d).transpose(1, 0, 3, 2, 4)

    qc, kc, gc = chunks(q, HG_EXPAND), chunks(k, HG_EXPAND), chunks(log_f, HG_EXPAND)
    vc = chunks(v, HG_HEAD_V)
    causal = jnp.tril(jnp.ones((HG_CHUNK, HG_CHUNK), dtype=bool))

    def step(state, inp):
        qb, kb, vb, gb = inp
        G = jnp.cumsum(gb, axis=-2)
        diff = G[..., :, None, :] - G[..., None, :, :]
        decay = jnp.exp(jnp.where(causal[:, :, None], diff, -jnp.inf))
        A = jnp.einsum('bhtk,bhtsk,bhsk->bhts', qb, decay, kb)
        o = (jnp.einsum('bhts,bhsv->bhtv', A, vb)
             + jnp.einsum('bhtk,bhkv->bhtv', qb * jnp.exp(G), state))
        G_end = G[..., -1:, :]
        new_state = (state * jnp.exp(G_end)[..., 0, :, None]
                     + jnp.einsum('bhsk,bhsv->bhkv', kb * jnp.exp(G_end - G), vb))
        return new_state, o

    s0 = jnp.zeros((B, HG_HEADS, HG_EXPAND, HG_HEAD_V), jnp.float32)
    _, o = lax.scan(step, s0, (qc, kc, vc, gc))
    o = o.transpose(1, 0, 3, 2, 4).reshape(B, S, HG_HEADS, HG_HEAD_V)
    o = o * lax.rsqrt(jnp.mean(o * o, axis=-1, keepdims=True) + EPS)
    o = o * norm_g.astype(jnp.float32).reshape(HG_HEADS, HG_HEAD_V)
    return o.reshape(B, S, HG_WIDTH).astype(q_in.dtype)


def setup_inputs(seed: int = 0) -> dict:
    key = jax.random.key(seed)
    ks = jax.random.split(key, 12)
    nrm = jax.random.normal
    x = nrm(ks[0], (BATCH, SEQ, D_MODEL), jnp.float32)
    start = jax.random.randint(ks[1], (BATCH, 1), 0, 1024, dtype=jnp.int32)
    positions = (start + jnp.arange(SEQ, dtype=jnp.int32)[None, :]).astype(jnp.int32)
    ln_g = 1.0 + 0.02 * nrm(ks[2], (DEPTH, D_MODEL), jnp.float32)
    w_in = nrm(ks[3], (DEPTH, D_MODEL, D_IN), jnp.float32) * D_MODEL ** -0.5
    q_a_norm_g = 1.0 + 0.02 * nrm(ks[4], (DEPTH, MLA_Q_RANK), jnp.float32)
    w_q_b = nrm(ks[5], (DEPTH, MLA_Q_RANK, MLA_HEADS * (MLA_NOPE + MLA_ROPE)), jnp.float32) * MLA_Q_RANK ** -0.5
    kv_a_norm_g = 1.0 + 0.02 * nrm(ks[6], (DEPTH, MLA_KV_RANK), jnp.float32)
    w_kv_b = nrm(ks[7], (DEPTH, MLA_KV_RANK, MLA_HEADS * (MLA_NOPE + MLA_V)), jnp.float32) * MLA_KV_RANK ** -0.5
    hg_lower_bounds = nrm(ks[8], (DEPTH + 1, HG_FDIM), jnp.float32)
    hg_norm_g = 1.0 + 0.02 * nrm(ks[9], (DEPTH, HG_WIDTH), jnp.float32)
    w_out = nrm(ks[10], (DEPTH, D_MIX, D_MODEL), jnp.float32) * D_MIX ** -0.5
    final_norm_g = 1.0 + 0.02 * nrm(ks[11], (D_MODEL,), jnp.float32)
    return {"x": x, "positions": positions, "ln_g": ln_g, "w_in": w_in,
            "q_a_norm_g": q_a_norm_g, "w_q_b": w_q_b, "kv_a_norm_g": kv_a_norm_g,
            "w_kv_b": w_kv_b, "hg_lower_bounds": hg_lower_bounds, "hg_norm_g": hg_norm_g,
            "w_out": w_out, "final_norm_g": final_norm_g}


def reference(x, positions, ln_g, w_in, q_a_norm_g, w_q_b, kv_a_norm_g, w_kv_b,
              hg_lower_bounds, hg_norm_g, w_out, final_norm_g):
    lb_all = jnp.cumsum(jax.nn.softmax(hg_lower_bounds.astype(jnp.float32), axis=0), axis=0)
    split_pts = [int(p) for p in np.cumsum(IN_SPLITS)[:-1]]
    for l in range(DEPTH):
        h = rms_norm(x, ln_g[l])
        proj = h @ w_in[l]
        q_lat, kv_lat, k_rope, g_mla, hq, hf, hi, g_hg = jnp.split(proj, split_pts, axis=-1)
        y_mla = mla_group(q_lat, kv_lat, k_rope, positions,
                          q_a_norm_g[l], w_q_b[l], kv_a_norm_g[l], w_kv_b[l]) * jax.nn.silu(g_mla)
        y_hg = hgrn2_group(hq, hf, hi, lb_all[l], hg_norm_g[l]) * jax.nn.silu(g_hg)
        y = jnp.concatenate([y_mla, y_hg], axis=-1) @ w_out[l]
        x = x + y.astype(x.dtype)
    return rms_norm(x, final_norm_g)
```

```python
import functools
import math

import jax
import jax.numpy as jnp
from jax import lax
from jax.experimental import pallas as pl
from jax.experimental.pallas import tpu as pltpu

F32 = jnp.float32
BF16 = jnp.bfloat16

D_MODEL = 1024
MLA_HEADS = 8
MLA_NOPE = 64
MLA_ROPE = 32
MLA_V = 64
MLA_Q_RANK = 256
MLA_KV_RANK = 128
MLA_WIDTH = MLA_HEADS * MLA_V
HG_HEADS = 4
HG_EXPAND = 128
HG_HEAD_V = 128
HG_WIDTH = HG_HEADS * HG_HEAD_V
HG_FDIM = HG_HEADS * HG_EXPAND
ROPE_THETA = 10000.0
EPS = 1e-6
HALF = MLA_ROPE // 2

LANES = 128
HEAD_SLAB = LANES
NEG_BIG = -1e30

PROJ_TM = 256
ATT_TQ = 256
ATT_TK = 256
HG_CHUNK = 64
HG_SUB = 16
HG_TB = 256
OUT_TM = 512
VMEM_LIMIT = 48 * 1024 * 1024

_C_QLAT = 0
_C_KVLAT = _C_QLAT + MLA_Q_RANK
_C_KRM = _C_KVLAT + MLA_KV_RANK
_C_KRS = _C_KRM + LANES
_C_GM = _C_KRS + LANES
_C_HQ = _C_GM + MLA_WIDTH
_C_HF = _C_HQ + HG_FDIM
_C_HI = _C_HF + HG_FDIM
_C_GH = _C_HI + HG_WIDTH
_C_END = _C_GH + HG_WIDTH

_NT = (((1,), (1,)), ((), ()))
_TN = (((0,), (0,)), ((), ()))


def _rms(x, g):
    return x * lax.rsqrt(jnp.mean(x * x, axis=-1, keepdims=True) + EPS) * g


def _silu(x):
    return x * jax.nn.sigmoid(x)


def _proj_kernel(x_ref, pos_ref, lng_ref, win_ref, qg_ref, wqm_ref, wqs_ref,
                 kvg_ref, wk_ref, wv_ref, lbraw_ref, rope_ref,
                 q_out, k_out, v_out, gm_out, hq_out, hf_out, hv_out, gh_out):
    x = x_ref[0]
    h = _rms(x, lng_ref[...]).astype(BF16)
    proj = jnp.dot(h, win_ref[...], preferred_element_type=F32)

    pos = pos_ref[0].astype(F32)
    ang = pos * rope_ref[0:1, :]
    cos = jnp.cos(ang)
    sin = jnp.sin(ang) * rope_ref[1:2, :]
    scale = 1.0 / math.sqrt(MLA_NOPE + MLA_ROPE)
    cos_q = cos * scale
    sin_q = sin * scale

    qn = _rms(proj[:, _C_QLAT:_C_QLAT + MLA_Q_RANK], qg_ref[...]).astype(BF16)
    qm = jnp.dot(qn, wqm_ref[...], preferred_element_type=F32)
    qs = jnp.dot(qn, wqs_ref[...], preferred_element_type=F32)
    kvn = _rms(proj[:, _C_KVLAT:_C_KVLAT + MLA_KV_RANK], kvg_ref[...]).astype(BF16)
    kk = jnp.dot(kvn, wk_ref[...], preferred_element_type=F32)
    vv = jnp.dot(kvn, wv_ref[...], preferred_element_type=F32)
    kr = proj[:, _C_KRM:_C_KRM + LANES] * cos + proj[:, _C_KRS:_C_KRS + LANES] * sin
    for hd in range(MLA_HEADS):
        sl = slice(hd * HEAD_SLAB, (hd + 1) * HEAD_SLAB)
        q_out[0, hd] = (qm[:, sl] * cos_q + qs[:, sl] * sin_q).astype(BF16)
        k_out[0, hd] = (kk[:, sl] + kr).astype(BF16)
    v_out[0] = vv.astype(BF16)
    gm_out[0] = _silu(proj[:, _C_GM:_C_GM + MLA_WIDTH])

    a0 = lbraw_ref[0:1, :]
    a1 = lbraw_ref[1:2, :]
    mx = jnp.maximum(a0, a1)
    e0 = jnp.exp(a0 - mx)
    e1 = jnp.exp(a1 - mx)
    lb = e0 / (e0 + e1)
    hf_out[0] = lb + (1.0 - lb) * jax.nn.sigmoid(proj[:, _C_HF:_C_HF + HG_FDIM])
    hq_out[0] = _silu(proj[:, _C_HQ:_C_HQ + HG_FDIM])
    hv_out[0] = proj[:, _C_HI:_C_HI + HG_WIDTH].astype(BF16)
    gh_out[0] = _silu(proj[:, _C_GH:_C_GH + HG_WIDTH])


def _proj_call(x, pos3, lng, win, qg, wqm, wqs, kvg, wk, wv, lbraw, rope_tab):
    B, S, _ = x.shape
    tm = PROJ_TM
    grid = (B, S // tm)
    tok = lambda w: pl.BlockSpec((1, tm, w), lambda b, i: (b, i, 0))
    full = lambda a: pl.BlockSpec(a.shape, lambda b, i: (0,) * a.ndim)
    head = pl.BlockSpec((1, MLA_HEADS, tm, HEAD_SLAB), lambda b, i: (b, 0, i, 0))
    out_shape = (
        jax.ShapeDtypeStruct((B, MLA_HEADS, S, HEAD_SLAB), BF16),
        jax.ShapeDtypeStruct((B, MLA_HEADS, S, HEAD_SLAB), BF16),
        jax.ShapeDtypeStruct((B, S, MLA_WIDTH), BF16),
        jax.ShapeDtypeStruct((B, S, MLA_WIDTH), F32),
        jax.ShapeDtypeStruct((B, S, HG_FDIM), F32),
        jax.ShapeDtypeStruct((B, S, HG_FDIM), F32),
        jax.ShapeDtypeStruct((B, S, HG_WIDTH), BF16),
        jax.ShapeDtypeStruct((B, S, HG_WIDTH), F32),
    )
    return pl.pallas_call(
        _proj_kernel,
        grid=grid,
        in_specs=[tok(D_MODEL), tok(1), full(lng), full(win), full(qg), full(wqm), full(wqs),
                  full(kvg), full(wk), full(wv), full(lbraw), full(rope_tab)],
        out_specs=(head, head, tok(MLA_WIDTH), tok(MLA_WIDTH), tok(HG_FDIM), tok(HG_FDIM),
                   tok(HG_WIDTH), tok(HG_WIDTH)),
        out_shape=out_shape,
        compiler_params=pltpu.CompilerParams(
            dimension_semantics=("parallel", "parallel"), vmem_limit_bytes=VMEM_LIMIT),
        name="proj",
    )(x, pos3, lng, win, qg, wqm, wqs, kvg, wk, wv, lbraw, rope_tab)


def _attn_kernel(q_ref, k_ref, v_ref, g_ref, o_ref):
    qi = pl.program_id(2)
    tq, tk = ATT_TQ, ATT_TK
    outs = []
    for hh in range(2):
        q = q_ref[0, hh]

        def step(kb, carry, masked, q=q, hh=hh):
            m, l, acc = carry
            r0 = pl.multiple_of(kb * tk, tk)
            k = k_ref[0, hh, pl.ds(r0, tk), :]
            v = v_ref[0, pl.ds(r0, tk), :]
            s = lax.dot_general(q, k, _NT, preferred_element_type=F32)
            if masked:
                row = lax.broadcasted_iota(jnp.int32, (tq, tk), 0)
                col = lax.broadcasted_iota(jnp.int32, (tq, tk), 1)
                s = jnp.where(col <= row, s, NEG_BIG)
            m_new = jnp.maximum(m, jnp.max(s, axis=-1, keepdims=True))
            p = jnp.exp(s - m_new)
            alpha = jnp.exp(m - m_new)
            l = alpha * l + jnp.sum(p, axis=-1, keepdims=True)
            acc = alpha * acc + jnp.dot(p.astype(BF16), v, preferred_element_type=F32)
            return m_new, l, acc

        init = (jnp.full((tq, 1), NEG_BIG, F32), jnp.zeros((tq, 1), F32),
                jnp.zeros((tq, LANES), F32))
        carry = lax.fori_loop(0, qi, functools.partial(step, masked=False), init)
        m, l, acc = step(qi, carry, masked=True)
        outs.append(acc / l)
    lane = lax.broadcasted_iota(jnp.int32, (tq, LANES), 1)
    o = jnp.where(lane < MLA_V, outs[0], outs[1])
    o_ref[0] = (o * g_ref[0]).astype(BF16)


def _attn_call(q, k, v, gm):
    B, H, S, _ = q.shape
    tq = ATT_TQ
    assert ATT_TQ == ATT_TK
    grid = (B, H // 2, S // tq)
    return pl.pallas_call(
        _attn_kernel,
        grid=grid,
        in_specs=[
            pl.BlockSpec((1, 2, tq, HEAD_SLAB), lambda b, j, i: (b, j, i, 0)),
            pl.BlockSpec((1, 2, S, HEAD_SLAB), lambda b, j, i: (b, j, 0, 0)),
            pl.BlockSpec((1, S, LANES), lambda b, j, i: (b, 0, j)),
            pl.BlockSpec((1, tq, LANES), lambda b, j, i: (b, i, j)),
        ],
        out_specs=pl.BlockSpec((1, tq, LANES), lambda b, j, i: (b, i, j)),
        out_shape=jax.ShapeDtypeStruct((B, S, MLA_WIDTH), BF16),
        compiler_params=pltpu.CompilerParams(
            dimension_semantics=("parallel", "parallel", "arbitrary"),
            vmem_limit_bytes=VMEM_LIMIT),
        name="attn",
    )(q, k, v, gm)


def _hgrn_kernel(f_ref, q_ref, v_ref, g_ref, ng_ref, o_ref, st_scr):
    C, SUB = HG_CHUNK, HG_SUB

    @pl.when(pl.program_id(2) == 0)
    def _():
        st_scr[...] = jnp.zeros_like(st_scr)

    rr = lax.broadcasted_iota(jnp.int32, (C, C), 0)
    cc = lax.broadcasted_iota(jnp.int32, (C, C), 1)
    tri = (rr >= cc).astype(BF16)
    ones = jnp.ones((LANES, LANES), BF16)
    rows_t = lax.broadcasted_iota(jnp.int32, (SUB, LANES), 0)
    ng = ng_ref[...]

    def chunk(c, carry):
        r0 = pl.multiple_of(c * C, C)
        f = f_ref[0, pl.ds(r0, C), :]
        q = q_ref[0, pl.ds(r0, C), :]
        v = v_ref[0, pl.ds(r0, C), :]
        vf = v.astype(F32)
        g = jnp.log(f)
        k = 1.0 - f
        g1 = g.astype(BF16)
        e1 = g - g1.astype(F32)
        g2 = e1.astype(BF16)
        g3 = (e1 - g2.astype(F32)).astype(BF16)
        gc = jnp.dot(tri, jnp.concatenate([g1, g2, g3], axis=1), preferred_element_type=F32)
        G = gc[:, :LANES] + gc[:, LANES:2 * LANES] + gc[:, 2 * LANES:]
        g_end = G[C - 1:C, :]

        st = st_scr[...]
        qd = (q * jnp.exp(G)).astype(BF16)
        o = lax.dot_general(qd, st.astype(BF16), _NT, preferred_element_type=F32)
        kd = (k * jnp.exp(g_end - G)).astype(BF16)
        st_scr[...] = st * jnp.exp(g_end) + lax.dot_general(v, kd, _TN, preferred_element_type=F32)

        blocks = []
        for i in range(C // SUB):
            lo = i * SUB
            gb = G[lo:lo + SUB]
            qb = q[lo:lo + SUB]
            ob = o[lo:lo + SUB]
            if i > 0:
                ref = G[lo - 1:lo]
                qi = (qb * jnp.exp(gb - ref)).astype(BF16)
                ki = (k[:lo] * jnp.exp(ref - G[:lo])).astype(BF16)
                a = lax.dot_general(qi, ki, _NT, preferred_element_type=F32)
                ob = ob + jnp.dot(a.astype(BF16), v[:lo], preferred_element_type=F32)
            ps = []
            for s in range(SUB):
                gs = G[lo + s:lo + s + 1]
                ks = k[lo + s:lo + s + 1]
                dg = jnp.where(rows_t >= s, gb - gs, NEG_BIG)
                ps.append((qb * jnp.exp(dg) * ks).astype(BF16))
            r = jnp.dot(jnp.concatenate(ps, axis=0), ones, preferred_element_type=F32)
            for s in range(SUB):
                ob = ob + r[s * SUB:(s + 1) * SUB] * vf[lo + s:lo + s + 1]
            blocks.append(ob)
        o = jnp.concatenate(blocks, axis=0)
        o = o * lax.rsqrt(jnp.mean(o * o, axis=-1, keepdims=True) + EPS) * ng
        o_ref[0, pl.ds(r0, C), :] = (o * g_ref[0, pl.ds(r0, C), :]).astype(BF16)
        return carry

    lax.fori_loop(0, HG_TB // C, chunk, 0)


def _hgrn_call(hf, hq, hv, gh, ng):
    B, S, _ = hf.shape
    tb = HG_TB
    grid = (B, HG_HEADS, S // tb)
    blk = pl.BlockSpec((1, tb, LANES), lambda b, h, t: (b, t, h))
    return pl.pallas_call(
        _hgrn_kernel,
        grid=grid,
        in_specs=[blk, blk, blk, blk, pl.BlockSpec((1, LANES), lambda b, h, t: (0, h))],
        out_specs=blk,
        out_shape=jax.ShapeDtypeStruct((B, S, HG_WIDTH), BF16),
        scratch_shapes=[pltpu.VMEM((HG_HEAD_V, HG_EXPAND), F32)],
        compiler_params=pltpu.CompilerParams(
            dimension_semantics=("parallel", "parallel", "arbitrary"),
            vmem_limit_bytes=VMEM_LIMIT),
        name="hgrn",
    )(hf, hq, hv, gh, ng)


def _out_kernel(ya_ref, yh_ref, x_ref, wa_ref, wh_ref, fg_ref, o_ref):
    y = jnp.dot(ya_ref[0], wa_ref[...], preferred_element_type=F32)
    y = y + jnp.dot(yh_ref[0], wh_ref[...], preferred_element_type=F32)
    o_ref[0] = _rms(x_ref[0] + y, fg_ref[...])


def _out_call(ya, yh, x, wa, wh, fg):
    B, S, _ = x.shape
    tm = OUT_TM
    grid = (B, S // tm)
    tok = lambda w: pl.BlockSpec((1, tm, w), lambda b, i: (b, i, 0))
    full = lambda a: pl.BlockSpec(a.shape, lambda b, i: (0,) * a.ndim)
    return pl.pallas_call(
        _out_kernel,
        grid=grid,
        in_specs=[tok(MLA_WIDTH), tok(HG_WIDTH), tok(D_MODEL), full(wa), full(wh), full(fg)],
        out_specs=tok(D_MODEL),
        out_shape=jax.ShapeDtypeStruct((B, S, D_MODEL), F32),
        compiler_params=pltpu.CompilerParams(
            dimension_semantics=("parallel", "parallel"), vmem_limit_bytes=VMEM_LIMIT),
        name="outproj",
    )(ya, yh, x, wa, wh, fg)


def _pack_weights(w_in, w_q_b, w_kv_b):
    o = 0
    w_qlat = w_in[:, o:o + MLA_Q_RANK]; o += MLA_Q_RANK
    w_kvlat = w_in[:, o:o + MLA_KV_RANK]; o += MLA_KV_RANK
    w_kr = w_in[:, o:o + MLA_ROPE]; o += MLA_ROPE
    w_rest = w_in[:, o:]
    z = lambda n: jnp.zeros((w_in.shape[0], n), w_in.dtype)
    x1, x2 = w_kr[:, :HALF], w_kr[:, HALF:]
    kr_main = jnp.concatenate([z(MLA_NOPE), x1, x2, z(LANES - MLA_NOPE - MLA_ROPE)], axis=1)
    kr_swap = jnp.concatenate([z(MLA_NOPE), x2, x1, z(LANES - MLA_NOPE - MLA_ROPE)], axis=1)
    win = jnp.concatenate([w_qlat, w_kvlat, kr_main, kr_swap, w_rest], axis=1).astype(BF16)
    assert win.shape[1] == _C_END

    wq = w_q_b.reshape(MLA_Q_RANK, MLA_HEADS, MLA_NOPE + MLA_ROPE)
    qn_, q1, q2 = wq[..., :MLA_NOPE], wq[..., MLA_NOPE:MLA_NOPE + HALF], wq[..., MLA_NOPE + HALF:]
    zq = lambda n: jnp.zeros((MLA_Q_RANK, MLA_HEADS, n), wq.dtype)
    pad = LANES - MLA_NOPE - MLA_ROPE
    wqm = jnp.concatenate([qn_, q1, q2, zq(pad)], axis=-1).reshape(MLA_Q_RANK, -1).astype(BF16)
    wqs = jnp.concatenate([zq(MLA_NOPE), q2, q1, zq(pad)], axis=-1).reshape(MLA_Q_RANK, -1).astype(BF16)

    wkv = w_kv_b.reshape(MLA_KV_RANK, MLA_HEADS, MLA_NOPE + MLA_V)
    wk = jnp.concatenate([wkv[..., :MLA_NOPE],
                          jnp.zeros((MLA_KV_RANK, MLA_HEADS, LANES - MLA_NOPE), wkv.dtype)],
                         axis=-1).reshape(MLA_KV_RANK, -1).astype(BF16)
    wv = wkv[..., MLA_NOPE:].reshape(MLA_KV_RANK, -1).astype(BF16)
    return win, wqm, wqs, wk, wv


def _rope_table():
    inv = ROPE_THETA ** (-jnp.arange(HALF, dtype=F32) / HALF)
    zl = jnp.zeros((MLA_NOPE,), F32)
    zr = jnp.zeros((LANES - MLA_NOPE - MLA_ROPE,), F32)
    inv_lane = jnp.concatenate([zl, inv, inv, zr])
    sign = jnp.concatenate([zl, -jnp.ones((HALF,), F32), jnp.ones((HALF,), F32), zr])
    return jnp.stack([inv_lane, sign])


def kernel(x, positions, ln_g, w_in, q_a_norm_g, w_q_b, kv_a_norm_g, w_kv_b,
           hg_lower_bounds, hg_norm_g, w_out, final_norm_g):
    B, S, _ = x.shape
    assert ln_g.shape[0] == 1, "single-layer stack"
    win, wqm, wqs, wk, wv = _pack_weights(w_in[0], w_q_b[0], w_kv_b[0])
    pos3 = positions.reshape(B, S, 1)
    q, k, v, gm, hq, hf, hv, gh = _proj_call(
        x, pos3, ln_g[0:1], win, q_a_norm_g[0:1], wqm, wqs, kv_a_norm_g[0:1], wk, wv,
        hg_lower_bounds, _rope_table())
    ya = _attn_call(q, k, v, gm)
    yh = _hgrn_call(hf, hq, hv, gh, hg_norm_g[0:1])
    wo = w_out[0].astype(BF16)
    return _out_call(ya, yh, x, wo[:MLA_WIDTH], wo[MLA_WIDTH:], final_norm_g.reshape(1, D_MODEL))
```

```python
import functools
import math

import jax
import jax.numpy as jnp
from jax import lax
from jax.experimental import pallas as pl
from jax.experimental.pallas import tpu as pltpu

F32 = jnp.float32
BF16 = jnp.bfloat16

D_MODEL = 1024
MLA_HEADS = 8
MLA_NOPE = 64
MLA_ROPE = 32
MLA_V = 64
MLA_Q_RANK = 256
MLA_KV_RANK = 128
MLA_WIDTH = MLA_HEADS * MLA_V
HG_HEADS = 4
HG_EXPAND = 128
HG_HEAD_V = 128
HG_WIDTH = HG_HEADS * HG_HEAD_V
HG_FDIM = HG_HEADS * HG_EXPAND
ROPE_THETA = 10000.0
EPS = 1e-6
HALF = MLA_ROPE // 2

LANES = 128
HEAD_SLAB = LANES
NEG_BIG = -1e30
LOG2E = math.log2(math.e)
VT_ROWS = 80

PROJ_TM = 256
ATT_TQ = 1024
ATT_TK = 256
ATT_NH = 4
ATT_CW = 128
HG_CHUNK = 64
HG_SUB = 16
HG_TB = 256
OUT_TM = 512
VMEM_LIMIT = 48 * 1024 * 1024

_C_QLAT = 0
_C_KVLAT = _C_QLAT + MLA_Q_RANK
_C_KRM = _C_KVLAT + MLA_KV_RANK
_C_KRS = _C_KRM + LANES
_C_GM = _C_KRS + LANES
_C_HQ = _C_GM + MLA_WIDTH
_C_HF = _C_HQ + HG_FDIM
_C_HI = _C_HF + HG_FDIM
_C_GH = _C_HI + HG_WIDTH
_C_END = _C_GH + HG_WIDTH

_NT = (((1,), (1,)), ((), ()))
_TN = (((0,), (0,)), ((), ()))


def _rms(x, g):
    return x * lax.rsqrt(jnp.mean(x * x, axis=-1, keepdims=True) + EPS) * g


def _silu(x):
    return x * jax.nn.sigmoid(x)


def _proj_kernel(x_ref, pos_ref, lng_ref, win_ref, qg_ref, wqm_ref, wqs_ref,
                 kvg_ref, wk_ref, wvt_ref, lbraw_ref, rope_ref,
                 q_out, k_out, vt_out, gm_out, hq_out, hf_out, hv_out, gh_out):
    x = x_ref[0]
    h = _rms(x, lng_ref[...]).astype(BF16)
    proj = jnp.dot(h, win_ref[...], preferred_element_type=F32)

    pos = pos_ref[0].astype(F32)
    ang = pos * rope_ref[0:1, :]
    cos = jnp.cos(ang)
    sin = jnp.sin(ang) * rope_ref[1:2, :]
    scale = LOG2E / math.sqrt(MLA_NOPE + MLA_ROPE)
    cos_q = cos * scale
    sin_q = sin * scale

    qn = _rms(proj[:, _C_QLAT:_C_QLAT + MLA_Q_RANK], qg_ref[...]).astype(BF16)
    qm = jnp.dot(qn, wqm_ref[...], preferred_element_type=F32)
    qs = jnp.dot(qn, wqs_ref[...], preferred_element_type=F32)
    kvn = _rms(proj[:, _C_KVLAT:_C_KVLAT + MLA_KV_RANK], kvg_ref[...]).astype(BF16)
    kk = jnp.dot(kvn, wk_ref[...], preferred_element_type=F32)
    kr = proj[:, _C_KRM:_C_KRM + LANES] * cos + proj[:, _C_KRS:_C_KRS + LANES] * sin
    ones_row = (lax.broadcasted_iota(jnp.int32, (VT_ROWS, 1), 0) == MLA_V).astype(F32)
    for hd in range(MLA_HEADS):
        sl = slice(hd * HEAD_SLAB, (hd + 1) * HEAD_SLAB)
        q_out[0, hd] = (qm[:, sl] * cos_q + qs[:, sl] * sin_q).astype(BF16)
        k_out[0, hd] = (kk[:, sl] + kr).astype(BF16)
        vt = lax.dot_general(wvt_ref[hd], kvn, _NT, preferred_element_type=F32)
        vt_out[0, hd] = (vt + ones_row).astype(BF16)
    gm_out[0] = _silu(proj[:, _C_GM:_C_GM + MLA_WIDTH])

    a0 = lbraw_ref[0:1, :]
    a1 = lbraw_ref[1:2, :]
    mx = jnp.maximum(a0, a1)
    e0 = jnp.exp(a0 - mx)
    e1 = jnp.exp(a1 - mx)
    lb = e0 / (e0 + e1)
    hf_out[0] = lb + (1.0 - lb) * jax.nn.sigmoid(proj[:, _C_HF:_C_HF + HG_FDIM])
    hq_out[0] = _silu(proj[:, _C_HQ:_C_HQ + HG_FDIM])
    hv_out[0] = proj[:, _C_HI:_C_HI + HG_WIDTH].astype(BF16)
    gh_out[0] = _silu(proj[:, _C_GH:_C_GH + HG_WIDTH])


def _proj_call(x, pos3, lng, win, qg, wqm, wqs, kvg, wk, wvt, lbraw, rope_tab):
    B, S, _ = x.shape
    tm = PROJ_TM
    grid = (B, S // tm)
    tok = lambda w: pl.BlockSpec((1, tm, w), lambda b, i: (b, i, 0))
    full = lambda a: pl.BlockSpec(a.shape, lambda b, i: (0,) * a.ndim)
    head = pl.BlockSpec((1, MLA_HEADS, tm, HEAD_SLAB), lambda b, i: (b, 0, i, 0))
    out_shape = (
        jax.ShapeDtypeStruct((B, MLA_HEADS, S, HEAD_SLAB), BF16),
        jax.ShapeDtypeStruct((B, MLA_HEADS, S, HEAD_SLAB), BF16),
        jax.ShapeDtypeStruct((B, MLA_HEADS, VT_ROWS, S), BF16),
        jax.ShapeDtypeStruct((B, S, MLA_WIDTH), F32),
        jax.ShapeDtypeStruct((B, S, HG_FDIM), F32),
        jax.ShapeDtypeStruct((B, S, HG_FDIM), F32),
        jax.ShapeDtypeStruct((B, S, HG_WIDTH), BF16),
        jax.ShapeDtypeStruct((B, S, HG_WIDTH), F32),
    )
    return pl.pallas_call(
        _proj_kernel,
        grid=grid,
        in_specs=[tok(D_MODEL), tok(1), full(lng), full(win), full(qg), full(wqm), full(wqs),
                  full(kvg), full(wk), full(wvt), full(lbraw), full(rope_tab)],
        out_specs=(head, head,
                   pl.BlockSpec((1, MLA_HEADS, VT_ROWS, tm), lambda b, i: (b, 0, 0, i)),
                   tok(MLA_WIDTH), tok(HG_FDIM), tok(HG_FDIM),
                   tok(HG_WIDTH), tok(HG_WIDTH)),
        out_shape=out_shape,
        compiler_params=pltpu.CompilerParams(
            dimension_semantics=("parallel", "parallel"), vmem_limit_bytes=VMEM_LIMIT),
        name="proj",
    )(x, pos3, lng, win, qg, wqm, wqs, kvg, wk, wvt, lbraw, rope_tab)


def _attn_kernel(q_ref, k_ref, vt_ref, g_ref, o_ref, s_scr, p_scr, m_scr, acc_scr):
    qi = pl.program_id(2)
    tq, tk, nh, cw = ATT_TQ, ATT_TK, ATT_NH, ATT_CW
    m_scr[...] = jnp.full(m_scr.shape, NEG_BIG, F32)
    acc_scr[...] = jnp.zeros(acc_scr.shape, F32)

    def scores(h, r0, c0):
        k = k_ref[0, h, pl.ds(r0, tk), :]
        q = q_ref[0, h, c0:, :]
        s_scr[h, :, c0:] = lax.dot_general(k, q, _NT, preferred_element_type=F32)

    def softmax_pv(h, r0, c0, masked):
        alphas = []
        for c in range(c0, tq, cw):
            s = s_scr[h, :, c:c + cw]
            if masked and c - c0 < tk:
                key = lax.broadcasted_iota(jnp.int32, (tk, cw), 0)
                qry = lax.broadcasted_iota(jnp.int32, (tk, cw), 1) + (c - c0)
                s = jnp.where(key <= qry, s, NEG_BIG)
            m_old = m_scr[h, :, c:c + cw]
            m_new = jnp.maximum(m_old, jnp.max(s, axis=0, keepdims=True))
            p_scr[h, :, c:c + cw] = jnp.exp2(s - m_new).astype(BF16)
            alphas.append(jnp.exp2(m_old - m_new))
            m_scr[h, :, c:c + cw] = m_new
        alpha = jnp.concatenate(alphas, axis=1) if len(alphas) > 1 else alphas[0]
        vt = vt_ref[0, h, :, pl.ds(r0, tk)]
        pv = jnp.dot(vt, p_scr[h, :, c0:], preferred_element_type=F32)
        acc_scr[h, :, c0:] = alpha * acc_scr[h, :, c0:] + pv

    def step(r0, c0, masked):
        for h in range(nh):
            scores(h, r0, c0)
        for h in range(nh):
            softmax_pv(h, r0, c0, masked)

    def full_step(kb, carry):
        step(pl.multiple_of(kb * tk, tk), 0, False)
        return carry

    lax.fori_loop(0, qi * (tq // tk), full_step, 0)
    for j in range(tq // tk):
        step(pl.multiple_of(qi * tq + j * tk, tk), j * tk, True)

    outs = []
    for h in range(nh):
        acc = acc_scr[h]
        o = acc[:MLA_V] / acc[MLA_V:MLA_V + 1]
        outs.append(o.T)
    o_ref[0] = (jnp.concatenate(outs, axis=1) * g_ref[0]).astype(BF16)


def _attn_call(q, k, vt, gm):
    B, H, S, _ = q.shape
    tq, nh = ATT_TQ, ATT_NH
    grid = (B, H // nh, S // tq)
    return pl.pallas_call(
        _attn_kernel,
        grid=grid,
        in_specs=[
            pl.BlockSpec((1, nh, tq, HEAD_SLAB), lambda b, j, i: (b, j, i, 0)),
            pl.BlockSpec((1, nh, S, HEAD_SLAB), lambda b, j, i: (b, j, 0, 0)),
            pl.BlockSpec((1, nh, VT_ROWS, S), lambda b, j, i: (b, j, 0, 0)),
            pl.BlockSpec((1, tq, nh * MLA_V), lambda b, j, i: (b, i, j)),
        ],
        out_specs=pl.BlockSpec((1, tq, nh * MLA_V), lambda b, j, i: (b, i, j)),
        out_shape=jax.ShapeDtypeStruct((B, S, MLA_WIDTH), BF16),
        scratch_shapes=[
            pltpu.VMEM((nh, ATT_TK, tq), F32),
            pltpu.VMEM((nh, ATT_TK, tq), BF16),
            pltpu.VMEM((nh, 1, tq), F32),
            pltpu.VMEM((nh, VT_ROWS, tq), F32),
        ],
        compiler_params=pltpu.CompilerParams(
            dimension_semantics=("parallel", "parallel", "arbitrary"),
            vmem_limit_bytes=VMEM_LIMIT),
        name="attn",
    )(q, k, vt, gm)


def _hgrn_kernel(f_ref, q_ref, v_ref, g_ref, ng_ref, o_ref, st_scr):
    C, SUB = HG_CHUNK, HG_SUB

    @pl.when(pl.program_id(2) == 0)
    def _():
        st_scr[...] = jnp.zeros_like(st_scr)

    rr = lax.broadcasted_iota(jnp.int32, (C, C), 0)
    cc = lax.broadcasted_iota(jnp.int32, (C, C), 1)
    tri = (rr >= cc).astype(BF16)
    ones = jnp.ones((LANES, LANES), BF16)
    rows_t = lax.broadcasted_iota(jnp.int32, (SUB, LANES), 0)
    ng = ng_ref[...]

    def chunk(c, carry):
        r0 = pl.multiple_of(c * C, C)
        f = f_ref[0, pl.ds(r0, C), :]
        q = q_ref[0, pl.ds(r0, C), :]
        v = v_ref[0, pl.ds(r0, C), :]
        vf = v.astype(F32)
        g = jnp.log(f)
        k = 1.0 - f
        g1 = g.astype(BF16)
        e1 = g - g1.astype(F32)
        g2 = e1.astype(BF16)
        g3 = (e1 - g2.astype(F32)).astype(BF16)
        gc = jnp.dot(tri, jnp.concatenate([g1, g2, g3], axis=1), preferred_element_type=F32)
        G = gc[:, :LANES] + gc[:, LANES:2 * LANES] + gc[:, 2 * LANES:]
        g_end = G[C - 1:C, :]

        st = st_scr[...]
        qd = (q * jnp.exp(G)).astype(BF16)
        o = lax.dot_general(qd, st.astype(BF16), _NT, preferred_element_type=F32)
        kd = (k * jnp.exp(g_end - G)).astype(BF16)
        st_scr[...] = st * jnp.exp(g_end) + lax.dot_general(v, kd, _TN, preferred_element_type=F32)

        blocks = []
        for i in range(C // SUB):
            lo = i * SUB
            gb = G[lo:lo + SUB]
            qb = q[lo:lo + SUB]
            ob = o[lo:lo + SUB]
            if i > 0:
                ref = G[lo - 1:lo]
                qi = (qb * jnp.exp(gb - ref)).astype(BF16)
                ki = (k[:lo] * jnp.exp(ref - G[:lo])).astype(BF16)
                a = lax.dot_general(qi, ki, _NT, preferred_element_type=F32)
                ob = ob + jnp.dot(a.astype(BF16), v[:lo], preferred_element_type=F32)
            ps = []
            for s in range(SUB):
                gs = G[lo + s:lo + s + 1]
                ks = k[lo + s:lo + s + 1]
                dg = jnp.where(rows_t >= s, gb - gs, NEG_BIG)
                ps.append((qb * jnp.exp(dg) * ks).astype(BF16))
            r = jnp.dot(jnp.concatenate(ps, axis=0), ones, preferred_element_type=F32)
            for s in range(SUB):
                ob = ob + r[s * SUB:(s + 1) * SUB] * vf[lo + s:lo + s + 1]
            blocks.append(ob)
        o = jnp.concatenate(blocks, axis=0)
        o = o * lax.rsqrt(jnp.mean(o * o, axis=-1, keepdims=True) + EPS) * ng
        o_ref[0, pl.ds(r0, C), :] = (o * g_ref[0, pl.ds(r0, C), :]).astype(BF16)
        return carry

    lax.fori_loop(0, HG_TB // C, chunk, 0)


def _hgrn_call(hf, hq, hv, gh, ng):
    B, S, _ = hf.shape
    tb = HG_TB
    grid = (B, HG_HEADS, S // tb)
    blk = pl.BlockSpec((1, tb, LANES), lambda b, h, t: (b, t, h))
    return pl.pallas_call(
        _hgrn_kernel,
        grid=grid,
        in_specs=[blk, blk, blk, blk, pl.BlockSpec((1, LANES), lambda b, h, t: (0, h))],
        out_specs=blk,
        out_shape=jax.ShapeDtypeStruct((B, S, HG_WIDTH), BF16),
        scratch_shapes=[pltpu.VMEM((HG_HEAD_V, HG_EXPAND), F32)],
        compiler_params=pltpu.CompilerParams(
            dimension_semantics=("parallel", "parallel", "arbitrary"),
            vmem_limit_bytes=VMEM_LIMIT),
        name="hgrn",
    )(hf, hq, hv, gh, ng)


def _out_kernel(ya_ref, yh_ref, x_ref, wa_ref, wh_ref, fg_ref, o_ref):
    y = jnp.dot(ya_ref[0], wa_ref[...], preferred_element_type=F32)
    y = y + jnp.dot(yh_ref[0], wh_ref[...], preferred_element_type=F32)
    o_ref[0] = _rms(x_ref[0] + y, fg_ref[...])


def _out_call(ya, yh, x, wa, wh, fg):
    B, S, _ = x.shape
    tm = OUT_TM
    grid = (B, S // tm)
    tok = lambda w: pl.BlockSpec((1, tm, w), lambda b, i: (b, i, 0))
    full = lambda a: pl.BlockSpec(a.shape, lambda b, i: (0,) * a.ndim)
    return pl.pallas_call(
        _out_kernel,
        grid=grid,
        in_specs=[tok(MLA_WIDTH), tok(HG_WIDTH), tok(D_MODEL), full(wa), full(wh), full(fg)],
        out_specs=tok(D_MODEL),
        out_shape=jax.ShapeDtypeStruct((B, S, D_MODEL), F32),
        compiler_params=pltpu.CompilerParams(
            dimension_semantics=("parallel", "parallel"), vmem_limit_bytes=VMEM_LIMIT),
        name="outproj",
    )(ya, yh, x, wa, wh, fg)


def _pack_weights(w_in, w_q_b, w_kv_b):
    o = 0
    w_qlat = w_in[:, o:o + MLA_Q_RANK]; o += MLA_Q_RANK
    w_kvlat = w_in[:, o:o + MLA_KV_RANK]; o += MLA_KV_RANK
    w_kr = w_in[:, o:o + MLA_ROPE]; o += MLA_ROPE
    w_rest = w_in[:, o:]
    z = lambda n: jnp.zeros((w_in.shape[0], n), w_in.dtype)
    x1, x2 = w_kr[:, :HALF], w_kr[:, HALF:]
    kr_main = jnp.concatenate([z(MLA_NOPE), x1, x2, z(LANES - MLA_NOPE - MLA_ROPE)], axis=1)
    kr_swap = jnp.concatenate([z(MLA_NOPE), x2, x1, z(LANES - MLA_NOPE - MLA_ROPE)], axis=1)
    win = jnp.concatenate([w_qlat, w_kvlat, kr_main, kr_swap, w_rest], axis=1).astype(BF16)
    assert win.shape[1] == _C_END

    wq = w_q_b.reshape(MLA_Q_RANK, MLA_HEADS, MLA_NOPE + MLA_ROPE)
    qn_, q1, q2 = wq[..., :MLA_NOPE], wq[..., MLA_NOPE:MLA_NOPE + HALF], wq[..., MLA_NOPE + HALF:]
    zq = lambda n: jnp.zeros((MLA_Q_RANK, MLA_HEADS, n), wq.dtype)
    pad = LANES - MLA_NOPE - MLA_ROPE
    wqm = jnp.concatenate([qn_, q1, q2, zq(pad)], axis=-1).reshape(MLA_Q_RANK, -1).astype(BF16)
    wqs = jnp.concatenate([zq(MLA_NOPE), q2, q1, zq(pad)], axis=-1).reshape(MLA_Q_RANK, -1).astype(BF16)

    wkv = w_kv_b.reshape(MLA_KV_RANK, MLA_HEADS, MLA_NOPE + MLA_V)
    wk = jnp.concatenate([wkv[..., :MLA_NOPE],
                          jnp.zeros((MLA_KV_RANK, MLA_HEADS, LANES - MLA_NOPE), wkv.dtype)],
                         axis=-1).reshape(MLA_KV_RANK, -1).astype(BF16)
    wvt = jnp.transpose(wkv[..., MLA_NOPE:], (1, 2, 0))
    wvt = jnp.concatenate([wvt, jnp.zeros((MLA_HEADS, VT_ROWS - MLA_V, MLA_KV_RANK), wvt.dtype)],
                          axis=1).astype(BF16)
    return win, wqm, wqs, wk, wvt


def _rope_table():
    inv = ROPE_THETA ** (-jnp.arange(HALF, dtype=F32) / HALF)
    zl = jnp.zeros((MLA_NOPE,), F32)
    zr = jnp.zeros((LANES - MLA_NOPE - MLA_ROPE,), F32)
    inv_lane = jnp.concatenate([zl, inv, inv, zr])
    sign = jnp.concatenate([zl, -jnp.ones((HALF,), F32), jnp.ones((HALF,), F32), zr])
    return jnp.stack([inv_lane, sign])


def kernel(x, positions, ln_g, w_in, q_a_norm_g, w_q_b, kv_a_norm_g, w_kv_b,
           hg_lower_bounds, hg_norm_g, w_out, final_norm_g):
    B, S, _ = x.shape
    assert ln_g.shape[0] == 1, "single-layer stack"
    win, wqm, wqs, wk, wvt = _pack_weights(w_in[0], w_q_b[0], w_kv_b[0])
    pos3 = positions.reshape(B, S, 1)
    q, k, vt, gm, hq, hf, hv, gh = _proj_call(
        x, pos3, ln_g[0:1], win, q_a_norm_g[0:1], wqm, wqs, kv_a_norm_g[0:1], wk, wvt,
        hg_lower_bounds, _rope_table())
    ya = _attn_call(q, k, vt, gm)
    yh = _hgrn_call(hf, hq, hv, gh, hg_norm_g[0:1])
    wo = w_out[0].astype(BF16)
    return _out_call(ya, yh, x, wo[:MLA_WIDTH], wo[MLA_WIDTH:], final_norm_g.reshape(1, D_MODEL))
```

```python
import functools
import math

import numpy as np
import jax
import jax.numpy as jnp
from jax import lax
from jax.experimental import pallas as pl
from jax.experimental.pallas import tpu as pltpu

F32 = jnp.float32
BF16 = jnp.bfloat16

D_MODEL = 1024
MLA_HEADS = 8
MLA_NOPE = 64
MLA_ROPE = 32
MLA_V = 64
MLA_Q_RANK = 256
MLA_KV_RANK = 128
MLA_WIDTH = MLA_HEADS * MLA_V
HG_HEADS = 4
HG_EXPAND = 128
HG_HEAD_V = 128
HG_WIDTH = HG_HEADS * HG_HEAD_V
HG_FDIM = HG_HEADS * HG_EXPAND
ROPE_THETA = 10000.0
EPS = 1e-6
HALF = MLA_ROPE // 2

LANES = 128
HEAD_SLAB = LANES
NEG_BIG = -1e30
LOG2E = math.log2(math.e)
VT_ROWS = 80

PROJ_TM = 256
ATT_TQ = 1024
ATT_TK = 256
ATT_NH = 4
ATT_CW = 128
HG_CHUNK = 64
HG_SUB = 8
HG_LEVELS = 4
HG_TB = 256
OUT_TM = 512
VMEM_LIMIT = 48 * 1024 * 1024

_C_QLAT = 0
_C_KVLAT = _C_QLAT + MLA_Q_RANK
_C_KRM = _C_KVLAT + MLA_KV_RANK
_C_KRS = _C_KRM + LANES
_C_GM = _C_KRS + LANES
_C_HQ = _C_GM + MLA_WIDTH
_C_HF = _C_HQ + HG_FDIM
_C_HI = _C_HF + HG_FDIM
_C_GH = _C_HI + HG_WIDTH
_C_END = _C_GH + HG_WIDTH

_NT = (((1,), (1,)), ((), ()))
_TN = (((0,), (0,)), ((), ()))


def _rms(x, g):
    return x * lax.rsqrt(jnp.mean(x * x, axis=-1, keepdims=True) + EPS) * g


def _silu(x):
    return x * jax.nn.sigmoid(x)


def _proj_kernel(x_ref, pos_ref, lng_ref, win_ref, qg_ref, wqm_ref, wqs_ref,
                 kvg_ref, wk_ref, wvt_ref, lbraw_ref, rope_ref,
                 q_out, k_out, vt_out, gm_out, hq_out, hf_out, hv_out, gh_out):
    x = x_ref[0]
    h = _rms(x, lng_ref[...]).astype(BF16)
    proj = jnp.dot(h, win_ref[...], preferred_element_type=F32)

    pos = pos_ref[0].astype(F32)
    ang = pos * rope_ref[0:1, :]
    cos = jnp.cos(ang)
    sin = jnp.sin(ang) * rope_ref[1:2, :]
    scale = LOG2E / math.sqrt(MLA_NOPE + MLA_ROPE)
    cos_q = cos * scale
    sin_q = sin * scale

    qn = _rms(proj[:, _C_QLAT:_C_QLAT + MLA_Q_RANK], qg_ref[...]).astype(BF16)
    qm = jnp.dot(qn, wqm_ref[...], preferred_element_type=F32)
    qs = jnp.dot(qn, wqs_ref[...], preferred_element_type=F32)
    kvn = _rms(proj[:, _C_KVLAT:_C_KVLAT + MLA_KV_RANK], kvg_ref[...]).astype(BF16)
    kk = jnp.dot(kvn, wk_ref[...], preferred_element_type=F32)
    kr = proj[:, _C_KRM:_C_KRM + LANES] * cos + proj[:, _C_KRS:_C_KRS + LANES] * sin
    ones_row = (lax.broadcasted_iota(jnp.int32, (VT_ROWS, 1), 0) == MLA_V).astype(F32)
    for hd in range(MLA_HEADS):
        sl = slice(hd * HEAD_SLAB, (hd + 1) * HEAD_SLAB)
        q_out[0, hd] = (qm[:, sl] * cos_q + qs[:, sl] * sin_q).astype(BF16)
        k_out[0, hd] = (kk[:, sl] + kr).astype(BF16)
        vt = lax.dot_general(wvt_ref[hd], kvn, _NT, preferred_element_type=F32)
        vt_out[0, hd] = (vt + ones_row).astype(BF16)
    gm_out[0] = _silu(proj[:, _C_GM:_C_GM + MLA_WIDTH])

    a0 = lbraw_ref[0:1, :]
    a1 = lbraw_ref[1:2, :]
    mx = jnp.maximum(a0, a1)
    e0 = jnp.exp(a0 - mx)
    e1 = jnp.exp(a1 - mx)
    lb = e0 / (e0 + e1)
    hf_out[0] = lb + (1.0 - lb) * jax.nn.sigmoid(proj[:, _C_HF:_C_HF + HG_FDIM])
    hq_out[0] = _silu(proj[:, _C_HQ:_C_HQ + HG_FDIM])
    hv_out[0] = proj[:, _C_HI:_C_HI + HG_WIDTH].astype(BF16)
    gh_out[0] = _silu(proj[:, _C_GH:_C_GH + HG_WIDTH])


def _proj_call(x, pos3, lng, win, qg, wqm, wqs, kvg, wk, wvt, lbraw, rope_tab):
    B, S, _ = x.shape
    tm = PROJ_TM
    grid = (B, S // tm)
    tok = lambda w: pl.BlockSpec((1, tm, w), lambda b, i: (b, i, 0))
    full = lambda a: pl.BlockSpec(a.shape, lambda b, i: (0,) * a.ndim)
    head = pl.BlockSpec((1, MLA_HEADS, tm, HEAD_SLAB), lambda b, i: (b, 0, i, 0))
    out_shape = (
        jax.ShapeDtypeStruct((B, MLA_HEADS, S, HEAD_SLAB), BF16),
        jax.ShapeDtypeStruct((B, MLA_HEADS, S, HEAD_SLAB), BF16),
        jax.ShapeDtypeStruct((B, MLA_HEADS, VT_ROWS, S), BF16),
        jax.ShapeDtypeStruct((B, S, MLA_WIDTH), F32),
        jax.ShapeDtypeStruct((B, S, HG_FDIM), F32),
        jax.ShapeDtypeStruct((B, S, HG_FDIM), F32),
        jax.ShapeDtypeStruct((B, S, HG_WIDTH), BF16),
        jax.ShapeDtypeStruct((B, S, HG_WIDTH), F32),
    )
    return pl.pallas_call(
        _proj_kernel,
        grid=grid,
        in_specs=[tok(D_MODEL), tok(1), full(lng), full(win), full(qg), full(wqm), full(wqs),
                  full(kvg), full(wk), full(wvt), full(lbraw), full(rope_tab)],
        out_specs=(head, head,
                   pl.BlockSpec((1, MLA_HEADS, VT_ROWS, tm), lambda b, i: (b, 0, 0, i)),
                   tok(MLA_WIDTH), tok(HG_FDIM), tok(HG_FDIM),
                   tok(HG_WIDTH), tok(HG_WIDTH)),
        out_shape=out_shape,
        compiler_params=pltpu.CompilerParams(
            dimension_semantics=("parallel", "parallel"), vmem_limit_bytes=VMEM_LIMIT),
        name="proj",
    )(x, pos3, lng, win, qg, wqm, wqs, kvg, wk, wvt, lbraw, rope_tab)


def _attn_kernel(q_ref, k_ref, vt_ref, g_ref, o_ref, s_scr, p_scr, m_scr, acc_scr):
    qi = pl.program_id(2)
    tq, tk, nh, cw = ATT_TQ, ATT_TK, ATT_NH, ATT_CW
    m_scr[...] = jnp.full(m_scr.shape, NEG_BIG, F32)
    acc_scr[...] = jnp.zeros(acc_scr.shape, F32)

    def scores(h, r0, c0):
        k = k_ref[0, h, pl.ds(r0, tk), :]
        q = q_ref[0, h, c0:, :]
        s_scr[h, :, c0:] = lax.dot_general(k, q, _NT, preferred_element_type=F32)

    def softmax_pv(h, r0, c0, masked):
        alphas = []
        for c in range(c0, tq, cw):
            s = s_scr[h, :, c:c + cw]
            if masked and c - c0 < tk:
                key = lax.broadcasted_iota(jnp.int32, (tk, cw), 0)
                qry = lax.broadcasted_iota(jnp.int32, (tk, cw), 1) + (c - c0)
                s = jnp.where(key <= qry, s, NEG_BIG)
            m_old = m_scr[h, :, c:c + cw]
            m_new = jnp.maximum(m_old, jnp.max(s, axis=0, keepdims=True))
            p_scr[h, :, c:c + cw] = jnp.exp2(s - m_new).astype(BF16)
            alphas.append(jnp.exp2(m_old - m_new))
            m_scr[h, :, c:c + cw] = m_new
        alpha = jnp.concatenate(alphas, axis=1) if len(alphas) > 1 else alphas[0]
        vt = vt_ref[0, h, :, pl.ds(r0, tk)]
        pv = jnp.dot(vt, p_scr[h, :, c0:], preferred_element_type=F32)
        acc_scr[h, :, c0:] = alpha * acc_scr[h, :, c0:] + pv

    def step(r0, c0, masked):
        for h in range(nh):
            scores(h, r0, c0)
        for h in range(nh):
            softmax_pv(h, r0, c0, masked)

    def full_step(kb, carry):
        step(pl.multiple_of(kb * tk, tk), 0, False)
        return carry

    lax.fori_loop(0, qi * (tq // tk), full_step, 0)
    for j in range(tq // tk):
        step(pl.multiple_of(qi * tq + j * tk, tk), j * tk, True)

    outs = []
    for h in range(nh):
        acc = acc_scr[h]
        o = acc[:MLA_V] / acc[MLA_V:MLA_V + 1]
        outs.append(o.T)
    o_ref[0] = (jnp.concatenate(outs, axis=1) * g_ref[0]).astype(BF16)


def _attn_call(q, k, vt, gm):
    B, H, S, _ = q.shape
    tq, nh = ATT_TQ, ATT_NH
    grid = (B, H // nh, S // tq)
    return pl.pallas_call(
        _attn_kernel,
        grid=grid,
        in_specs=[
            pl.BlockSpec((1, nh, tq, HEAD_SLAB), lambda b, j, i: (b, j, i, 0)),
            pl.BlockSpec((1, nh, S, HEAD_SLAB), lambda b, j, i: (b, j, 0, 0)),
            pl.BlockSpec((1, nh, VT_ROWS, S), lambda b, j, i: (b, j, 0, 0)),
            pl.BlockSpec((1, tq, nh * MLA_V), lambda b, j, i: (b, i, j)),
        ],
        out_specs=pl.BlockSpec((1, tq, nh * MLA_V), lambda b, j, i: (b, i, j)),
        out_shape=jax.ShapeDtypeStruct((B, S, MLA_WIDTH), BF16),
        scratch_shapes=[
            pltpu.VMEM((nh, ATT_TK, tq), F32),
            pltpu.VMEM((nh, ATT_TK, tq), BF16),
            pltpu.VMEM((nh, 1, tq), F32),
            pltpu.VMEM((nh, VT_ROWS, tq), F32),
        ],
        compiler_params=pltpu.CompilerParams(
            dimension_semantics=("parallel", "parallel", "arbitrary"),
            vmem_limit_bytes=VMEM_LIMIT),
        name="attn",
    )(q, k, vt, gm)


def _hgrn_constants():
    C, SUB, NL = HG_CHUNK, HG_SUB, HG_LEVELS
    t = np.arange(C)
    lm = np.zeros((2 * NL, C, C), np.float32)
    masks = np.zeros((NL, C, C), np.float32)
    masks[0] = (t[:, None] // SUB) == (t[None, :] // SUB)
    for l in range(NL):
        bs = SUB << l
        same = (t[:, None] // bs) == (t[None, :] // bs)
        lm[l] = same & (t[None, :] <= t[:, None])
        lm[NL + l] = same & (t[None, :] > t[:, None])
        if l < NL - 1:
            masks[l + 1] = ((t[:, None] // bs) == (t[None, :] // bs) + 1) & ((t[None, :] // bs) % 2 == 0)
    lmat = lm.reshape(2 * NL * C, C)
    lmat = np.concatenate([lmat, lmat], axis=1)
    j = np.arange(C)
    sel = np.zeros((SUB, LANES, C), np.float32)
    sel[:] = (j[None, None, :] % SUB) == np.arange(SUB)[:, None, None]
    return (jnp.asarray(lmat, BF16), jnp.asarray(sel.reshape(SUB * LANES, C), BF16),
            jnp.asarray(masks, F32))


def _hgrn_kernel(f_ref, q_ref, v_ref, g_ref, ng_ref, lmat_ref, sel_ref, mask_ref, o_ref,
                 st_scr, qs_scr, ks_scr, c8_scr, k3_scr, p2_scr, a_scr, ds_scr, oi_scr, dec_scr):
    C, SUB, NL, TB, NH = HG_CHUNK, HG_SUB, HG_LEVELS, HG_TB, HG_HEADS
    NC = TB // C

    @pl.when(pl.program_id(1) == 0)
    def _():
        st_scr[...] = jnp.zeros_like(st_scr)

    f = f_ref[0]
    q = q_ref[0]
    g = jnp.log(f)
    k = 1.0 - f
    g1 = g.astype(BF16)
    g2 = (g - g1.astype(F32)).astype(BF16)
    lmat = lmat_ref[...]

    for c in range(NC):
        rows = slice(c * C, (c + 1) * C)
        e = jnp.dot(lmat, jnp.concatenate([g1[rows], g2[rows]], axis=0),
                    preferred_element_type=F32)
        for l in range(NL):
            cq = e[l * C:(l + 1) * C]
            ck = e[(NL + l) * C:(NL + l + 1) * C]
            qs_scr[l, rows, :] = (q[rows] * jnp.exp(cq)).astype(BF16)
            ks_scr[l, rows, :] = (k[rows] * jnp.exp(ck)).astype(BF16)
            if l == 0:
                c8_scr[c * (C // SUB):(c + 1) * (C // SUB)] = cq.reshape(C // SUB, SUB, NH * LANES)
            if l == NL - 1:
                dec_scr[c:c + 1, :] = jnp.exp(cq[C - 1:C, :])
    k3_scr[...] = k.reshape(TB // SUB, SUB, NH * LANES)

    q3 = q.reshape(TB // SUB, SUB, NH * LANES)
    c8 = c8_scr[...]
    tt = lax.broadcasted_iota(jnp.int32, c8.shape, 1)
    for s in range(SUB):
        dg = jnp.where(tt >= s, c8 - c8_scr[:, s:s + 1, :], NEG_BIG)
        p = (q3 * jnp.exp(dg) * k3_scr[:, s:s + 1, :]).reshape(TB, NH * LANES).astype(BF16)
        for h in range(NH):
            p2_scr[h, :, s * LANES:(s + 1) * LANES] = p[:, h * LANES:(h + 1) * LANES]

    v = v_ref[0]
    for h in range(NH):
        hs = slice(h * LANES, (h + 1) * LANES)
        ad = jnp.dot(p2_scr[h], sel_ref[...], preferred_element_type=F32)
        for c in range(NC):
            rows = slice(c * C, (c + 1) * C)
            a = ad[rows] * mask_ref[0]
            for l in range(NL - 1):
                al = lax.dot_general(qs_scr[l, rows, hs], ks_scr[l, rows, hs], _NT,
                                     preferred_element_type=F32)
                a = a + al * mask_ref[l + 1]
            a_scr[h, rows, :] = a.astype(BF16)
    for h in range(NH):
        hs = slice(h * LANES, (h + 1) * LANES)
        for c in range(NC):
            rows = slice(c * C, (c + 1) * C)
            oi_scr[rows, hs] = jnp.dot(a_scr[h, rows, :], v[rows, hs], preferred_element_type=F32)
            ds_scr[h * NC + c] = lax.dot_general(v[rows, hs], ks_scr[NL - 1, rows, hs], _TN,
                                                 preferred_element_type=F32)

    for c in range(NC):
        rows = slice(c * C, (c + 1) * C)
        for h in range(NH):
            hs = slice(h * LANES, (h + 1) * LANES)
            st = st_scr[h]
            o = oi_scr[rows, hs] + lax.dot_general(qs_scr[NL - 1, rows, hs], st.astype(BF16), _NT,
                                                   preferred_element_type=F32)
            st_scr[h] = st * dec_scr[c:c + 1, hs] + ds_scr[h * NC + c]
            o = o * lax.rsqrt(jnp.mean(o * o, axis=-1, keepdims=True) + EPS) * ng_ref[:, hs]
            o_ref[0, rows, hs] = (o * g_ref[0, rows, hs]).astype(BF16)


def _hgrn_call(hf, hq, hv, gh, ng):
    B, S, W = hf.shape
    tb, C, SUB, NL, NH = HG_TB, HG_CHUNK, HG_SUB, HG_LEVELS, HG_HEADS
    assert SUB << (NL - 1) == C
    lmat, sel, masks = _hgrn_constants()
    grid = (B, S // tb)
    blk = pl.BlockSpec((1, tb, W), lambda b, t: (b, t, 0))
    full = lambda a: pl.BlockSpec(a.shape, lambda b, t: (0,) * a.ndim)
    return pl.pallas_call(
        _hgrn_kernel,
        grid=grid,
        in_specs=[blk, blk, blk, blk, full(ng), full(lmat), full(sel), full(masks)],
        out_specs=blk,
        out_shape=jax.ShapeDtypeStruct((B, S, HG_WIDTH), BF16),
        scratch_shapes=[
            pltpu.VMEM((NH, HG_HEAD_V, HG_EXPAND), F32),
            pltpu.VMEM((NL, tb, W), BF16),
            pltpu.VMEM((NL, tb, W), BF16),
            pltpu.VMEM((tb // SUB, SUB, W), F32),
            pltpu.VMEM((tb // SUB, SUB, W), F32),
            pltpu.VMEM((NH, tb, SUB * LANES), BF16),
            pltpu.VMEM((NH, tb, C), BF16),
            pltpu.VMEM((NH * (tb // C), HG_HEAD_V, HG_EXPAND), F32),
            pltpu.VMEM((tb, W), F32),
            pltpu.VMEM((tb // C, W), F32),
        ],
        compiler_params=pltpu.CompilerParams(
            dimension_semantics=("parallel", "arbitrary"),
            vmem_limit_bytes=VMEM_LIMIT),
        name="hgrn",
    )(hf, hq, hv, gh, ng, lmat, sel, masks)


def _out_kernel(ya_ref, yh_ref, x_ref, wa_ref, wh_ref, fg_ref, o_ref):
    y = jnp.dot(ya_ref[0], wa_ref[...], preferred_element_type=F32)
    y = y + jnp.dot(yh_ref[0], wh_ref[...], preferred_element_type=F32)
    o_ref[0] = _rms(x_ref[0] + y, fg_ref[...])


def _out_call(ya, yh, x, wa, wh, fg):
    B, S, _ = x.shape
    tm = OUT_TM
    grid = (B, S // tm)
    tok = lambda w: pl.BlockSpec((1, tm, w), lambda b, i: (b, i, 0))
    full = lambda a: pl.BlockSpec(a.shape, lambda b, i: (0,) * a.ndim)
    return pl.pallas_call(
        _out_kernel,
        grid=grid,
        in_specs=[tok(MLA_WIDTH), tok(HG_WIDTH), tok(D_MODEL), full(wa), full(wh), full(fg)],
        out_specs=tok(D_MODEL),
        out_shape=jax.ShapeDtypeStruct((B, S, D_MODEL), F32),
        compiler_params=pltpu.CompilerParams(
            dimension_semantics=("parallel", "parallel"), vmem_limit_bytes=VMEM_LIMIT),
        name="outproj",
    )(ya, yh, x, wa, wh, fg)


def _pack_weights(w_in, w_q_b, w_kv_b):
    o = 0
    w_qlat = w_in[:, o:o + MLA_Q_RANK]; o += MLA_Q_RANK
    w_kvlat = w_in[:, o:o + MLA_KV_RANK]; o += MLA_KV_RANK
    w_kr = w_in[:, o:o + MLA_ROPE]; o += MLA_ROPE
    w_rest = w_in[:, o:]
    z = lambda n: jnp.zeros((w_in.shape[0], n), w_in.dtype)
    x1, x2 = w_kr[:, :HALF], w_kr[:, HALF:]
    kr_main = jnp.concatenate([z(MLA_NOPE), x1, x2, z(LANES - MLA_NOPE - MLA_ROPE)], axis=1)
    kr_swap = jnp.concatenate([z(MLA_NOPE), x2, x1, z(LANES - MLA_NOPE - MLA_ROPE)], axis=1)
    win = jnp.concatenate([w_qlat, w_kvlat, kr_main, kr_swap, w_rest], axis=1).astype(BF16)
    assert win.shape[1] == _C_END

    wq = w_q_b.reshape(MLA_Q_RANK, MLA_HEADS, MLA_NOPE + MLA_ROPE)
    qn_, q1, q2 = wq[..., :MLA_NOPE], wq[..., MLA_NOPE:MLA_NOPE + HALF], wq[..., MLA_NOPE + HALF:]
    zq = lambda n: jnp.zeros((MLA_Q_RANK, MLA_HEADS, n), wq.dtype)
    pad = LANES - MLA_NOPE - MLA_ROPE
    wqm = jnp.concatenate([qn_, q1, q2, zq(pad)], axis=-1).reshape(MLA_Q_RANK, -1).astype(BF16)
    wqs = jnp.concatenate([zq(MLA_NOPE), q2, q1, zq(pad)], axis=-1).reshape(MLA_Q_RANK, -1).astype(BF16)

    wkv = w_kv_b.reshape(MLA_KV_RANK, MLA_HEADS, MLA_NOPE + MLA_V)
    wk = jnp.concatenate([wkv[..., :MLA_NOPE],
                          jnp.zeros((MLA_KV_RANK, MLA_HEADS, LANES - MLA_NOPE), wkv.dtype)],
                         axis=-1).reshape(MLA_KV_RANK, -1).astype(BF16)
    wvt = jnp.transpose(wkv[..., MLA_NOPE:], (1, 2, 0))
    wvt = jnp.concatenate([wvt, jnp.zeros((MLA_HEADS, VT_ROWS - MLA_V, MLA_KV_RANK), wvt.dtype)],
                          axis=1).astype(BF16)
    return win, wqm, wqs, wk, wvt


def _rope_table():
    inv = ROPE_THETA ** (-jnp.arange(HALF, dtype=F32) / HALF)
    zl = jnp.zeros((MLA_NOPE,), F32)
    zr = jnp.zeros((LANES - MLA_NOPE - MLA_ROPE,), F32)
    inv_lane = jnp.concatenate([zl, inv, inv, zr])
    sign = jnp.concatenate([zl, -jnp.ones((HALF,), F32), jnp.ones((HALF,), F32), zr])
    return jnp.stack([inv_lane, sign])


def kernel(x, positions, ln_g, w_in, q_a_norm_g, w_q_b, kv_a_norm_g, w_kv_b,
           hg_lower_bounds, hg_norm_g, w_out, final_norm_g):
    B, S, _ = x.shape
    assert ln_g.shape[0] == 1, "single-layer stack"
    win, wqm, wqs, wk, wvt = _pack_weights(w_in[0], w_q_b[0], w_kv_b[0])
    pos3 = positions.reshape(B, S, 1)
    q, k, vt, gm, hq, hf, hv, gh = _proj_call(
        x, pos3, ln_g[0:1], win, q_a_norm_g[0:1], wqm, wqs, kv_a_norm_g[0:1], wk, wvt,
        hg_lower_bounds, _rope_table())
    ya = _attn_call(q, k, vt, gm)
    yh = _hgrn_call(hf, hq, hv, gh, hg_norm_g[0:1])
    wo = w_out[0].astype(BF16)
    return _out_call(ya, yh, x, wo[:MLA_WIDTH], wo[MLA_WIDTH:], final_norm_g.reshape(1, D_MODEL))
```

```python
import functools
import math

import numpy as np
import jax
import jax.numpy as jnp
from jax import lax
from jax.experimental import pallas as pl
from jax.experimental.pallas import tpu as pltpu

F32 = jnp.float32
BF16 = jnp.bfloat16

D_MODEL = 1024
MLA_HEADS = 8
MLA_NOPE = 64
MLA_ROPE = 32
MLA_V = 64
MLA_Q_RANK = 256
MLA_KV_RANK = 128
MLA_WIDTH = MLA_HEADS * MLA_V
HG_HEADS = 4
HG_EXPAND = 128
HG_HEAD_V = 128
HG_WIDTH = HG_HEADS * HG_HEAD_V
HG_FDIM = HG_HEADS * HG_EXPAND
ROPE_THETA = 10000.0
EPS = 1e-6
HALF = MLA_ROPE // 2

LANES = 128
HEAD_SLAB = LANES
NEG_BIG = -1e30
LOG2E = math.log2(math.e)
VT_ROWS = 80

PROJ_TM = 256
ATT_TQ = 1024
ATT_TK = 256
ATT_NH = 8
ATT_CW = 256
HG_CHUNK = 64
HG_SUB = 8
HG_LEVELS = 4
HG_TB = 256
OUT_TM = 512
VMEM_LIMIT = 48 * 1024 * 1024

_C_QLAT = 0
_C_KVLAT = _C_QLAT + MLA_Q_RANK
_C_KRM = _C_KVLAT + MLA_KV_RANK
_C_KRS = _C_KRM + LANES
_C_GM = _C_KRS + LANES
_C_HQ = _C_GM + MLA_WIDTH
_C_HF = _C_HQ + HG_FDIM
_C_HI = _C_HF + HG_FDIM
_C_GH = _C_HI + HG_WIDTH
_C_END = _C_GH + HG_WIDTH

_NT = (((1,), (1,)), ((), ()))
_TN = (((0,), (0,)), ((), ()))


def _rms(x, g):
    return x * lax.rsqrt(jnp.mean(x * x, axis=-1, keepdims=True) + EPS) * g


def _silu(x):
    return x * jax.nn.sigmoid(x)


def _proj_kernel(x_ref, pos_ref, lng_ref, win_ref, qg_ref, wqm_ref, wqs_ref,
                 kvg_ref, wk_ref, wvt_ref, lbraw_ref, rope_ref,
                 q_out, k_out, vt_out, gm_out, hq_out, hf_out, hv_out, gh_out):
    x = x_ref[0]
    h = _rms(x, lng_ref[...]).astype(BF16)
    proj = jnp.dot(h, win_ref[...], preferred_element_type=F32)

    pos = pos_ref[0].astype(F32)
    ang = pos * rope_ref[0:1, :]
    cos = jnp.cos(ang)
    sin = jnp.sin(ang) * rope_ref[1:2, :]
    scale = LOG2E / math.sqrt(MLA_NOPE + MLA_ROPE)
    cos_q = cos * scale
    sin_q = sin * scale

    qn = _rms(proj[:, _C_QLAT:_C_QLAT + MLA_Q_RANK], qg_ref[...]).astype(BF16)
    qm = jnp.dot(qn, wqm_ref[...], preferred_element_type=F32)
    qs = jnp.dot(qn, wqs_ref[...], preferred_element_type=F32)
    kvn = _rms(proj[:, _C_KVLAT:_C_KVLAT + MLA_KV_RANK], kvg_ref[...]).astype(BF16)
    kk = jnp.dot(kvn, wk_ref[...], preferred_element_type=F32)
    kr = proj[:, _C_KRM:_C_KRM + LANES] * cos + proj[:, _C_KRS:_C_KRS + LANES] * sin
    ones_row = (lax.broadcasted_iota(jnp.int32, (VT_ROWS, 1), 0) == MLA_V).astype(F32)
    for hd in range(MLA_HEADS):
        sl = slice(hd * HEAD_SLAB, (hd + 1) * HEAD_SLAB)
        q_out[0, hd] = (qm[:, sl] * cos_q + qs[:, sl] * sin_q).astype(BF16)
        k_out[0, hd] = (kk[:, sl] + kr).astype(BF16)
        vt = lax.dot_general(wvt_ref[hd], kvn, _NT, preferred_element_type=F32)
        vt_out[0, hd] = (vt + ones_row).astype(BF16)
    gm_out[0] = _silu(proj[:, _C_GM:_C_GM + MLA_WIDTH])

    a0 = lbraw_ref[0:1, :]
    a1 = lbraw_ref[1:2, :]
    mx = jnp.maximum(a0, a1)
    e0 = jnp.exp(a0 - mx)
    e1 = jnp.exp(a1 - mx)
    lb = e0 / (e0 + e1)
    hf_out[0] = lb + (1.0 - lb) * jax.nn.sigmoid(proj[:, _C_HF:_C_HF + HG_FDIM])
    hq_out[0] = _silu(proj[:, _C_HQ:_C_HQ + HG_FDIM])
    hv_out[0] = proj[:, _C_HI:_C_HI + HG_WIDTH].astype(BF16)
    gh_out[0] = _silu(proj[:, _C_GH:_C_GH + HG_WIDTH])


def _proj_call(x, pos3, lng, win, qg, wqm, wqs, kvg, wk, wvt, lbraw, rope_tab):
    B, S, _ = x.shape
    tm = PROJ_TM
    grid = (B, S // tm)
    tok = lambda w: pl.BlockSpec((1, tm, w), lambda b, i: (b, i, 0))
    full = lambda a: pl.BlockSpec(a.shape, lambda b, i: (0,) * a.ndim)
    head = pl.BlockSpec((1, MLA_HEADS, tm, HEAD_SLAB), lambda b, i: (b, 0, i, 0))
    out_shape = (
        jax.ShapeDtypeStruct((B, MLA_HEADS, S, HEAD_SLAB), BF16),
        jax.ShapeDtypeStruct((B, MLA_HEADS, S, HEAD_SLAB), BF16),
        jax.ShapeDtypeStruct((B, MLA_HEADS, VT_ROWS, S), BF16),
        jax.ShapeDtypeStruct((B, S, MLA_WIDTH), F32),
        jax.ShapeDtypeStruct((B, S, HG_FDIM), F32),
        jax.ShapeDtypeStruct((B, S, HG_FDIM), F32),
        jax.ShapeDtypeStruct((B, S, HG_WIDTH), BF16),
        jax.ShapeDtypeStruct((B, S, HG_WIDTH), F32),
    )
    return pl.pallas_call(
        _proj_kernel,
        grid=grid,
        in_specs=[tok(D_MODEL), tok(1), full(lng), full(win), full(qg), full(wqm), full(wqs),
                  full(kvg), full(wk), full(wvt), full(lbraw), full(rope_tab)],
        out_specs=(head, head,
                   pl.BlockSpec((1, MLA_HEADS, VT_ROWS, tm), lambda b, i: (b, 0, 0, i)),
                   tok(MLA_WIDTH), tok(HG_FDIM), tok(HG_FDIM),
                   tok(HG_WIDTH), tok(HG_WIDTH)),
        out_shape=out_shape,
        compiler_params=pltpu.CompilerParams(
            dimension_semantics=("parallel", "parallel"), vmem_limit_bytes=VMEM_LIMIT),
        name="proj",
    )(x, pos3, lng, win, qg, wqm, wqs, kvg, wk, wvt, lbraw, rope_tab)


def _attn_kernel(q_ref, k_ref, vt_ref, g_ref, o_ref, s_scr, m_scr, acc_scr):
    qi = pl.program_id(2)
    tq, tk, nh, cw = ATT_TQ, ATT_TK, ATT_NH, ATT_CW
    m_scr[...] = jnp.full(m_scr.shape, NEG_BIG, F32)
    acc_scr[...] = jnp.zeros(acc_scr.shape, F32)

    def scores(h, r0, c0, masked):
        k = k_ref[0, h, pl.ds(r0, tk), :]
        q = q_ref[0, h, c0:, :]
        s = lax.dot_general(k, q, _NT, preferred_element_type=F32)
        for c in range(c0, tq, cw):
            blk = s[:, c - c0:c - c0 + cw]
            if masked and c - c0 < tk:
                key = lax.broadcasted_iota(jnp.int32, (tk, cw), 0)
                qry = lax.broadcasted_iota(jnp.int32, (tk, cw), 1) + (c - c0)
                blk = jnp.where(key <= qry, blk, NEG_BIG)
            s_scr[h, c // cw] = blk

    def softmax_pv(h, r0, c0):
        vt = vt_ref[0, h, :, pl.ds(r0, tk)]
        for c in range(c0, tq, cw):
            j = c // cw
            m_old = m_scr[h, :, c:c + cw]
            m_new = jnp.maximum(m_old, jnp.max(s_scr[h, j], axis=0, keepdims=True))
            p = jnp.exp2(s_scr[h, j] - m_new).astype(BF16)
            m_scr[h, :, c:c + cw] = m_new
            pv = jnp.dot(vt, p, preferred_element_type=F32)
            acc_scr[h, :, c:c + cw] = jnp.exp2(m_old - m_new) * acc_scr[h, :, c:c + cw] + pv

    def step(r0, c0, masked):
        for h in range(nh):
            scores(h, r0, c0, masked)
        for h in range(nh):
            softmax_pv(h, r0, c0)

    def full_step(kb, carry):
        step(pl.multiple_of(kb * tk, tk), 0, False)
        return carry

    lax.fori_loop(0, qi * (tq // tk), full_step, 0)
    for j in range(tq // tk):
        step(pl.multiple_of(qi * tq + j * tk, tk), j * tk, True)

    outs = []
    for h in range(nh):
        acc = acc_scr[h]
        o = acc[:MLA_V] / acc[MLA_V:MLA_V + 1]
        outs.append(o.T)
    o_ref[0] = (jnp.concatenate(outs, axis=1) * g_ref[0]).astype(BF16)


def _attn_call(q, k, vt, gm):
    B, H, S, _ = q.shape
    tq, nh = ATT_TQ, ATT_NH
    grid = (B, H // nh, S // tq)
    return pl.pallas_call(
        _attn_kernel,
        grid=grid,
        in_specs=[
            pl.BlockSpec((1, nh, tq, HEAD_SLAB), lambda b, j, i: (b, j, i, 0)),
            pl.BlockSpec((1, nh, S, HEAD_SLAB), lambda b, j, i: (b, j, 0, 0),
                         pipeline_mode=pl.Buffered(1)),
            pl.BlockSpec((1, nh, VT_ROWS, S), lambda b, j, i: (b, j, 0, 0),
                         pipeline_mode=pl.Buffered(1)),
            pl.BlockSpec((1, tq, nh * MLA_V), lambda b, j, i: (b, i, j)),
        ],
        out_specs=pl.BlockSpec((1, tq, nh * MLA_V), lambda b, j, i: (b, i, j)),
        out_shape=jax.ShapeDtypeStruct((B, S, MLA_WIDTH), BF16),
        scratch_shapes=[
            pltpu.VMEM((nh, tq // ATT_CW, ATT_TK, ATT_CW), F32),
            pltpu.VMEM((nh, 1, tq), F32),
            pltpu.VMEM((nh, VT_ROWS, tq), F32),
        ],
        compiler_params=pltpu.CompilerParams(
            dimension_semantics=("parallel", "parallel", "arbitrary"),
            vmem_limit_bytes=VMEM_LIMIT),
        name="attn",
    )(q, k, vt, gm)


def _hgrn_constants():
    C, SUB, NL = HG_CHUNK, HG_SUB, HG_LEVELS
    t = np.arange(C)
    lm = np.zeros((2 * NL, C, C), np.float32)
    masks = np.zeros((NL, C, C), np.float32)
    masks[0] = (t[:, None] // SUB) == (t[None, :] // SUB)
    for l in range(NL):
        bs = SUB << l
        same = (t[:, None] // bs) == (t[None, :] // bs)
        lm[l] = same & (t[None, :] <= t[:, None])
        lm[NL + l] = same & (t[None, :] > t[:, None])
        if l < NL - 1:
            masks[l + 1] = ((t[:, None] // bs) == (t[None, :] // bs) + 1) & ((t[None, :] // bs) % 2 == 0)
    lmat = lm.reshape(2 * NL * C, C)
    lmat = np.concatenate([lmat, lmat], axis=1)
    j = np.arange(C)
    sel = np.zeros((SUB, LANES, C), np.float32)
    sel[:] = (j[None, None, :] % SUB) == np.arange(SUB)[:, None, None]
    return (jnp.asarray(lmat, BF16), jnp.asarray(sel.reshape(SUB * LANES, C), BF16),
            jnp.asarray(masks, F32))


def _hgrn_kernel(f_ref, q_ref, v_ref, g_ref, ng_ref, lmat_ref, sel_ref, mask_ref, o_ref,
                 st_scr, qs_scr, ks_scr, c8_scr, k3_scr, p2_scr, a_scr, ds_scr, oi_scr, dec_scr):
    C, SUB, NL, TB, NH = HG_CHUNK, HG_SUB, HG_LEVELS, HG_TB, HG_HEADS
    NC = TB // C

    @pl.when(pl.program_id(1) == 0)
    def _():
        st_scr[...] = jnp.zeros_like(st_scr)

    f = f_ref[0]
    q = q_ref[0]
    g = jnp.log(f)
    k = 1.0 - f
    g1 = g.astype(BF16)
    g2 = (g - g1.astype(F32)).astype(BF16)
    lmat = lmat_ref[...]

    for c in range(NC):
        rows = slice(c * C, (c + 1) * C)
        e = jnp.dot(lmat, jnp.concatenate([g1[rows], g2[rows]], axis=0),
                    preferred_element_type=F32)
        for l in range(NL):
            cq = e[l * C:(l + 1) * C]
            ck = e[(NL + l) * C:(NL + l + 1) * C]
            qs_scr[l, rows, :] = (q[rows] * jnp.exp(cq)).astype(BF16)
            ks_scr[l, rows, :] = (k[rows] * jnp.exp(ck)).astype(BF16)
            if l == 0:
                c8_scr[c * (C // SUB):(c + 1) * (C // SUB)] = cq.reshape(C // SUB, SUB, NH * LANES)
            if l == NL - 1:
                dec_scr[c:c + 1, :] = jnp.exp(cq[C - 1:C, :])
    k3_scr[...] = k.reshape(TB // SUB, SUB, NH * LANES)

    q3 = q.reshape(TB // SUB, SUB, NH * LANES)
    c8 = c8_scr[...]
    tt = lax.broadcasted_iota(jnp.int32, c8.shape, 1)
    for s in range(SUB):
        dg = jnp.where(tt >= s, c8 - c8_scr[:, s:s + 1, :], NEG_BIG)
        p = (q3 * jnp.exp(dg) * k3_scr[:, s:s + 1, :]).reshape(TB, NH * LANES).astype(BF16)
        for h in range(NH):
            p2_scr[h, :, s * LANES:(s + 1) * LANES] = p[:, h * LANES:(h + 1) * LANES]

    v = v_ref[0]
    for h in range(NH):
        hs = slice(h * LANES, (h + 1) * LANES)
        ad = jnp.dot(p2_scr[h], sel_ref[...], preferred_element_type=F32)
        for c in range(NC):
            rows = slice(c * C, (c + 1) * C)
            a = ad[rows] * mask_ref[0]
            for l in range(NL - 1):
                al = lax.dot_general(qs_scr[l, rows, hs], ks_scr[l, rows, hs], _NT,
                                     preferred_element_type=F32)
                a = a + al * mask_ref[l + 1]
            a_scr[h, rows, :] = a.astype(BF16)
    for h in range(NH):
        hs = slice(h * LANES, (h + 1) * LANES)
        for c in range(NC):
            rows = slice(c * C, (c + 1) * C)
            oi_scr[rows, hs] = jnp.dot(a_scr[h, rows, :], v[rows, hs], preferred_element_type=F32)
            ds_scr[h * NC + c] = lax.dot_general(v[rows, hs], ks_scr[NL - 1, rows, hs], _TN,
                                                 preferred_element_type=F32)

    for c in range(NC):
        rows = slice(c * C, (c + 1) * C)
        for h in range(NH):
            hs = slice(h * LANES, (h + 1) * LANES)
            st = st_scr[h]
            o = oi_scr[rows, hs] + lax.dot_general(qs_scr[NL - 1, rows, hs], st.astype(BF16), _NT,
                                                   preferred_element_type=F32)
            st_scr[h] = st * dec_scr[c:c + 1, hs] + ds_scr[h * NC + c]
            o = o * lax.rsqrt(jnp.mean(o * o, axis=-1, keepdims=True) + EPS) * ng_ref[:, hs]
            o_ref[0, rows, hs] = (o * g_ref[0, rows, hs]).astype(BF16)


def _hgrn_call(hf, hq, hv, gh, ng):
    B, S, W = hf.shape
    tb, C, SUB, NL, NH = HG_TB, HG_CHUNK, HG_SUB, HG_LEVELS, HG_HEADS
    assert SUB << (NL - 1) == C
    lmat, sel, masks = _hgrn_constants()
    grid = (B, S // tb)
    blk = pl.BlockSpec((1, tb, W), lambda b, t: (b, t, 0))
    full = lambda a: pl.BlockSpec(a.shape, lambda b, t: (0,) * a.ndim)
    return pl.pallas_call(
        _hgrn_kernel,
        grid=grid,
        in_specs=[blk, blk, blk, blk, full(ng), full(lmat), full(sel), full(masks)],
        out_specs=blk,
        out_shape=jax.ShapeDtypeStruct((B, S, HG_WIDTH), BF16),
        scratch_shapes=[
            pltpu.VMEM((NH, HG_HEAD_V, HG_EXPAND), F32),
            pltpu.VMEM((NL, tb, W), BF16),
            pltpu.VMEM((NL, tb, W), BF16),
            pltpu.VMEM((tb // SUB, SUB, W), F32),
            pltpu.VMEM((tb // SUB, SUB, W), F32),
            pltpu.VMEM((NH, tb, SUB * LANES), BF16),
            pltpu.VMEM((NH, tb, C), BF16),
            pltpu.VMEM((NH * (tb // C), HG_HEAD_V, HG_EXPAND), F32),
            pltpu.VMEM((tb, W), F32),
            pltpu.VMEM((tb // C, W), F32),
        ],
        compiler_params=pltpu.CompilerParams(
            dimension_semantics=("parallel", "arbitrary"),
            vmem_limit_bytes=VMEM_LIMIT),
        name="hgrn",
    )(hf, hq, hv, gh, ng, lmat, sel, masks)


def _out_kernel(ya_ref, yh_ref, x_ref, wa_ref, wh_ref, fg_ref, o_ref):
    y = jnp.dot(ya_ref[0], wa_ref[...], preferred_element_type=F32)
    y = y + jnp.dot(yh_ref[0], wh_ref[...], preferred_element_type=F32)
    o_ref[0] = _rms(x_ref[0] + y, fg_ref[...])


def _out_call(ya, yh, x, wa, wh, fg):
    B, S, _ = x.shape
    tm = OUT_TM
    grid = (B, S // tm)
    tok = lambda w: pl.BlockSpec((1, tm, w), lambda b, i: (b, i, 0))
    full = lambda a: pl.BlockSpec(a.shape, lambda b, i: (0,) * a.ndim)
    return pl.pallas_call(
        _out_kernel,
        grid=grid,
        in_specs=[tok(MLA_WIDTH), tok(HG_WIDTH), tok(D_MODEL), full(wa), full(wh), full(fg)],
        out_specs=tok(D_MODEL),
        out_shape=jax.ShapeDtypeStruct((B, S, D_MODEL), F32),
        compiler_params=pltpu.CompilerParams(
            dimension_semantics=("parallel", "parallel"), vmem_limit_bytes=VMEM_LIMIT),
        name="outproj",
    )(ya, yh, x, wa, wh, fg)


def _pack_weights(w_in, w_q_b, w_kv_b):
    o = 0
    w_qlat = w_in[:, o:o + MLA_Q_RANK]; o += MLA_Q_RANK
    w_kvlat = w_in[:, o:o + MLA_KV_RANK]; o += MLA_KV_RANK
    w_kr = w_in[:, o:o + MLA_ROPE]; o += MLA_ROPE
    w_rest = w_in[:, o:]
    z = lambda n: jnp.zeros((w_in.shape[0], n), w_in.dtype)
    x1, x2 = w_kr[:, :HALF], w_kr[:, HALF:]
    kr_main = jnp.concatenate([z(MLA_NOPE), x1, x2, z(LANES - MLA_NOPE - MLA_ROPE)], axis=1)
    kr_swap = jnp.concatenate([z(MLA_NOPE), x2, x1, z(LANES - MLA_NOPE - MLA_ROPE)], axis=1)
    win = jnp.concatenate([w_qlat, w_kvlat, kr_main, kr_swap, w_rest], axis=1).astype(BF16)
    assert win.shape[1] == _C_END

    wq = w_q_b.reshape(MLA_Q_RANK, MLA_HEADS, MLA_NOPE + MLA_ROPE)
    qn_, q1, q2 = wq[..., :MLA_NOPE], wq[..., MLA_NOPE:MLA_NOPE + HALF], wq[..., MLA_NOPE + HALF:]
    zq = lambda n: jnp.zeros((MLA_Q_RANK, MLA_HEADS, n), wq.dtype)
    pad = LANES - MLA_NOPE - MLA_ROPE
    wqm = jnp.concatenate([qn_, q1, q2, zq(pad)], axis=-1).reshape(MLA_Q_RANK, -1).astype(BF16)
    wqs = jnp.concatenate([zq(MLA_NOPE), q2, q1, zq(pad)], axis=-1).reshape(MLA_Q_RANK, -1).astype(BF16)

    wkv = w_kv_b.reshape(MLA_KV_RANK, MLA_HEADS, MLA_NOPE + MLA_V)
    wk = jnp.concatenate([wkv[..., :MLA_NOPE],
                          jnp.zeros((MLA_KV_RANK, MLA_HEADS, LANES - MLA_NOPE), wkv.dtype)],
                         axis=-1).reshape(MLA_KV_RANK, -1).astype(BF16)
    wvt = jnp.transpose(wkv[..., MLA_NOPE:], (1, 2, 0))
    wvt = jnp.concatenate([wvt, jnp.zeros((MLA_HEADS, VT_ROWS - MLA_V, MLA_KV_RANK), wvt.dtype)],
                          axis=1).astype(BF16)
    return win, wqm, wqs, wk, wvt


def _rope_table():
    inv = ROPE_THETA ** (-jnp.arange(HALF, dtype=F32) / HALF)
    zl = jnp.zeros((MLA_NOPE,), F32)
    zr = jnp.zeros((LANES - MLA_NOPE - MLA_ROPE,), F32)
    inv_lane = jnp.concatenate([zl, inv, inv, zr])
    sign = jnp.concatenate([zl, -jnp.ones((HALF,), F32), jnp.ones((HALF,), F32), zr])
    return jnp.stack([inv_lane, sign])


def kernel(x, positions, ln_g, w_in, q_a_norm_g, w_q_b, kv_a_norm_g, w_kv_b,
           hg_lower_bounds, hg_norm_g, w_out, final_norm_g):
    B, S, _ = x.shape
    assert ln_g.shape[0] == 1, "single-layer stack"
    win, wqm, wqs, wk, wvt = _pack_weights(w_in[0], w_q_b[0], w_kv_b[0])
    pos3 = positions.reshape(B, S, 1)
    q, k, vt, gm, hq, hf, hv, gh = _proj_call(
        x, pos3, ln_g[0:1], win, q_a_norm_g[0:1], wqm, wqs, kv_a_norm_g[0:1], wk, wvt,
        hg_lower_bounds, _rope_table())
    ya = _attn_call(q, k, vt, gm)
    yh = _hgrn_call(hf, hq, hv, gh, hg_norm_g[0:1])
    wo = w_out[0].astype(BF16)
    return _out_call(ya, yh, x, wo[:MLA_WIDTH], wo[MLA_WIDTH:], final_norm_g.reshape(1, D_MODEL))
```

```python
import functools
import math

import numpy as np
import jax
import jax.numpy as jnp
from jax import lax
from jax.experimental import pallas as pl
from jax.experimental.pallas import tpu as pltpu

F32 = jnp.float32
BF16 = jnp.bfloat16

D_MODEL = 1024
MLA_HEADS = 8
MLA_NOPE = 64
MLA_ROPE = 32
MLA_V = 64
MLA_Q_RANK = 256
MLA_KV_RANK = 128
MLA_WIDTH = MLA_HEADS * MLA_V
HG_HEADS = 4
HG_EXPAND = 128
HG_HEAD_V = 128
HG_WIDTH = HG_HEADS * HG_HEAD_V
HG_FDIM = HG_HEADS * HG_EXPAND
ROPE_THETA = 10000.0
EPS = 1e-6
HALF = MLA_ROPE // 2

LANES = 128
HEAD_SLAB = LANES
NEG_BIG = -1e30
LOG2E = math.log2(math.e)
VT_ROWS = 80

PROJ_TM = 256
ATT_TQ = 1024
ATT_TK = 256
ATT_NH = 8
ATT_AHEAD = 2
ATT_CW = 256
HG_CHUNK = 64
HG_SUB = 8
HG_LEVELS = 4
HG_TB = 256
OUT_TM = 512
VMEM_LIMIT = 48 * 1024 * 1024

_C_QLAT = 0
_C_KVLAT = _C_QLAT + MLA_Q_RANK
_C_KRM = _C_KVLAT + MLA_KV_RANK
_C_KRS = _C_KRM + LANES
_C_GM = _C_KRS + LANES
_C_HQ = _C_GM + MLA_WIDTH
_C_HF = _C_HQ + HG_FDIM
_C_HI = _C_HF + HG_FDIM
_C_GH = _C_HI + HG_WIDTH
_C_END = _C_GH + HG_WIDTH

_NT = (((1,), (1,)), ((), ()))
_TN = (((0,), (0,)), ((), ()))


def _rms(x, g):
    return x * lax.rsqrt(jnp.mean(x * x, axis=-1, keepdims=True) + EPS) * g


def _silu(x):
    return x * jax.nn.sigmoid(x)


def _proj_kernel(x_ref, pos_ref, lng_ref, win_ref, qg_ref, wqm_ref, wqs_ref,
                 kvg_ref, wk_ref, wvt_ref, lbraw_ref, rope_ref,
                 q_out, k_out, vt_out, gm_out, hq_out, hf_out, hv_out, gh_out):
    x = x_ref[0]
    tm = x.shape[0]
    h = _rms(x, lng_ref[...]).astype(BF16)

    def in_proj(c0, width):
        return jnp.dot(h, win_ref[:, c0:c0 + width], preferred_element_type=F32)

    lat = in_proj(0, _C_GM)

    ang_t = rope_ref[...] * pos_ref[0].astype(F32)
    cos_h = jnp.cos(ang_t).T
    sin_h = jnp.sin(ang_t).T
    pad = LANES - MLA_ROPE
    cos = jnp.concatenate([cos_h, cos_h, jnp.ones((tm, pad), F32)], axis=1)
    sin = jnp.concatenate([-sin_h, sin_h, jnp.zeros((tm, pad), F32)], axis=1)
    scale = LOG2E / math.sqrt(MLA_NOPE + MLA_ROPE)
    cos_q = cos * scale
    sin_q = sin * scale

    qn = _rms(lat[:, _C_QLAT:_C_QLAT + MLA_Q_RANK], qg_ref[...]).astype(BF16)
    qm = jnp.dot(qn, wqm_ref[...], preferred_element_type=F32)
    qs = jnp.dot(qn, wqs_ref[...], preferred_element_type=F32)
    kvn = _rms(lat[:, _C_KVLAT:_C_KVLAT + MLA_KV_RANK], kvg_ref[...]).astype(BF16)
    kk = jnp.dot(kvn, wk_ref[...], preferred_element_type=F32)
    kr = lat[:, _C_KRM:_C_KRM + LANES] * cos + lat[:, _C_KRS:_C_KRS + LANES] * sin
    ones_row = (lax.broadcasted_iota(jnp.int32, (VT_ROWS, 1), 0) == MLA_V).astype(F32)
    for hd in range(MLA_HEADS):
        sl = slice(hd * HEAD_SLAB, (hd + 1) * HEAD_SLAB)
        q_out[0, hd] = (qm[:, sl] * cos_q + qs[:, sl] * sin_q).astype(BF16)
        k_out[0, hd] = (kk[:, sl] + kr).astype(BF16)
        vt = lax.dot_general(wvt_ref[hd], kvn, _NT, preferred_element_type=F32)
        vt_out[0, hd] = (vt + ones_row).astype(BF16)
    gm_out[0] = _silu(in_proj(_C_GM, MLA_WIDTH))

    a0 = lbraw_ref[0:1, :]
    a1 = lbraw_ref[1:2, :]
    mx = jnp.maximum(a0, a1)
    e0 = jnp.exp(a0 - mx)
    e1 = jnp.exp(a1 - mx)
    lb = e0 / (e0 + e1)
    hq_out[0] = _silu(in_proj(_C_HQ, HG_FDIM))
    hf_out[0] = lb + (1.0 - lb) * jax.nn.sigmoid(in_proj(_C_HF, HG_FDIM))
    hv_out[0] = in_proj(_C_HI, HG_WIDTH).astype(BF16)
    gh_out[0] = _silu(in_proj(_C_GH, HG_WIDTH))


def _proj_call(x, pos3, lng, win, qg, wqm, wqs, kvg, wk, wvt, lbraw, rope_tab):
    B, S, _ = x.shape
    tm = PROJ_TM
    grid = (B, S // tm)
    tok = lambda w: pl.BlockSpec((1, tm, w), lambda b, i: (b, i, 0))
    full = lambda a: pl.BlockSpec(a.shape, lambda b, i: (0,) * a.ndim)
    head = pl.BlockSpec((1, MLA_HEADS, tm, HEAD_SLAB), lambda b, i: (b, 0, i, 0))
    out_shape = (
        jax.ShapeDtypeStruct((B, MLA_HEADS, S, HEAD_SLAB), BF16),
        jax.ShapeDtypeStruct((B, MLA_HEADS, S, HEAD_SLAB), BF16),
        jax.ShapeDtypeStruct((B, MLA_HEADS, VT_ROWS, S), BF16),
        jax.ShapeDtypeStruct((B, S, MLA_WIDTH), F32),
        jax.ShapeDtypeStruct((B, S, HG_FDIM), F32),
        jax.ShapeDtypeStruct((B, S, HG_FDIM), F32),
        jax.ShapeDtypeStruct((B, S, HG_WIDTH), BF16),
        jax.ShapeDtypeStruct((B, S, HG_WIDTH), F32),
    )
    return pl.pallas_call(
        _proj_kernel,
        grid=grid,
        in_specs=[tok(D_MODEL), pl.BlockSpec((1, 1, tm), lambda b, i: (b, 0, i)), full(lng), full(win), full(qg), full(wqm), full(wqs),
                  full(kvg), full(wk), full(wvt), full(lbraw), full(rope_tab)],
        out_specs=(head, head,
                   pl.BlockSpec((1, MLA_HEADS, VT_ROWS, tm), lambda b, i: (b, 0, 0, i)),
                   tok(MLA_WIDTH), tok(HG_FDIM), tok(HG_FDIM),
                   tok(HG_WIDTH), tok(HG_WIDTH)),
        out_shape=out_shape,
        compiler_params=pltpu.CompilerParams(
            dimension_semantics=("parallel", "parallel"), vmem_limit_bytes=VMEM_LIMIT),
        name="proj",
    )(x, pos3, lng, win, qg, wqm, wqs, kvg, wk, wvt, lbraw, rope_tab)


def _attn_kernel(q_ref, k_ref, vt_ref, g_ref, o_ref, s_scr, m_scr, acc_scr):
    qi = pl.program_id(2)
    tq, tk, nh, cw = ATT_TQ, ATT_TK, ATT_NH, ATT_CW
    m_scr[...] = jnp.full(m_scr.shape, NEG_BIG, F32)
    acc_scr[...] = jnp.zeros(acc_scr.shape, F32)

    def scores(h, r0, c0, masked):
        k = k_ref[0, h, pl.ds(r0, tk), :]
        q = q_ref[0, h, c0:, :]
        s = lax.dot_general(k, q, _NT, preferred_element_type=F32)
        for c in range(c0, tq, cw):
            blk = s[:, c - c0:c - c0 + cw]
            if masked and c - c0 < tk:
                key = lax.broadcasted_iota(jnp.int32, (tk, cw), 0)
                qry = lax.broadcasted_iota(jnp.int32, (tk, cw), 1) + (c - c0)
                blk = jnp.where(key <= qry, blk, NEG_BIG)
            s_scr[h, c // cw] = blk

    def softmax_pv(h, r0, c0):
        vt = vt_ref[0, h, :, pl.ds(r0, tk)]
        for c in range(c0, tq, cw):
            j = c // cw
            m_old = m_scr[h, :, c:c + cw]
            m_new = jnp.maximum(m_old, jnp.max(s_scr[h, j], axis=0, keepdims=True))
            p = jnp.exp2(s_scr[h, j] - m_new).astype(BF16)
            m_scr[h, :, c:c + cw] = m_new
            pv = jnp.dot(vt, p, preferred_element_type=F32)
            acc_scr[h, :, c:c + cw] = jnp.exp2(m_old - m_new) * acc_scr[h, :, c:c + cw] + pv

    def step(r0, c0, masked):
        for h in range(min(ATT_AHEAD, nh)):
            scores(h, r0, c0, masked)
        for h in range(nh):
            if h + ATT_AHEAD < nh:
                scores(h + ATT_AHEAD, r0, c0, masked)
            softmax_pv(h, r0, c0)

    def full_step(kb, carry):
        step(pl.multiple_of(kb * tk, tk), 0, False)
        return carry

    lax.fori_loop(0, qi * (tq // tk), full_step, 0)
    for j in range(tq // tk):
        step(pl.multiple_of(qi * tq + j * tk, tk), j * tk, True)

    outs = []
    for h in range(nh):
        acc = acc_scr[h]
        o = acc[:MLA_V] / acc[MLA_V:MLA_V + 1]
        outs.append(o.T)
    o_ref[0] = (jnp.concatenate(outs, axis=1) * g_ref[0]).astype(BF16)


def _attn_call(q, k, vt, gm):
    B, H, S, _ = q.shape
    tq, nh = ATT_TQ, ATT_NH
    grid = (B, H // nh, S // tq)
    return pl.pallas_call(
        _attn_kernel,
        grid=grid,
        in_specs=[
            pl.BlockSpec((1, nh, tq, HEAD_SLAB), lambda b, j, i: (b, j, i, 0)),
            pl.BlockSpec((1, nh, S, HEAD_SLAB), lambda b, j, i: (b, j, 0, 0),
                         pipeline_mode=pl.Buffered(1)),
            pl.BlockSpec((1, nh, VT_ROWS, S), lambda b, j, i: (b, j, 0, 0),
                         pipeline_mode=pl.Buffered(1)),
            pl.BlockSpec((1, tq, nh * MLA_V), lambda b, j, i: (b, i, j)),
        ],
        out_specs=pl.BlockSpec((1, tq, nh * MLA_V), lambda b, j, i: (b, i, j)),
        out_shape=jax.ShapeDtypeStruct((B, S, MLA_WIDTH), BF16),
        scratch_shapes=[
            pltpu.VMEM((nh, tq // ATT_CW, ATT_TK, ATT_CW), F32),
            pltpu.VMEM((nh, 1, tq), F32),
            pltpu.VMEM((nh, VT_ROWS, tq), F32),
        ],
        compiler_params=pltpu.CompilerParams(
            dimension_semantics=("parallel", "parallel", "arbitrary"),
            vmem_limit_bytes=VMEM_LIMIT),
        name="attn",
    )(q, k, vt, gm)


def _hgrn_constants():
    C, SUB, NL = HG_CHUNK, HG_SUB, HG_LEVELS
    t = np.arange(C)
    lm = np.zeros((2 * NL, C, C), np.float32)
    masks = np.zeros((NL, C, C), np.float32)
    masks[0] = (t[:, None] // SUB) == (t[None, :] // SUB)
    for l in range(NL):
        bs = SUB << l
        same = (t[:, None] // bs) == (t[None, :] // bs)
        lm[l] = same & (t[None, :] <= t[:, None])
        lm[NL + l] = same & (t[None, :] > t[:, None])
        if l < NL - 1:
            masks[l + 1] = ((t[:, None] // bs) == (t[None, :] // bs) + 1) & ((t[None, :] // bs) % 2 == 0)
    lmat = lm.reshape(2 * NL * C, C)
    lmat = np.concatenate([lmat, lmat], axis=1)
    j = np.arange(C)
    sel = np.zeros((SUB, LANES, C), np.float32)
    sel[:] = (j[None, None, :] % SUB) == np.arange(SUB)[:, None, None]
    return (jnp.asarray(lmat, BF16), jnp.asarray(sel.reshape(SUB * LANES, C), BF16),
            jnp.asarray(masks, F32))


def _hgrn_kernel(f_ref, q_ref, v_ref, g_ref, ng_ref, lmat_ref, sel_ref, mask_ref, o_ref,
                 st_scr, qs_scr, ks_scr, c8_scr, k3_scr, p2_scr, a_scr, ds_scr, oi_scr, dec_scr):
    C, SUB, NL, TB, NH = HG_CHUNK, HG_SUB, HG_LEVELS, HG_TB, HG_HEADS
    NC = TB // C

    @pl.when(pl.program_id(1) == 0)
    def _():
        st_scr[...] = jnp.zeros_like(st_scr)

    f = f_ref[0]
    q = q_ref[0]
    g = jnp.log2(f)
    k = 1.0 - f
    g1 = g.astype(BF16)
    g2 = (g - g1.astype(F32)).astype(BF16)
    lmat = lmat_ref[...]

    for c in range(NC):
        rows = slice(c * C, (c + 1) * C)
        e = jnp.dot(lmat, jnp.concatenate([g1[rows], g2[rows]], axis=0),
                    preferred_element_type=F32)
        for l in range(NL):
            cq = e[l * C:(l + 1) * C]
            ck = e[(NL + l) * C:(NL + l + 1) * C]
            qs_scr[l, rows, :] = (q[rows] * jnp.exp2(cq)).astype(BF16)
            ks_scr[l, rows, :] = (k[rows] * jnp.exp2(ck)).astype(BF16)
            if l == 0:
                for h in range(NH):
                    c8_scr[h, c * (C // SUB):(c + 1) * (C // SUB)] = (
                        cq[:, h * LANES:(h + 1) * LANES].reshape(C // SUB, SUB, LANES))
            if l == NL - 1:
                dec_scr[c:c + 1, :] = jnp.exp2(cq[C - 1:C, :])

    tt = lax.broadcasted_iota(jnp.int32, (TB // SUB, SUB, LANES), 1)
    for h in range(NH):
        hs = slice(h * LANES, (h + 1) * LANES)
        k3_scr[h] = k[:, hs].reshape(TB // SUB, SUB, LANES)
        q3 = q[:, hs].reshape(TB // SUB, SUB, LANES)
        c8 = c8_scr[h]
        for s in range(SUB):
            bcast = pl.ds(s, SUB, stride=0)
            dg = jnp.where(tt >= s, c8 - c8_scr[h, :, bcast, :], NEG_BIG)
            p = q3 * jnp.exp2(dg) * k3_scr[h, :, bcast, :]
            p2_scr[h, :, s * LANES:(s + 1) * LANES] = p.reshape(TB, LANES).astype(BF16)

    v = v_ref[0]
    sel_masks = [mask_ref[l] > 0.5 for l in range(NL)]
    for h in range(NH):
        hs = slice(h * LANES, (h + 1) * LANES)
        ad = jnp.dot(p2_scr[h], sel_ref[...], preferred_element_type=F32)
        for c in range(NC):
            rows = slice(c * C, (c + 1) * C)
            a = jnp.where(sel_masks[0], ad[rows], 0.0)
            for l in range(NL - 1):
                al = lax.dot_general(qs_scr[l, rows, hs], ks_scr[l, rows, hs], _NT,
                                     preferred_element_type=F32)
                a = jnp.where(sel_masks[l + 1], al, a)
            a_scr[h, rows, :] = a.astype(BF16)
    for h in range(NH):
        hs = slice(h * LANES, (h + 1) * LANES)
        for c in range(NC):
            rows = slice(c * C, (c + 1) * C)
            oi_scr[rows, hs] = jnp.dot(a_scr[h, rows, :], v[rows, hs], preferred_element_type=F32)
            ds_scr[h * NC + c] = lax.dot_general(v[rows, hs], ks_scr[NL - 1, rows, hs], _TN,
                                                 preferred_element_type=F32)

    for c in range(NC):
        rows = slice(c * C, (c + 1) * C)
        for h in range(NH):
            hs = slice(h * LANES, (h + 1) * LANES)
            st = st_scr[h]
            o = oi_scr[rows, hs] + lax.dot_general(qs_scr[NL - 1, rows, hs], st.astype(BF16), _NT,
                                                   preferred_element_type=F32)
            st_scr[h] = st * dec_scr[c:c + 1, hs] + ds_scr[h * NC + c]
            o = o * lax.rsqrt(jnp.mean(o * o, axis=-1, keepdims=True) + EPS) * ng_ref[:, hs]
            o_ref[0, rows, hs] = (o * g_ref[0, rows, hs]).astype(BF16)


def _hgrn_call(hf, hq, hv, gh, ng):
    B, S, W = hf.shape
    tb, C, SUB, NL, NH = HG_TB, HG_CHUNK, HG_SUB, HG_LEVELS, HG_HEADS
    assert SUB << (NL - 1) == C
    lmat, sel, masks = _hgrn_constants()
    grid = (B, S // tb)
    blk = pl.BlockSpec((1, tb, W), lambda b, t: (b, t, 0))
    full = lambda a: pl.BlockSpec(a.shape, lambda b, t: (0,) * a.ndim)
    return pl.pallas_call(
        _hgrn_kernel,
        grid=grid,
        in_specs=[blk, blk, blk, blk, full(ng), full(lmat), full(sel), full(masks)],
        out_specs=blk,
        out_shape=jax.ShapeDtypeStruct((B, S, HG_WIDTH), BF16),
        scratch_shapes=[
            pltpu.VMEM((NH, HG_HEAD_V, HG_EXPAND), F32),
            pltpu.VMEM((NL, tb, W), BF16),
            pltpu.VMEM((NL, tb, W), BF16),
            pltpu.VMEM((NH, tb // SUB, SUB, LANES), F32),
            pltpu.VMEM((NH, tb // SUB, SUB, LANES), F32),
            pltpu.VMEM((NH, tb, SUB * LANES), BF16),
            pltpu.VMEM((NH, tb, C), BF16),
            pltpu.VMEM((NH * (tb // C), HG_HEAD_V, HG_EXPAND), F32),
            pltpu.VMEM((tb, W), F32),
            pltpu.VMEM((tb // C, W), F32),
        ],
        compiler_params=pltpu.CompilerParams(
            dimension_semantics=("parallel", "arbitrary"),
            vmem_limit_bytes=VMEM_LIMIT),
        name="hgrn",
    )(hf, hq, hv, gh, ng, lmat, sel, masks)


def _out_kernel(ya_ref, yh_ref, x_ref, wa_ref, wh_ref, fg_ref, o_ref):
    y = jnp.dot(ya_ref[0], wa_ref[...], preferred_element_type=F32)
    y = y + jnp.dot(yh_ref[0], wh_ref[...], preferred_element_type=F32)
    o_ref[0] = _rms(x_ref[0] + y, fg_ref[...])


def _out_call(ya, yh, x, wa, wh, fg):
    B, S, _ = x.shape
    tm = OUT_TM
    grid = (B, S // tm)
    tok = lambda w: pl.BlockSpec((1, tm, w), lambda b, i: (b, i, 0))
    full = lambda a: pl.BlockSpec(a.shape, lambda b, i: (0,) * a.ndim)
    return pl.pallas_call(
        _out_kernel,
        grid=grid,
        in_specs=[tok(MLA_WIDTH), tok(HG_WIDTH), tok(D_MODEL), full(wa), full(wh), full(fg)],
        out_specs=tok(D_MODEL),
        out_shape=jax.ShapeDtypeStruct((B, S, D_MODEL), F32),
        compiler_params=pltpu.CompilerParams(
            dimension_semantics=("parallel", "parallel"), vmem_limit_bytes=VMEM_LIMIT),
        name="outproj",
    )(ya, yh, x, wa, wh, fg)


def _pack_weights(w_in, w_q_b, w_kv_b):
    o = 0
    w_qlat = w_in[:, o:o + MLA_Q_RANK]; o += MLA_Q_RANK
    w_kvlat = w_in[:, o:o + MLA_KV_RANK]; o += MLA_KV_RANK
    w_kr = w_in[:, o:o + MLA_ROPE]; o += MLA_ROPE
    w_rest = w_in[:, o:]
    z = lambda n: jnp.zeros((w_in.shape[0], n), w_in.dtype)
    x1, x2 = w_kr[:, :HALF], w_kr[:, HALF:]
    kr_main = jnp.concatenate([x1, x2, z(LANES - MLA_ROPE)], axis=1)
    kr_swap = jnp.concatenate([x2, x1, z(LANES - MLA_ROPE)], axis=1)
    win = jnp.concatenate([w_qlat, w_kvlat, kr_main, kr_swap, w_rest], axis=1).astype(BF16)
    assert win.shape[1] == _C_END

    wq = w_q_b.reshape(MLA_Q_RANK, MLA_HEADS, MLA_NOPE + MLA_ROPE)
    qn_, q1, q2 = wq[..., :MLA_NOPE], wq[..., MLA_NOPE:MLA_NOPE + HALF], wq[..., MLA_NOPE + HALF:]
    zq = lambda n: jnp.zeros((MLA_Q_RANK, MLA_HEADS, n), wq.dtype)
    pad = LANES - MLA_NOPE - MLA_ROPE
    wqm = jnp.concatenate([q1, q2, qn_, zq(pad)], axis=-1).reshape(MLA_Q_RANK, -1).astype(BF16)
    wqs = jnp.concatenate([q2, q1, zq(MLA_NOPE), zq(pad)], axis=-1).reshape(MLA_Q_RANK, -1).astype(BF16)

    wkv = w_kv_b.reshape(MLA_KV_RANK, MLA_HEADS, MLA_NOPE + MLA_V)
    zk = lambda n: jnp.zeros((MLA_KV_RANK, MLA_HEADS, n), wkv.dtype)
    wk = jnp.concatenate([zk(MLA_ROPE), wkv[..., :MLA_NOPE], zk(pad)],
                         axis=-1).reshape(MLA_KV_RANK, -1).astype(BF16)
    wvt = jnp.transpose(wkv[..., MLA_NOPE:], (1, 2, 0))
    wvt = jnp.concatenate([wvt, jnp.zeros((MLA_HEADS, VT_ROWS - MLA_V, MLA_KV_RANK), wvt.dtype)],
                          axis=1).astype(BF16)
    return win, wqm, wqs, wk, wvt


def _rope_table():
    inv = ROPE_THETA ** (-jnp.arange(HALF, dtype=F32) / HALF)
    return inv.reshape(HALF, 1)


def kernel(x, positions, ln_g, w_in, q_a_norm_g, w_q_b, kv_a_norm_g, w_kv_b,
           hg_lower_bounds, hg_norm_g, w_out, final_norm_g):
    B, S, _ = x.shape
    assert ln_g.shape[0] == 1, "single-layer stack"
    win, wqm, wqs, wk, wvt = _pack_weights(w_in[0], w_q_b[0], w_kv_b[0])
    pos3 = positions.reshape(B, 1, S)
    q, k, vt, gm, hq, hf, hv, gh = _proj_call(
        x, pos3, ln_g[0:1], win, q_a_norm_g[0:1], wqm, wqs, kv_a_norm_g[0:1], wk, wvt,
        hg_lower_bounds, _rope_table())
    ya = _attn_call(q, k, vt, gm)
    yh = _hgrn_call(hf, hq, hv, gh, hg_norm_g[0:1])
    wo = w_out[0].astype(BF16)
    return _out_call(ya, yh, x, wo[:MLA_WIDTH], wo[MLA_WIDTH:], final_norm_g.reshape(1, D_MODEL))
```

```python
import functools
import math

import numpy as np
import jax
import jax.numpy as jnp
from jax import lax
from jax.experimental import pallas as pl
from jax.experimental.pallas import tpu as pltpu

F32 = jnp.float32
BF16 = jnp.bfloat16

D_MODEL = 1024
MLA_HEADS = 8
MLA_NOPE = 64
MLA_ROPE = 32
MLA_V = 64
MLA_Q_RANK = 256
MLA_KV_RANK = 128
MLA_WIDTH = MLA_HEADS * MLA_V
HG_HEADS = 4
HG_EXPAND = 128
HG_HEAD_V = 128
HG_WIDTH = HG_HEADS * HG_HEAD_V
HG_FDIM = HG_HEADS * HG_EXPAND
ROPE_THETA = 10000.0
EPS = 1e-6
HALF = MLA_ROPE // 2

LANES = 128
HEAD_SLAB = LANES
NEG_BIG = -1e30
LOG2E = math.log2(math.e)
VT_ROWS = 80

PROJ_TM = 512
ATT_TQ = 1024
ATT_TK = 256
ATT_NH = 8
ATT_AHEAD = 8
ATT_UNROLL = 4
ATT_CW = 256
HG_CHUNK = 64
HG_SUB = 8
HG_LEVELS = 4
HG_TB = 512
OUT_TM = 512
VMEM_LIMIT = 48 * 1024 * 1024

_C_QLAT = 0
_C_KVLAT = _C_QLAT + MLA_Q_RANK
_C_KRM = _C_KVLAT + MLA_KV_RANK
_C_GM = _C_KRM + LANES
_C_HQ = _C_GM + MLA_WIDTH
_C_HF = _C_HQ + HG_FDIM
_C_HI = _C_HF + HG_FDIM
_C_GH = _C_HI + HG_WIDTH
_C_END = _C_GH + HG_WIDTH

_NT = (((1,), (1,)), ((), ()))
_TN = (((0,), (0,)), ((), ()))


def _rms(x, g):
    return x * lax.rsqrt(jnp.mean(x * x, axis=-1, keepdims=True) + EPS) * g


def _silu(x):
    return x * jax.nn.sigmoid(x)


def _swap_rope_halves(slab):
    lane = lax.broadcasted_iota(jnp.int32, slab.shape, 1)
    return jnp.where(lane < HALF, pltpu.roll(slab, LANES - HALF, 1), pltpu.roll(slab, HALF, 1))


def _proj_kernel(x_ref, pos_ref, lng_ref, win_ref, qg_ref, wqm_ref,
                 kvg_ref, wk_ref, wvt_ref, lbraw_ref, rope_ref,
                 q_out, k_out, vt_out, gm_out, hq_out, hf_out, hv_out, gh_out):
    x = x_ref[0]
    tm = x.shape[0]
    h = _rms(x, lng_ref[...]).astype(BF16)

    def in_proj(c0, width):
        return jnp.dot(h, win_ref[:, c0:c0 + width], preferred_element_type=F32)

    lat = in_proj(0, _C_GM)

    ang_t = rope_ref[...] * pos_ref[0].astype(F32)
    cos_h = jnp.cos(ang_t).T
    sin_h = jnp.sin(ang_t).T
    pad = LANES - MLA_ROPE
    cos = jnp.concatenate([cos_h, cos_h, jnp.ones((tm, pad), F32)], axis=1)
    sin = jnp.concatenate([-sin_h, sin_h, jnp.zeros((tm, pad), F32)], axis=1)
    scale = LOG2E / math.sqrt(MLA_NOPE + MLA_ROPE)
    cos_q = cos * scale
    sin_q = sin * scale

    qn = _rms(lat[:, _C_QLAT:_C_QLAT + MLA_Q_RANK], qg_ref[...]).astype(BF16)
    qm = jnp.dot(qn, wqm_ref[...], preferred_element_type=F32)
    kvn = _rms(lat[:, _C_KVLAT:_C_KVLAT + MLA_KV_RANK], kvg_ref[...]).astype(BF16)
    kk = jnp.dot(kvn, wk_ref[...], preferred_element_type=F32)
    kr = lat[:, _C_KRM:_C_KRM + LANES]
    kr = kr * cos + _swap_rope_halves(kr) * sin
    ones_row = (lax.broadcasted_iota(jnp.int32, (VT_ROWS, 1), 0) == MLA_V).astype(F32)
    for hd in range(MLA_HEADS):
        sl = slice(hd * HEAD_SLAB, (hd + 1) * HEAD_SLAB)
        q_out[0, hd] = (qm[:, sl] * cos_q + _swap_rope_halves(qm[:, sl]) * sin_q).astype(BF16)
        k_out[0, hd] = (kk[:, sl] + kr).astype(BF16)
        vt = lax.dot_general(wvt_ref[hd], kvn, _NT, preferred_element_type=F32)
        vt_out[0, hd] = (vt + ones_row).astype(BF16)
    gm_out[0] = _silu(in_proj(_C_GM, MLA_WIDTH))

    a0 = lbraw_ref[0:1, :]
    a1 = lbraw_ref[1:2, :]
    mx = jnp.maximum(a0, a1)
    e0 = jnp.exp(a0 - mx)
    e1 = jnp.exp(a1 - mx)
    lb = e0 / (e0 + e1)
    hq_out[0] = _silu(in_proj(_C_HQ, HG_FDIM))
    hf_out[0] = lb + (1.0 - lb) * jax.nn.sigmoid(in_proj(_C_HF, HG_FDIM))
    hv_out[0] = in_proj(_C_HI, HG_WIDTH).astype(BF16)
    gh_out[0] = _silu(in_proj(_C_GH, HG_WIDTH))


def _proj_call(x, pos3, lng, win, qg, wqm, kvg, wk, wvt, lbraw, rope_tab):
    B, S, _ = x.shape
    tm = PROJ_TM
    grid = (B, S // tm)
    tok = lambda w: pl.BlockSpec((1, tm, w), lambda b, i: (b, i, 0))
    full = lambda a: pl.BlockSpec(a.shape, lambda b, i: (0,) * a.ndim)
    head = pl.BlockSpec((1, MLA_HEADS, tm, HEAD_SLAB), lambda b, i: (b, 0, i, 0))
    out_shape = (
        jax.ShapeDtypeStruct((B, MLA_HEADS, S, HEAD_SLAB), BF16),
        jax.ShapeDtypeStruct((B, MLA_HEADS, S, HEAD_SLAB), BF16),
        jax.ShapeDtypeStruct((B, MLA_HEADS, VT_ROWS, S), BF16),
        jax.ShapeDtypeStruct((B, S, MLA_WIDTH), F32),
        jax.ShapeDtypeStruct((B, S, HG_FDIM), F32),
        jax.ShapeDtypeStruct((B, S, HG_FDIM), F32),
        jax.ShapeDtypeStruct((B, S, HG_WIDTH), BF16),
        jax.ShapeDtypeStruct((B, S, HG_WIDTH), F32),
    )
    return pl.pallas_call(
        _proj_kernel,
        grid=grid,
        in_specs=[tok(D_MODEL), pl.BlockSpec((1, 1, tm), lambda b, i: (b, 0, i)), full(lng), full(win), full(qg), full(wqm),
                  full(kvg), full(wk), full(wvt), full(lbraw), full(rope_tab)],
        out_specs=(head, head,
                   pl.BlockSpec((1, MLA_HEADS, VT_ROWS, tm), lambda b, i: (b, 0, 0, i)),
                   tok(MLA_WIDTH), tok(HG_FDIM), tok(HG_FDIM),
                   tok(HG_WIDTH), tok(HG_WIDTH)),
        out_shape=out_shape,
        compiler_params=pltpu.CompilerParams(
            dimension_semantics=("parallel", "parallel"), vmem_limit_bytes=VMEM_LIMIT),
        name="proj",
    )(x, pos3, lng, win, qg, wqm, kvg, wk, wvt, lbraw, rope_tab)


def _attn_kernel(q_ref, k_ref, vt_ref, g_ref, o_ref, s_scr, m_scr, acc_scr):
    qi = pl.program_id(2)
    tq, tk, nh, cw = ATT_TQ, ATT_TK, ATT_NH, ATT_CW
    m_scr[...] = jnp.full(m_scr.shape, NEG_BIG, F32)
    acc_scr[...] = jnp.zeros(acc_scr.shape, F32)

    def scores(h, j, r0, diag):
        k = k_ref[0, h, pl.ds(r0, tk), :]
        q = q_ref[0, h, j * cw:(j + 1) * cw, :]
        s = lax.dot_general(k, q, _NT, preferred_element_type=F32)
        if diag:
            key = lax.broadcasted_iota(jnp.int32, (tk, cw), 0)
            qry = lax.broadcasted_iota(jnp.int32, (tk, cw), 1)
            s = jnp.where(key <= qry, s, NEG_BIG)
        s_scr[h, j] = s

    def softmax_pv(h, j, r0):
        c = j * cw
        vt = vt_ref[0, h, :, pl.ds(r0, tk)]
        m_old = m_scr[h, :, c:c + cw]
        m_new = jnp.maximum(m_old, jnp.max(s_scr[h, j], axis=0, keepdims=True))
        p = jnp.exp2(s_scr[h, j] - m_new).astype(BF16)
        m_scr[h, :, c:c + cw] = m_new
        pv = jnp.dot(vt, p, preferred_element_type=F32)
        acc_scr[h, :, c:c + cw] = jnp.exp2(m_old - m_new) * acc_scr[h, :, c:c + cw] + pv

    def step(r0, c0, masked):
        units = [(h, j) for h in range(nh) for j in range(c0 // cw, tq // cw)]
        for u, (h, j) in enumerate(units[:ATT_AHEAD]):
            scores(h, j, r0, masked and j == c0 // cw)
        for u, (h, j) in enumerate(units):
            if u + ATT_AHEAD < len(units):
                h2, j2 = units[u + ATT_AHEAD]
                scores(h2, j2, r0, masked and j2 == c0 // cw)
            softmax_pv(h, j, r0)

    def full_steps(kb, carry):
        for i in range(ATT_UNROLL):
            step(pl.multiple_of((kb * ATT_UNROLL + i) * tk, tk), 0, False)
        return carry

    lax.fori_loop(0, qi * (tq // tk // ATT_UNROLL), full_steps, 0)
    for j in range(tq // tk):
        step(pl.multiple_of(qi * tq + j * tk, tk), j * tk, True)

    outs = []
    for h in range(nh):
        acc = acc_scr[h]
        o = acc[:MLA_V] / acc[MLA_V:MLA_V + 1]
        outs.append(o.T)
    o_ref[0] = (jnp.concatenate(outs, axis=1) * g_ref[0]).astype(BF16)


def _attn_call(q, k, vt, gm):
    B, H, S, _ = q.shape
    tq, nh = ATT_TQ, ATT_NH
    grid = (B, H // nh, S // tq)
    return pl.pallas_call(
        _attn_kernel,
        grid=grid,
        in_specs=[
            pl.BlockSpec((1, nh, tq, HEAD_SLAB), lambda b, j, i: (b, j, i, 0)),
            pl.BlockSpec((1, nh, S, HEAD_SLAB), lambda b, j, i: (b, j, 0, 0),
                         pipeline_mode=pl.Buffered(1)),
            pl.BlockSpec((1, nh, VT_ROWS, S), lambda b, j, i: (b, j, 0, 0),
                         pipeline_mode=pl.Buffered(1)),
            pl.BlockSpec((1, tq, nh * MLA_V), lambda b, j, i: (b, i, j)),
        ],
        out_specs=pl.BlockSpec((1, tq, nh * MLA_V), lambda b, j, i: (b, i, j)),
        out_shape=jax.ShapeDtypeStruct((B, S, MLA_WIDTH), BF16),
        scratch_shapes=[
            pltpu.VMEM((nh, tq // ATT_CW, ATT_TK, ATT_CW), F32),
            pltpu.VMEM((nh, 1, tq), F32),
            pltpu.VMEM((nh, VT_ROWS, tq), F32),
        ],
        compiler_params=pltpu.CompilerParams(
            dimension_semantics=("parallel", "parallel", "arbitrary"),
            vmem_limit_bytes=VMEM_LIMIT),
        name="attn",
    )(q, k, vt, gm)


def _hgrn_constants():
    C, SUB, NL = HG_CHUNK, HG_SUB, HG_LEVELS
    t = np.arange(C)
    lm = np.zeros((2 * NL, C, C), np.float32)
    masks = np.zeros((NL, C, C), np.float32)
    masks[0] = (t[:, None] // SUB) == (t[None, :] // SUB)
    for l in range(NL):
        bs = SUB << l
        same = (t[:, None] // bs) == (t[None, :] // bs)
        lm[l] = same & (t[None, :] <= t[:, None])
        lm[NL + l] = same & (t[None, :] > t[:, None])
        if l < NL - 1:
            masks[l + 1] = ((t[:, None] // bs) == (t[None, :] // bs) + 1) & ((t[None, :] // bs) % 2 == 0)
    lmat = lm.reshape(2 * NL * C, C)
    lmat = np.concatenate([lmat, lmat], axis=1)
    j = np.arange(C)
    sel = np.zeros((SUB, LANES, C), np.float32)
    sel[:] = (j[None, None, :] % SUB) == np.arange(SUB)[:, None, None]
    return (jnp.asarray(lmat, BF16), jnp.asarray(sel.reshape(SUB * LANES, C), BF16),
            jnp.asarray(masks, F32))


def _hgrn_kernel(f_ref, q_ref, v_ref, g_ref, ng_ref, lmat_ref, sel_ref, mask_ref, o_ref,
                 st_scr, qs_scr, ks_scr, c8_scr, k3_scr, p2_scr, a_scr, ds_scr, oi_scr, dec_scr):
    C, SUB, NL, TB, NH = HG_CHUNK, HG_SUB, HG_LEVELS, HG_TB, HG_HEADS
    NC = TB // C

    @pl.when(pl.program_id(1) == 0)
    def _():
        st_scr[...] = jnp.zeros_like(st_scr)

    f = f_ref[0]
    q = q_ref[0]
    g = jnp.log2(f)
    k = 1.0 - f
    g1 = g.astype(BF16)
    g2 = (g - g1.astype(F32)).astype(BF16)
    lmat = lmat_ref[...]

    for c in range(NC):
        rows = slice(c * C, (c + 1) * C)
        e = jnp.dot(lmat, jnp.concatenate([g1[rows], g2[rows]], axis=0),
                    preferred_element_type=F32)
        for l in range(NL):
            cq = e[l * C:(l + 1) * C]
            ck = e[(NL + l) * C:(NL + l + 1) * C]
            qs_scr[l, rows, :] = (q[rows] * jnp.exp2(cq)).astype(BF16)
            ks_scr[l, rows, :] = (k[rows] * jnp.exp2(ck)).astype(BF16)
            if l == 0:
                for h in range(NH):
                    c8_scr[h, c * (C // SUB):(c + 1) * (C // SUB)] = (
                        cq[:, h * LANES:(h + 1) * LANES].reshape(C // SUB, SUB, LANES))
            if l == NL - 1:
                dec_scr[c:c + 1, :] = jnp.exp2(cq[C - 1:C, :])

    tt = lax.broadcasted_iota(jnp.int32, (TB // SUB, SUB, LANES), 1)
    for h in range(NH):
        hs = slice(h * LANES, (h + 1) * LANES)
        k3_scr[h] = k[:, hs].reshape(TB // SUB, SUB, LANES)
        q3 = q[:, hs].reshape(TB // SUB, SUB, LANES)
        c8 = c8_scr[h]
        for s in range(SUB):
            bcast = pl.ds(s, SUB, stride=0)
            dg = jnp.where(tt >= s, c8 - c8_scr[h, :, bcast, :], NEG_BIG)
            p = q3 * jnp.exp2(dg) * k3_scr[h, :, bcast, :]
            p2_scr[h, :, s * LANES:(s + 1) * LANES] = p.reshape(TB, LANES).astype(BF16)

    v = v_ref[0]
    sel_masks = [mask_ref[l] > 0.5 for l in range(NL)]
    for h in range(NH):
        hs = slice(h * LANES, (h + 1) * LANES)
        ad = jnp.dot(p2_scr[h], sel_ref[...], preferred_element_type=F32)
        for c in range(NC):
            rows = slice(c * C, (c + 1) * C)
            a = jnp.where(sel_masks[0], ad[rows], 0.0)
            for l in range(NL - 1):
                al = lax.dot_general(qs_scr[l, rows, hs], ks_scr[l, rows, hs], _NT,
                                     preferred_element_type=F32)
                a = jnp.where(sel_masks[l + 1], al, a)
            a_scr[h, rows, :] = a.astype(BF16)
    for h in range(NH):
        hs = slice(h * LANES, (h + 1) * LANES)
        for c in range(NC):
            rows = slice(c * C, (c + 1) * C)
            oi_scr[rows, hs] = jnp.dot(a_scr[h, rows, :], v[rows, hs], preferred_element_type=F32)
            ds_scr[h * NC + c] = lax.dot_general(v[rows, hs], ks_scr[NL - 1, rows, hs], _TN,
                                                 preferred_element_type=F32)

    for c in range(NC):
        rows = slice(c * C, (c + 1) * C)
        for h in range(NH):
            hs = slice(h * LANES, (h + 1) * LANES)
            st = st_scr[h]
            o = oi_scr[rows, hs] + lax.dot_general(qs_scr[NL - 1, rows, hs], st.astype(BF16), _NT,
                                                   preferred_element_type=F32)
            st_scr[h] = st * dec_scr[c:c + 1, hs] + ds_scr[h * NC + c]
            o = o * lax.rsqrt(jnp.mean(o * o, axis=-1, keepdims=True) + EPS) * ng_ref[:, hs]
            o_ref[0, rows, hs] = (o * g_ref[0, rows, hs]).astype(BF16)


def _hgrn_call(hf, hq, hv, gh, ng):
    B, S, W = hf.shape
    tb, C, SUB, NL, NH = HG_TB, HG_CHUNK, HG_SUB, HG_LEVELS, HG_HEADS
    assert SUB << (NL - 1) == C
    lmat, sel, masks = _hgrn_constants()
    grid = (B, S // tb)
    blk = pl.BlockSpec((1, tb, W), lambda b, t: (b, t, 0))
    full = lambda a: pl.BlockSpec(a.shape, lambda b, t: (0,) * a.ndim)
    return pl.pallas_call(
        _hgrn_kernel,
        grid=grid,
        in_specs=[blk, blk, blk, blk, full(ng), full(lmat), full(sel), full(masks)],
        out_specs=blk,
        out_shape=jax.ShapeDtypeStruct((B, S, HG_WIDTH), BF16),
        scratch_shapes=[
            pltpu.VMEM((NH, HG_HEAD_V, HG_EXPAND), F32),
            pltpu.VMEM((NL, tb, W), BF16),
            pltpu.VMEM((NL, tb, W), BF16),
            pltpu.VMEM((NH, tb // SUB, SUB, LANES), F32),
            pltpu.VMEM((NH, tb // SUB, SUB, LANES), F32),
            pltpu.VMEM((NH, tb, SUB * LANES), BF16),
            pltpu.VMEM((NH, tb, C), BF16),
            pltpu.VMEM((NH * (tb // C), HG_HEAD_V, HG_EXPAND), F32),
            pltpu.VMEM((tb, W), F32),
            pltpu.VMEM((tb // C, W), F32),
        ],
        compiler_params=pltpu.CompilerParams(
            dimension_semantics=("parallel", "arbitrary"),
            vmem_limit_bytes=VMEM_LIMIT),
        name="hgrn",
    )(hf, hq, hv, gh, ng, lmat, sel, masks)


def _out_kernel(ya_ref, yh_ref, x_ref, wa_ref, wh_ref, fg_ref, o_ref):
    y = jnp.dot(ya_ref[0], wa_ref[...], preferred_element_type=F32)
    y = y + jnp.dot(yh_ref[0], wh_ref[...], preferred_element_type=F32)
    o_ref[0] = _rms(x_ref[0] + y, fg_ref[...])


def _out_call(ya, yh, x, wa, wh, fg):
    B, S, _ = x.shape
    tm = OUT_TM
    grid = (B, S // tm)
    tok = lambda w: pl.BlockSpec((1, tm, w), lambda b, i: (b, i, 0))
    full = lambda a: pl.BlockSpec(a.shape, lambda b, i: (0,) * a.ndim)
    return pl.pallas_call(
        _out_kernel,
        grid=grid,
        in_specs=[tok(MLA_WIDTH), tok(HG_WIDTH), tok(D_MODEL), full(wa), full(wh), full(fg)],
        out_specs=tok(D_MODEL),
        out_shape=jax.ShapeDtypeStruct((B, S, D_MODEL), F32),
        compiler_params=pltpu.CompilerParams(
            dimension_semantics=("parallel", "parallel"), vmem_limit_bytes=VMEM_LIMIT),
        name="outproj",
    )(ya, yh, x, wa, wh, fg)


def _pack_weights(w_in, w_q_b, w_kv_b):
    o = 0
    w_qlat = w_in[:, o:o + MLA_Q_RANK]; o += MLA_Q_RANK
    w_kvlat = w_in[:, o:o + MLA_KV_RANK]; o += MLA_KV_RANK
    w_kr = w_in[:, o:o + MLA_ROPE]; o += MLA_ROPE
    w_rest = w_in[:, o:]
    z_kr = jnp.zeros((w_in.shape[0], LANES - MLA_ROPE), w_in.dtype)
    win = jnp.concatenate([w_qlat, w_kvlat, w_kr, z_kr, w_rest], axis=1).astype(BF16)
    assert win.shape[1] == _C_END

    wq = w_q_b.reshape(MLA_Q_RANK, MLA_HEADS, MLA_NOPE + MLA_ROPE)
    qn_, q1, q2 = wq[..., :MLA_NOPE], wq[..., MLA_NOPE:MLA_NOPE + HALF], wq[..., MLA_NOPE + HALF:]
    zq = lambda n: jnp.zeros((MLA_Q_RANK, MLA_HEADS, n), wq.dtype)
    pad = LANES - MLA_NOPE - MLA_ROPE
    wqm = jnp.concatenate([q1, q2, qn_, zq(pad)], axis=-1).reshape(MLA_Q_RANK, -1).astype(BF16)

    wkv = w_kv_b.reshape(MLA_KV_RANK, MLA_HEADS, MLA_NOPE + MLA_V)
    zk = lambda n: jnp.zeros((MLA_KV_RANK, MLA_HEADS, n), wkv.dtype)
    wk = jnp.concatenate([zk(MLA_ROPE), wkv[..., :MLA_NOPE], zk(pad)],
                         axis=-1).reshape(MLA_KV_RANK, -1).astype(BF16)
    wvt = jnp.transpose(wkv[..., MLA_NOPE:], (1, 2, 0))
    wvt = jnp.concatenate([wvt, jnp.zeros((MLA_HEADS, VT_ROWS - MLA_V, MLA_KV_RANK), wvt.dtype)],
                          axis=1).astype(BF16)
    return win, wqm, wk, wvt


def _rope_table():
    inv = ROPE_THETA ** (-jnp.arange(HALF, dtype=F32) / HALF)
    return inv.reshape(HALF, 1)


def kernel(x, positions, ln_g, w_in, q_a_norm_g, w_q_b, kv_a_norm_g, w_kv_b,
           hg_lower_bounds, hg_norm_g, w_out, final_norm_g):
    B, S, _ = x.shape
    assert ln_g.shape[0] == 1, "single-layer stack"
    win, wqm, wk, wvt = _pack_weights(w_in[0], w_q_b[0], w_kv_b[0])
    pos3 = positions.reshape(B, 1, S)
    q, k, vt, gm, hq, hf, hv, gh = _proj_call(
        x, pos3, ln_g[0:1], win, q_a_norm_g[0:1], wqm, kv_a_norm_g[0:1], wk, wvt,
        hg_lower_bounds, _rope_table())
    ya = _attn_call(q, k, vt, gm)
    yh = _hgrn_call(hf, hq, hv, gh, hg_norm_g[0:1])
    wo = w_out[0].astype(BF16)
    return _out_call(ya, yh, x, wo[:MLA_WIDTH], wo[MLA_WIDTH:], final_norm_g.reshape(1, D_MODEL))
```

```python
import functools
import math

import numpy as np
import jax
import jax.numpy as jnp
from jax import lax
from jax.experimental import pallas as pl
from jax.experimental.pallas import tpu as pltpu

F32 = jnp.float32
BF16 = jnp.bfloat16

D_MODEL = 1024
MLA_HEADS = 8
MLA_NOPE = 64
MLA_ROPE = 32
MLA_V = 64
MLA_Q_RANK = 256
MLA_KV_RANK = 128
MLA_WIDTH = MLA_HEADS * MLA_V
HG_HEADS = 4
HG_EXPAND = 128
HG_HEAD_V = 128
HG_WIDTH = HG_HEADS * HG_HEAD_V
HG_FDIM = HG_HEADS * HG_EXPAND
ROPE_THETA = 10000.0
EPS = 1e-6
HALF = MLA_ROPE // 2

LANES = 128
HEAD_SLAB = LANES
NEG_BIG = -1e30
LOG2E = math.log2(math.e)
VT_ROWS = 80

PROJ_TM = 512
ATT_TQ = 1024
ATT_TK = 256
ATT_NH = 8
ATT_AHEAD = 8
ATT_UNROLL = 4
ATT_CW = 256
HG_CHUNK = 64
HG_SUB = 8
HG_LEVELS = 4
HG_TB = 512
OUT_TM = 1024
VMEM_LIMIT = 48 * 1024 * 1024

_C_QLAT = 0
_C_KVLAT = _C_QLAT + MLA_Q_RANK
_C_KRM = _C_KVLAT + MLA_KV_RANK
_C_GM = _C_KRM + LANES
_C_HQ = _C_GM + MLA_WIDTH
_C_HF = _C_HQ + HG_FDIM
_C_HI = _C_HF + HG_FDIM
_C_GH = _C_HI + HG_WIDTH
_C_END = _C_GH + HG_WIDTH

_NT = (((1,), (1,)), ((), ()))
_TN = (((0,), (0,)), ((), ()))


def _rms(x, g):
    return x * lax.rsqrt(jnp.mean(x * x, axis=-1, keepdims=True) + EPS) * g


def _silu(x):
    return x * jax.nn.sigmoid(x)


def _swap_rope_halves(slab):
    lane = lax.broadcasted_iota(jnp.int32, slab.shape, 1)
    return jnp.where(lane < HALF, pltpu.roll(slab, LANES - HALF, 1), pltpu.roll(slab, HALF, 1))


def _proj_kernel(x_ref, pos_ref, lng_ref, win_ref, qg_ref, wqt_ref,
                 kvg_ref, wk_ref, wvt_ref, lbraw_ref, rope_ref,
                 q_out, k_out, vt_out, gm_out, hq_out, hf_out, hv_out, gh_out):
    x = x_ref[0]
    tm = x.shape[0]
    h = _rms(x, lng_ref[...]).astype(BF16)

    def in_proj(c0, width):
        return jnp.dot(h, win_ref[:, c0:c0 + width], preferred_element_type=F32)

    lat = in_proj(0, _C_GM)

    ang_t = rope_ref[...] * pos_ref[0].astype(F32)
    cos_t = jnp.cos(ang_t)
    sin_t = jnp.sin(ang_t)
    cos_h = cos_t.T
    sin_h = sin_t.T
    pad = LANES - MLA_ROPE
    cos = jnp.concatenate([cos_h, cos_h, jnp.ones((tm, pad), F32)], axis=1)
    sin = jnp.concatenate([-sin_h, sin_h, jnp.zeros((tm, pad), F32)], axis=1)

    scale = LOG2E / math.sqrt(MLA_NOPE + MLA_ROPE)
    cos_q = jnp.concatenate([cos_t, cos_t, jnp.ones((pad, tm), F32)], axis=0) * scale
    sin_q = jnp.concatenate([-sin_t, sin_t, jnp.zeros((pad, tm), F32)], axis=0) * scale
    qn = _rms(lat[:, _C_QLAT:_C_QLAT + MLA_Q_RANK], qg_ref[...]).astype(BF16)
    for hd in range(MLA_HEADS):
        qt = lax.dot_general(wqt_ref[hd], qn, _NT, preferred_element_type=F32)
        qt_swap = jnp.concatenate([qt[HALF:MLA_ROPE], qt[:HALF], qt[MLA_ROPE:]], axis=0)
        q_out[0, hd] = (qt * cos_q + qt_swap * sin_q).astype(BF16)
    kvn = _rms(lat[:, _C_KVLAT:_C_KVLAT + MLA_KV_RANK], kvg_ref[...]).astype(BF16)
    kk = jnp.dot(kvn, wk_ref[...], preferred_element_type=F32)
    kr = lat[:, _C_KRM:_C_KRM + LANES]
    kr = kr * cos + _swap_rope_halves(kr) * sin
    ones_row = (lax.broadcasted_iota(jnp.int32, (VT_ROWS, 1), 0) == MLA_V).astype(F32)
    for hd in range(MLA_HEADS):
        sl = slice(hd * HEAD_SLAB, (hd + 1) * HEAD_SLAB)
        k_out[0, hd] = (kk[:, sl] + kr).astype(BF16)
        vt = lax.dot_general(wvt_ref[hd], kvn, _NT, preferred_element_type=F32)
        vt_out[0, hd] = (vt + ones_row).astype(BF16)
    gm_out[0] = _silu(in_proj(_C_GM, MLA_WIDTH))

    a0 = lbraw_ref[0:1, :]
    a1 = lbraw_ref[1:2, :]
    mx = jnp.maximum(a0, a1)
    e0 = jnp.exp(a0 - mx)
    e1 = jnp.exp(a1 - mx)
    lb = e0 / (e0 + e1)
    hq_out[0] = _silu(in_proj(_C_HQ, HG_FDIM))
    hf_out[0] = lb + (1.0 - lb) * jax.nn.sigmoid(in_proj(_C_HF, HG_FDIM))
    hv_out[0] = in_proj(_C_HI, HG_WIDTH).astype(BF16)
    gh_out[0] = _silu(in_proj(_C_GH, HG_WIDTH))


def _proj_call(x, pos3, lng, win, qg, wqm, kvg, wk, wvt, lbraw, rope_tab):
    B, S, _ = x.shape
    tm = PROJ_TM
    grid = (B, S // tm)
    tok = lambda w: pl.BlockSpec((1, tm, w), lambda b, i: (b, i, 0))
    full = lambda a: pl.BlockSpec(a.shape, lambda b, i: (0,) * a.ndim)
    head = pl.BlockSpec((1, MLA_HEADS, tm, HEAD_SLAB), lambda b, i: (b, 0, i, 0))
    out_shape = (
        jax.ShapeDtypeStruct((B, MLA_HEADS, HEAD_SLAB, S), BF16),
        jax.ShapeDtypeStruct((B, MLA_HEADS, S, HEAD_SLAB), BF16),
        jax.ShapeDtypeStruct((B, MLA_HEADS, VT_ROWS, S), BF16),
        jax.ShapeDtypeStruct((B, S, MLA_WIDTH), F32),
        jax.ShapeDtypeStruct((B, S, HG_FDIM), F32),
        jax.ShapeDtypeStruct((B, S, HG_FDIM), F32),
        jax.ShapeDtypeStruct((B, S, HG_WIDTH), BF16),
        jax.ShapeDtypeStruct((B, S, HG_WIDTH), F32),
    )
    return pl.pallas_call(
        _proj_kernel,
        grid=grid,
        in_specs=[tok(D_MODEL), pl.BlockSpec((1, 1, tm), lambda b, i: (b, 0, i)), full(lng), full(win),
                  full(qg), full(wqm), full(kvg), full(wk), full(wvt), full(lbraw), full(rope_tab)],
        out_specs=(pl.BlockSpec((1, MLA_HEADS, HEAD_SLAB, tm), lambda b, i: (b, 0, 0, i)), head,
                   pl.BlockSpec((1, MLA_HEADS, VT_ROWS, tm), lambda b, i: (b, 0, 0, i)),
                   tok(MLA_WIDTH), tok(HG_FDIM), tok(HG_FDIM),
                   tok(HG_WIDTH), tok(HG_WIDTH)),
        out_shape=out_shape,
        compiler_params=pltpu.CompilerParams(
            dimension_semantics=("parallel", "parallel"), vmem_limit_bytes=VMEM_LIMIT),
        name="proj",
    )(x, pos3, lng, win, qg, wqm, kvg, wk, wvt, lbraw, rope_tab)


def _attn_kernel(q_ref, k_ref, vt_ref, g_ref, o_ref, s_scr, m_scr, acc_scr):
    qi = pl.program_id(2)
    tq, tk, nh, cw = ATT_TQ, ATT_TK, ATT_NH, ATT_CW
    m_scr[...] = jnp.full(m_scr.shape, NEG_BIG, F32)
    acc_scr[...] = jnp.zeros(acc_scr.shape, F32)

    def scores(h, j, r0, diag):
        k = k_ref[0, h, pl.ds(r0, tk), :]
        q = q_ref[0, h, :, j * cw:(j + 1) * cw]
        s = jnp.dot(k, q, preferred_element_type=F32)
        if diag:
            key = lax.broadcasted_iota(jnp.int32, (tk, cw), 0)
            qry = lax.broadcasted_iota(jnp.int32, (tk, cw), 1)
            s = jnp.where(key <= qry, s, NEG_BIG)
        s_scr[h, j] = s

    def softmax_pv(h, j, r0):
        c = j * cw
        vt = vt_ref[0, h, :, pl.ds(r0, tk)]
        m_old = m_scr[h, :, c:c + cw]
        m_new = jnp.maximum(m_old, jnp.max(s_scr[h, j], axis=0, keepdims=True))
        p = jnp.exp2(s_scr[h, j] - m_new).astype(BF16)
        m_scr[h, :, c:c + cw] = m_new
        pv = jnp.dot(vt, p, preferred_element_type=F32)
        acc_scr[h, :, c:c + cw] = jnp.exp2(m_old - m_new) * acc_scr[h, :, c:c + cw] + pv

    def step(r0, c0, masked):
        units = [(h, j) for h in range(nh) for j in range(c0 // cw, tq // cw)]
        for u, (h, j) in enumerate(units[:ATT_AHEAD]):
            scores(h, j, r0, masked and j == c0 // cw)
        for u, (h, j) in enumerate(units):
            if u + ATT_AHEAD < len(units):
                h2, j2 = units[u + ATT_AHEAD]
                scores(h2, j2, r0, masked and j2 == c0 // cw)
            softmax_pv(h, j, r0)

    def full_steps(kb, carry):
        for i in range(ATT_UNROLL):
            step(pl.multiple_of((kb * ATT_UNROLL + i) * tk, tk), 0, False)
        return carry

    lax.fori_loop(0, qi * (tq // tk // ATT_UNROLL), full_steps, 0)
    for j in range(tq // tk):
        step(pl.multiple_of(qi * tq + j * tk, tk), j * tk, True)

    outs = []
    for h in range(nh):
        acc = acc_scr[h]
        o = acc[:MLA_V] / acc[MLA_V:MLA_V + 1]
        outs.append(o.T)
    o_ref[0] = (jnp.concatenate(outs, axis=1) * g_ref[0]).astype(BF16)


def _attn_call(qt, k, vt, gm):
    B, H, S, _ = k.shape
    tq, nh = ATT_TQ, ATT_NH
    assert ATT_CW == ATT_TK and (tq // ATT_TK) % ATT_UNROLL == 0
    grid = (B, H // nh, S // tq)
    return pl.pallas_call(
        _attn_kernel,
        grid=grid,
        in_specs=[
            pl.BlockSpec((1, nh, HEAD_SLAB, tq), lambda b, j, i: (b, j, 0, i)),
            pl.BlockSpec((1, nh, S, HEAD_SLAB), lambda b, j, i: (b, j, 0, 0),
                         pipeline_mode=pl.Buffered(1)),
            pl.BlockSpec((1, nh, VT_ROWS, S), lambda b, j, i: (b, j, 0, 0),
                         pipeline_mode=pl.Buffered(1)),
            pl.BlockSpec((1, tq, nh * MLA_V), lambda b, j, i: (b, i, j)),
        ],
        out_specs=pl.BlockSpec((1, tq, nh * MLA_V), lambda b, j, i: (b, i, j)),
        out_shape=jax.ShapeDtypeStruct((B, S, MLA_WIDTH), BF16),
        scratch_shapes=[
            pltpu.VMEM((nh, tq // ATT_CW, ATT_TK, ATT_CW), F32),
            pltpu.VMEM((nh, 1, tq), F32),
            pltpu.VMEM((nh, VT_ROWS, tq), F32),
        ],
        compiler_params=pltpu.CompilerParams(
            dimension_semantics=("parallel", "parallel", "arbitrary"),
            vmem_limit_bytes=VMEM_LIMIT),
        name="attn",
    )(qt, k, vt, gm)


def _hgrn_constants():
    C, SUB, NL = HG_CHUNK, HG_SUB, HG_LEVELS
    t = np.arange(C)
    lm = np.zeros((2 * NL, C, C), np.float32)
    masks = np.zeros((NL, C, C), np.float32)
    masks[0] = (t[:, None] // SUB) == (t[None, :] // SUB)
    for l in range(NL):
        bs = SUB << l
        same = (t[:, None] // bs) == (t[None, :] // bs)
        lm[l] = same & (t[None, :] <= t[:, None])
        lm[NL + l] = same & (t[None, :] > t[:, None])
        if l < NL - 1:
            masks[l + 1] = ((t[:, None] // bs) == (t[None, :] // bs) + 1) & ((t[None, :] // bs) % 2 == 0)
    lmat = lm.reshape(2 * NL * C, C)
    lmat = np.concatenate([lmat, lmat], axis=1)
    j = np.arange(C)
    sel = np.zeros((SUB, LANES, C), np.float32)
    sel[:] = (j[None, None, :] % SUB) == np.arange(SUB)[:, None, None]
    return (jnp.asarray(lmat, BF16), jnp.asarray(sel.reshape(SUB * LANES, C), BF16),
            jnp.asarray(masks, F32))


def _hgrn_kernel(f_ref, q_ref, v_ref, g_ref, ng_ref, lmat_ref, sel_ref, mask_ref, o_ref,
                 st_scr, qs_scr, ks_scr, c8_scr, k3_scr, p2_scr, a_scr, ds_scr, oi_scr, dec_scr):
    C, SUB, NL, TB, NH = HG_CHUNK, HG_SUB, HG_LEVELS, HG_TB, HG_HEADS
    NC = TB // C

    @pl.when(pl.program_id(1) == 0)
    def _():
        st_scr[...] = jnp.zeros_like(st_scr)

    f = f_ref[0]
    q = q_ref[0]
    g = jnp.log2(f)
    k = 1.0 - f
    g1 = g.astype(BF16)
    g2 = (g - g1.astype(F32)).astype(BF16)
    lmat = lmat_ref[...]

    for c in range(NC):
        rows = slice(c * C, (c + 1) * C)
        e = jnp.dot(lmat, jnp.concatenate([g1[rows], g2[rows]], axis=0),
                    preferred_element_type=F32)
        for l in range(NL):
            cq = e[l * C:(l + 1) * C]
            ck = e[(NL + l) * C:(NL + l + 1) * C]
            qs_scr[l, rows, :] = (q[rows] * jnp.exp2(cq)).astype(BF16)
            ks_scr[l, rows, :] = (k[rows] * jnp.exp2(ck)).astype(BF16)
            if l == 0:
                for h in range(NH):
                    c8_scr[h, c * (C // SUB):(c + 1) * (C // SUB)] = (
                        cq[:, h * LANES:(h + 1) * LANES].reshape(C // SUB, SUB, LANES))
            if l == NL - 1:
                dec_scr[c:c + 1, :] = jnp.exp2(cq[C - 1:C, :])

    tt = lax.broadcasted_iota(jnp.int32, (1, SUB, LANES), 1)
    for h in range(NH):
        hs = slice(h * LANES, (h + 1) * LANES)
        k3_scr[h] = k[:, hs].reshape(TB // SUB, SUB, LANES)
        q3 = q[:, hs].reshape(TB // SUB, SUB, LANES)
        c8 = c8_scr[h]
        for s in range(SUB):
            bcast = pl.ds(s, SUB, stride=0)
            dg = jnp.where(tt >= s, c8 - c8_scr[h, :, bcast, :], NEG_BIG)
            p = q3 * jnp.exp2(dg) * k3_scr[h, :, bcast, :]
            p2_scr[h, :, s * LANES:(s + 1) * LANES] = p.reshape(TB, LANES).astype(BF16)

    v = v_ref[0]
    sel_masks = [mask_ref[l] > 0.5 for l in range(NL)]
    for h in range(NH):
        hs = slice(h * LANES, (h + 1) * LANES)
        ad = jnp.dot(p2_scr[h], sel_ref[...], preferred_element_type=F32)
        for c in range(NC):
            rows = slice(c * C, (c + 1) * C)
            a = jnp.where(sel_masks[0], ad[rows], 0.0)
            for l in range(NL - 1):
                al = lax.dot_general(qs_scr[l, rows, hs], ks_scr[l, rows, hs], _NT,
                                     preferred_element_type=F32)
                a = jnp.where(sel_masks[l + 1], al, a)
            a_scr[h, rows, :] = a.astype(BF16)
    for h in range(NH):
        hs = slice(h * LANES, (h + 1) * LANES)
        for c in range(NC):
            rows = slice(c * C, (c + 1) * C)
            oi_scr[rows, hs] = jnp.dot(a_scr[h, rows, :], v[rows, hs], preferred_element_type=F32)
            ds_scr[h * NC + c] = lax.dot_general(v[rows, hs], ks_scr[NL - 1, rows, hs], _TN,
                                                 preferred_element_type=F32)

    for c in range(NC):
        rows = slice(c * C, (c + 1) * C)
        for h in range(NH):
            hs = slice(h * LANES, (h + 1) * LANES)
            st = st_scr[h]
            o = oi_scr[rows, hs] + lax.dot_general(qs_scr[NL - 1, rows, hs], st.astype(BF16), _NT,
                                                   preferred_element_type=F32)
            st_scr[h] = st * dec_scr[c:c + 1, hs] + ds_scr[h * NC + c]
            o = o * lax.rsqrt(jnp.mean(o * o, axis=-1, keepdims=True) + EPS) * ng_ref[:, hs]
            o_ref[0, rows, hs] = (o * g_ref[0, rows, hs]).astype(BF16)


def _hgrn_call(hf, hq, hv, gh, ng):
    B, S, W = hf.shape
    tb, C, SUB, NL, NH = HG_TB, HG_CHUNK, HG_SUB, HG_LEVELS, HG_HEADS
    assert SUB << (NL - 1) == C
    lmat, sel, masks = _hgrn_constants()
    grid = (B, S // tb)
    blk = pl.BlockSpec((1, tb, W), lambda b, t: (b, t, 0))
    full = lambda a: pl.BlockSpec(a.shape, lambda b, t: (0,) * a.ndim)
    return pl.pallas_call(
        _hgrn_kernel,
        grid=grid,
        in_specs=[blk, blk, blk, blk, full(ng), full(lmat), full(sel), full(masks)],
        out_specs=blk,
        out_shape=jax.ShapeDtypeStruct((B, S, HG_WIDTH), BF16),
        scratch_shapes=[
            pltpu.VMEM((NH, HG_HEAD_V, HG_EXPAND), F32),
            pltpu.VMEM((NL, tb, W), BF16),
            pltpu.VMEM((NL, tb, W), BF16),
            pltpu.VMEM((NH, tb // SUB, SUB, LANES), F32),
            pltpu.VMEM((NH, tb // SUB, SUB, LANES), F32),
            pltpu.VMEM((NH, tb, SUB * LANES), BF16),
            pltpu.VMEM((NH, tb, C), BF16),
            pltpu.VMEM((NH * (tb // C), HG_HEAD_V, HG_EXPAND), F32),
            pltpu.VMEM((tb, W), F32),
            pltpu.VMEM((tb // C, W), F32),
        ],
        compiler_params=pltpu.CompilerParams(
            dimension_semantics=("parallel", "arbitrary"),
            vmem_limit_bytes=VMEM_LIMIT),
        name="hgrn",
    )(hf, hq, hv, gh, ng, lmat, sel, masks)


def _out_kernel(ya_ref, yh_ref, x_ref, wa_ref, wh_ref, fg_ref, o_ref):
    y = jnp.dot(ya_ref[0], wa_ref[...], preferred_element_type=F32)
    y = y + jnp.dot(yh_ref[0], wh_ref[...], preferred_element_type=F32)
    o_ref[0] = _rms(x_ref[0] + y, fg_ref[...])


def _out_call(ya, yh, x, wa, wh, fg):
    B, S, _ = x.shape
    tm = OUT_TM
    grid = (B, S // tm)
    tok = lambda w: pl.BlockSpec((1, tm, w), lambda b, i: (b, i, 0))
    full = lambda a: pl.BlockSpec(a.shape, lambda b, i: (0,) * a.ndim)
    return pl.pallas_call(
        _out_kernel,
        grid=grid,
        in_specs=[tok(MLA_WIDTH), tok(HG_WIDTH), tok(D_MODEL), full(wa), full(wh), full(fg)],
        out_specs=tok(D_MODEL),
        out_shape=jax.ShapeDtypeStruct((B, S, D_MODEL), F32),
        compiler_params=pltpu.CompilerParams(
            dimension_semantics=("parallel", "parallel"), vmem_limit_bytes=VMEM_LIMIT),
        name="outproj",
    )(ya, yh, x, wa, wh, fg)


def _pack_weights(w_in, w_q_b, w_kv_b):
    o = 0
    w_qlat = w_in[:, o:o + MLA_Q_RANK]; o += MLA_Q_RANK
    w_kvlat = w_in[:, o:o + MLA_KV_RANK]; o += MLA_KV_RANK
    w_kr = w_in[:, o:o + MLA_ROPE]; o += MLA_ROPE
    w_rest = w_in[:, o:]
    z_kr = jnp.zeros((w_in.shape[0], LANES - MLA_ROPE), w_in.dtype)
    win = jnp.concatenate([w_qlat, w_kvlat, w_kr, z_kr, w_rest], axis=1).astype(BF16)
    assert win.shape[1] == _C_END

    wq = w_q_b.reshape(MLA_Q_RANK, MLA_HEADS, MLA_NOPE + MLA_ROPE)
    qn_, q1, q2 = wq[..., :MLA_NOPE], wq[..., MLA_NOPE:MLA_NOPE + HALF], wq[..., MLA_NOPE + HALF:]
    zq = lambda n: jnp.zeros((MLA_Q_RANK, MLA_HEADS, n), wq.dtype)
    pad = LANES - MLA_NOPE - MLA_ROPE
    wqm = jnp.transpose(jnp.concatenate([q1, q2, qn_, zq(pad)], axis=-1), (1, 2, 0)).astype(BF16)

    wkv = w_kv_b.reshape(MLA_KV_RANK, MLA_HEADS, MLA_NOPE + MLA_V)
    zk = lambda n: jnp.zeros((MLA_KV_RANK, MLA_HEADS, n), wkv.dtype)
    wk = jnp.concatenate([zk(MLA_ROPE), wkv[..., :MLA_NOPE], zk(pad)],
                         axis=-1).reshape(MLA_KV_RANK, -1).astype(BF16)
    wvt = jnp.transpose(wkv[..., MLA_NOPE:], (1, 2, 0))
    wvt = jnp.concatenate([wvt, jnp.zeros((MLA_HEADS, VT_ROWS - MLA_V, MLA_KV_RANK), wvt.dtype)],
                          axis=1).astype(BF16)
    return win, wqm, wk, wvt


def _rope_table():
    inv = ROPE_THETA ** (-jnp.arange(HALF, dtype=F32) / HALF)
    return inv.reshape(HALF, 1)


def kernel(x, positions, ln_g, w_in, q_a_norm_g, w_q_b, kv_a_norm_g, w_kv_b,
           hg_lower_bounds, hg_norm_g, w_out, final_norm_g):
    B, S, _ = x.shape
    assert ln_g.shape[0] == 1, "single-layer stack"
    win, wqm, wk, wvt = _pack_weights(w_in[0], w_q_b[0], w_kv_b[0])
    pos3 = positions.reshape(B, 1, S)
    q, k, vt, gm, hq, hf, hv, gh = _proj_call(
        x, pos3, ln_g[0:1], win, q_a_norm_g[0:1], wqm, kv_a_norm_g[0:1], wk, wvt,
        hg_lower_bounds, _rope_table())
    ya = _attn_call(q, k, vt, gm)
    yh = _hgrn_call(hf, hq, hv, gh, hg_norm_g[0:1])
    wo = w_out[0].astype(BF16)
    return _out_call(ya, yh, x, wo[:MLA_WIDTH], wo[MLA_WIDTH:], final_norm_g.reshape(1, D_MODEL))
```

```python
import functools
import math

import numpy as np
import jax
import jax.numpy as jnp
from jax import lax
from jax.experimental import pallas as pl
from jax.experimental.pallas import tpu as pltpu

F32 = jnp.float32
BF16 = jnp.bfloat16

D_MODEL = 1024
MLA_HEADS = 8
MLA_NOPE = 64
MLA_ROPE = 32
MLA_V = 64
MLA_Q_RANK = 256
MLA_KV_RANK = 128
MLA_WIDTH = MLA_HEADS * MLA_V
HG_HEADS = 4
HG_EXPAND = 128
HG_HEAD_V = 128
HG_WIDTH = HG_HEADS * HG_HEAD_V
HG_FDIM = HG_HEADS * HG_EXPAND
ROPE_THETA = 10000.0
EPS = 1e-6
HALF = MLA_ROPE // 2

LANES = 128
HEAD_SLAB = LANES
NEG_BIG = -1e30
LOG2E = math.log2(math.e)
VT_ROWS = 80

PROJ_TM = 1024
ATT_TQ = 1024
ATT_TK = 256
ATT_NH = 8
ATT_AHEAD = 8
ATT_UNROLL = 4
ATT_CW = 256
HG_CHUNK = 64
HG_SUB = 8
HG_LEVELS = 4
HG_TB = 512
OUT_TM = 2048
VMEM_LIMIT = 56 * 1024 * 1024

_C_QLAT = 0
_C_KVLAT = _C_QLAT + MLA_Q_RANK
_C_KRM = _C_KVLAT + MLA_KV_RANK
_C_GM = _C_KRM + LANES
_C_HQ = _C_GM + MLA_WIDTH
_C_HF = _C_HQ + HG_FDIM
_C_HI = _C_HF + HG_FDIM
_C_GH = _C_HI + HG_WIDTH
_C_END = _C_GH + HG_WIDTH

_NT = (((1,), (1,)), ((), ()))
_TN = (((0,), (0,)), ((), ()))


def _rms(x, g):
    return x * lax.rsqrt(jnp.mean(x * x, axis=-1, keepdims=True) + EPS) * g


def _silu(x):
    return x * jax.nn.sigmoid(x)


def _swap_rope_halves(slab):
    lane = lax.broadcasted_iota(jnp.int32, slab.shape, 1)
    return jnp.where(lane < HALF, pltpu.roll(slab, LANES - HALF, 1), pltpu.roll(slab, HALF, 1))


def _proj_kernel(x_ref, pos_ref, lng_ref, win_ref, qg_ref, wqt_ref,
                 kvg_ref, wk_ref, wvt_ref, lbraw_ref, rope_ref,
                 q_out, k_out, vt_out, gm_out, hq_out, hf_out, hv_out, gh_out):
    x = x_ref[0]
    tm = x.shape[0]
    h = _rms(x, lng_ref[...]).astype(BF16)

    def in_proj(c0, width):
        return jnp.dot(h, win_ref[:, c0:c0 + width], preferred_element_type=F32)

    lat = in_proj(0, _C_GM)

    ang_t = rope_ref[...] * pos_ref[0].astype(F32)
    cos_t = jnp.cos(ang_t)
    sin_t = jnp.sin(ang_t)
    cos_h = cos_t.T
    sin_h = sin_t.T
    pad = LANES - MLA_ROPE
    cos = jnp.concatenate([cos_h, cos_h, jnp.ones((tm, pad), F32)], axis=1)
    sin = jnp.concatenate([-sin_h, sin_h, jnp.zeros((tm, pad), F32)], axis=1)

    scale = LOG2E / math.sqrt(MLA_NOPE + MLA_ROPE)
    cos_q = jnp.concatenate([cos_t, cos_t, jnp.ones((pad, tm), F32)], axis=0) * scale
    sin_q = jnp.concatenate([-sin_t, sin_t, jnp.zeros((pad, tm), F32)], axis=0) * scale
    qn = _rms(lat[:, _C_QLAT:_C_QLAT + MLA_Q_RANK], qg_ref[...]).astype(BF16)
    for hd in range(MLA_HEADS):
        qt = lax.dot_general(wqt_ref[hd], qn, _NT, preferred_element_type=F32)
        qt_swap = jnp.concatenate([qt[HALF:MLA_ROPE], qt[:HALF], qt[MLA_ROPE:]], axis=0)
        q_out[0, hd] = (qt * cos_q + qt_swap * sin_q).astype(BF16)
    kvn = _rms(lat[:, _C_KVLAT:_C_KVLAT + MLA_KV_RANK], kvg_ref[...]).astype(BF16)
    kk = jnp.dot(kvn, wk_ref[...], preferred_element_type=F32)
    kr = lat[:, _C_KRM:_C_KRM + LANES]
    kr = kr * cos + _swap_rope_halves(kr) * sin
    ones_row = (lax.broadcasted_iota(jnp.int32, (VT_ROWS, 1), 0) == MLA_V).astype(F32)
    for hd in range(MLA_HEADS):
        sl = slice(hd * HEAD_SLAB, (hd + 1) * HEAD_SLAB)
        k_out[0, hd] = (kk[:, sl] + kr).astype(BF16)
        vt = lax.dot_general(wvt_ref[hd], kvn, _NT, preferred_element_type=F32)
        vt_out[0, hd] = (vt + ones_row).astype(BF16)
    gm_out[0] = _silu(in_proj(_C_GM, MLA_WIDTH))

    a0 = lbraw_ref[0:1, :]
    a1 = lbraw_ref[1:2, :]
    mx = jnp.maximum(a0, a1)
    e0 = jnp.exp(a0 - mx)
    e1 = jnp.exp(a1 - mx)
    lb = e0 / (e0 + e1)
    hq_out[0] = _silu(in_proj(_C_HQ, HG_FDIM))
    hf_out[0] = lb + (1.0 - lb) * jax.nn.sigmoid(in_proj(_C_HF, HG_FDIM))
    hv_out[0] = in_proj(_C_HI, HG_WIDTH).astype(BF16)
    gh_out[0] = _silu(in_proj(_C_GH, HG_WIDTH))


def _proj_call(x, pos3, lng, win, qg, wqm, kvg, wk, wvt, lbraw, rope_tab):
    B, S, _ = x.shape
    tm = PROJ_TM
    grid = (B, S // tm)
    tok = lambda w: pl.BlockSpec((1, tm, w), lambda b, i: (b, i, 0))
    full = lambda a: pl.BlockSpec(a.shape, lambda b, i: (0,) * a.ndim, pipeline_mode=pl.Buffered(1))
    head = pl.BlockSpec((1, MLA_HEADS, tm, HEAD_SLAB), lambda b, i: (b, 0, i, 0))
    out_shape = (
        jax.ShapeDtypeStruct((B, MLA_HEADS, HEAD_SLAB, S), BF16),
        jax.ShapeDtypeStruct((B, MLA_HEADS, S, HEAD_SLAB), BF16),
        jax.ShapeDtypeStruct((B, MLA_HEADS, VT_ROWS, S), BF16),
        jax.ShapeDtypeStruct((B, S, MLA_WIDTH), F32),
        jax.ShapeDtypeStruct((B, S, HG_FDIM), F32),
        jax.ShapeDtypeStruct((B, S, HG_FDIM), F32),
        jax.ShapeDtypeStruct((B, S, HG_WIDTH), BF16),
        jax.ShapeDtypeStruct((B, S, HG_WIDTH), F32),
    )
    return pl.pallas_call(
        _proj_kernel,
        grid=grid,
        in_specs=[tok(D_MODEL), pl.BlockSpec((1, 1, tm), lambda b, i: (b, 0, i)), full(lng), full(win),
                  full(qg), full(wqm), full(kvg), full(wk), full(wvt), full(lbraw), full(rope_tab)],
        out_specs=(pl.BlockSpec((1, MLA_HEADS, HEAD_SLAB, tm), lambda b, i: (b, 0, 0, i)), head,
                   pl.BlockSpec((1, MLA_HEADS, VT_ROWS, tm), lambda b, i: (b, 0, 0, i)),
                   tok(MLA_WIDTH), tok(HG_FDIM), tok(HG_FDIM),
                   tok(HG_WIDTH), tok(HG_WIDTH)),
        out_shape=out_shape,
        compiler_params=pltpu.CompilerParams(
            dimension_semantics=("parallel", "parallel"), vmem_limit_bytes=VMEM_LIMIT),
        name="proj",
    )(x, pos3, lng, win, qg, wqm, kvg, wk, wvt, lbraw, rope_tab)


def _attn_kernel(q_ref, k_ref, vt_ref, g_ref, o_ref, s_scr, m_scr, acc_scr):
    qi = pl.program_id(2)
    tq, tk, nh, cw = ATT_TQ, ATT_TK, ATT_NH, ATT_CW
    m_scr[...] = jnp.full(m_scr.shape, NEG_BIG, F32)
    acc_scr[...] = jnp.zeros(acc_scr.shape, F32)

    def scores(h, j, r0, diag):
        k = k_ref[0, h, pl.ds(r0, tk), :]
        q = q_ref[0, h, :, j * cw:(j + 1) * cw]
        s = jnp.dot(k, q, preferred_element_type=F32)
        if diag:
            key = lax.broadcasted_iota(jnp.int32, (tk, cw), 0)
            qry = lax.broadcasted_iota(jnp.int32, (tk, cw), 1)
            s = jnp.where(key <= qry, s, NEG_BIG)
        s_scr[h, j] = s

    def softmax_pv(h, j, r0):
        c = j * cw
        vt = vt_ref[0, h, :, pl.ds(r0, tk)]
        m_old = m_scr[h, :, c:c + cw]
        m_new = jnp.maximum(m_old, jnp.max(s_scr[h, j], axis=0, keepdims=True))
        p = jnp.exp2(s_scr[h, j] - m_new).astype(BF16)
        m_scr[h, :, c:c + cw] = m_new
        pv = jnp.dot(vt, p, preferred_element_type=F32)
        acc_scr[h, :, c:c + cw] = jnp.exp2(m_old - m_new) * acc_scr[h, :, c:c + cw] + pv

    def step(r0, c0, masked):
        units = [(h, j) for h in range(nh) for j in range(c0 // cw, tq // cw)]
        for u, (h, j) in enumerate(units[:ATT_AHEAD]):
            scores(h, j, r0, masked and j == c0 // cw)
        for u, (h, j) in enumerate(units):
            if u + ATT_AHEAD < len(units):
                h2, j2 = units[u + ATT_AHEAD]
                scores(h2, j2, r0, masked and j2 == c0 // cw)
            softmax_pv(h, j, r0)

    def full_steps(kb, carry):
        for i in range(ATT_UNROLL):
            step(pl.multiple_of((kb * ATT_UNROLL + i) * tk, tk), 0, False)
        return carry

    lax.fori_loop(0, qi * (tq // tk // ATT_UNROLL), full_steps, 0)
    for j in range(tq // tk):
        step(pl.multiple_of(qi * tq + j * tk, tk), j * tk, True)

    outs = []
    for h in range(nh):
        acc = acc_scr[h]
        o = acc[:MLA_V] / acc[MLA_V:MLA_V + 1]
        outs.append(o.T)
    o_ref[0] = (jnp.concatenate(outs, axis=1) * g_ref[0]).astype(BF16)


def _attn_call(qt, k, vt, gm):
    B, H, S, _ = k.shape
    tq, nh = ATT_TQ, ATT_NH
    assert ATT_CW == ATT_TK and (tq // ATT_TK) % ATT_UNROLL == 0
    grid = (B, H // nh, S // tq)
    return pl.pallas_call(
        _attn_kernel,
        grid=grid,
        in_specs=[
            pl.BlockSpec((1, nh, HEAD_SLAB, tq), lambda b, j, i: (b, j, 0, i)),
            pl.BlockSpec((1, nh, S, HEAD_SLAB), lambda b, j, i: (b, j, 0, 0),
                         pipeline_mode=pl.Buffered(1)),
            pl.BlockSpec((1, nh, VT_ROWS, S), lambda b, j, i: (b, j, 0, 0),
                         pipeline_mode=pl.Buffered(1)),
            pl.BlockSpec((1, tq, nh * MLA_V), lambda b, j, i: (b, i, j)),
        ],
        out_specs=pl.BlockSpec((1, tq, nh * MLA_V), lambda b, j, i: (b, i, j)),
        out_shape=jax.ShapeDtypeStruct((B, S, MLA_WIDTH), BF16),
        scratch_shapes=[
            pltpu.VMEM((nh, tq // ATT_CW, ATT_TK, ATT_CW), F32),
            pltpu.VMEM((nh, 1, tq), F32),
            pltpu.VMEM((nh, VT_ROWS, tq), F32),
        ],
        compiler_params=pltpu.CompilerParams(
            dimension_semantics=("parallel", "parallel", "arbitrary"),
            vmem_limit_bytes=VMEM_LIMIT),
        name="attn",
    )(qt, k, vt, gm)


def _hgrn_constants():
    C, SUB, NL = HG_CHUNK, HG_SUB, HG_LEVELS
    t = np.arange(C)
    lm = np.zeros((2 * NL, C, C), np.float32)
    masks = np.zeros((NL, C, C), np.float32)
    masks[0] = (t[:, None] // SUB) == (t[None, :] // SUB)
    for l in range(NL):
        bs = SUB << l
        same = (t[:, None] // bs) == (t[None, :] // bs)
        lm[l] = same & (t[None, :] <= t[:, None])
        lm[NL + l] = same & (t[None, :] > t[:, None])
        if l < NL - 1:
            masks[l + 1] = ((t[:, None] // bs) == (t[None, :] // bs) + 1) & ((t[None, :] // bs) % 2 == 0)
    lmat = lm.reshape(2 * NL * C, C)
    lmat = np.concatenate([lmat, lmat], axis=1)
    j = np.arange(C)
    sel = np.zeros((SUB, LANES, C), np.float32)
    sel[:] = (j[None, None, :] % SUB) == np.arange(SUB)[:, None, None]
    return (jnp.asarray(lmat, BF16), jnp.asarray(sel.reshape(SUB * LANES, C), BF16),
            jnp.asarray(masks, F32))


def _hgrn_kernel(f_ref, q_ref, v_ref, g_ref, ng_ref, lmat_ref, sel_ref, mask_ref, o_ref,
                 st_scr, qs_scr, ks_scr, c8_scr, k3_scr, p2_scr, a_scr, ds_scr, oi_scr, dec_scr):
    C, SUB, NL, TB, NH = HG_CHUNK, HG_SUB, HG_LEVELS, HG_TB, HG_HEADS
    NC = TB // C

    @pl.when(pl.program_id(1) == 0)
    def _():
        st_scr[...] = jnp.zeros_like(st_scr)

    f = f_ref[0]
    q = q_ref[0]
    g = jnp.log2(f)
    k = 1.0 - f
    g1 = g.astype(BF16)
    g2 = (g - g1.astype(F32)).astype(BF16)
    lmat = lmat_ref[...]

    for c in range(NC):
        rows = slice(c * C, (c + 1) * C)
        e = jnp.dot(lmat, jnp.concatenate([g1[rows], g2[rows]], axis=0),
                    preferred_element_type=F32)
        for l in range(NL):
            cq = e[l * C:(l + 1) * C]
            ck = e[(NL + l) * C:(NL + l + 1) * C]
            qs_scr[l, rows, :] = (q[rows] * jnp.exp2(cq)).astype(BF16)
            ks_scr[l, rows, :] = (k[rows] * jnp.exp2(ck)).astype(BF16)
            if l == 0:
                for h in range(NH):
                    c8_scr[h, c * (C // SUB):(c + 1) * (C // SUB)] = (
                        cq[:, h * LANES:(h + 1) * LANES].reshape(C // SUB, SUB, LANES))
            if l == NL - 1:
                dec_scr[c:c + 1, :] = jnp.exp2(cq[C - 1:C, :])

    tt = lax.broadcasted_iota(jnp.int32, (1, SUB, LANES), 1)
    for h in range(NH):
        hs = slice(h * LANES, (h + 1) * LANES)
        k3_scr[h] = k[:, hs].reshape(TB // SUB, SUB, LANES)
        q3 = q[:, hs].reshape(TB // SUB, SUB, LANES)
        c8 = c8_scr[h]
        for s in range(SUB):
            bcast = pl.ds(s, SUB, stride=0)
            dg = jnp.where(tt >= s, c8 - c8_scr[h, :, bcast, :], NEG_BIG)
            p = q3 * jnp.exp2(dg) * k3_scr[h, :, bcast, :]
            p2_scr[h, :, s * LANES:(s + 1) * LANES] = p.reshape(TB, LANES).astype(BF16)

    v = v_ref[0]
    sel_masks = [mask_ref[l] > 0.5 for l in range(NL)]
    for h in range(NH):
        hs = slice(h * LANES, (h + 1) * LANES)
        ad = jnp.dot(p2_scr[h], sel_ref[...], preferred_element_type=F32)
        for c in range(NC):
            rows = slice(c * C, (c + 1) * C)
            a = jnp.where(sel_masks[0], ad[rows], 0.0)
            for l in range(NL - 1):
                al = lax.dot_general(qs_scr[l, rows, hs], ks_scr[l, rows, hs], _NT,
                                     preferred_element_type=F32)
                a = jnp.where(sel_masks[l + 1], al, a)
            a_scr[h, rows, :] = a.astype(BF16)
    for h in range(NH):
        hs = slice(h * LANES, (h + 1) * LANES)
        for c in range(NC):
            rows = slice(c * C, (c + 1) * C)
            oi_scr[rows, hs] = jnp.dot(a_scr[h, rows, :], v[rows, hs], preferred_element_type=F32)
            ds_scr[h * NC + c] = lax.dot_general(v[rows, hs], ks_scr[NL - 1, rows, hs], _TN,
                                                 preferred_element_type=F32)

    for c in range(NC):
        rows = slice(c * C, (c + 1) * C)
        for h in range(NH):
            hs = slice(h * LANES, (h + 1) * LANES)
            st = st_scr[h]
            o = oi_scr[rows, hs] + lax.dot_general(qs_scr[NL - 1, rows, hs], st.astype(BF16), _NT,
                                                   preferred_element_type=F32)
            st_scr[h] = st * dec_scr[c:c + 1, hs] + ds_scr[h * NC + c]
            o = o * lax.rsqrt(jnp.mean(o * o, axis=-1, keepdims=True) + EPS) * ng_ref[:, hs]
            o_ref[0, rows, hs] = (o * g_ref[0, rows, hs]).astype(BF16)


def _hgrn_call(hf, hq, hv, gh, ng):
    B, S, W = hf.shape
    tb, C, SUB, NL, NH = HG_TB, HG_CHUNK, HG_SUB, HG_LEVELS, HG_HEADS
    assert SUB << (NL - 1) == C
    lmat, sel, masks = _hgrn_constants()
    grid = (B, S // tb)
    blk = pl.BlockSpec((1, tb, W), lambda b, t: (b, t, 0))
    full = lambda a: pl.BlockSpec(a.shape, lambda b, t: (0,) * a.ndim, pipeline_mode=pl.Buffered(1))
    return pl.pallas_call(
        _hgrn_kernel,
        grid=grid,
        in_specs=[blk, blk, blk, blk, full(ng), full(lmat), full(sel), full(masks)],
        out_specs=blk,
        out_shape=jax.ShapeDtypeStruct((B, S, HG_WIDTH), BF16),
        scratch_shapes=[
            pltpu.VMEM((NH, HG_HEAD_V, HG_EXPAND), F32),
            pltpu.VMEM((NL, tb, W), BF16),
            pltpu.VMEM((NL, tb, W), BF16),
            pltpu.VMEM((NH, tb // SUB, SUB, LANES), F32),
            pltpu.VMEM((NH, tb // SUB, SUB, LANES), F32),
            pltpu.VMEM((NH, tb, SUB * LANES), BF16),
            pltpu.VMEM((NH, tb, C), BF16),
            pltpu.VMEM((NH * (tb // C), HG_HEAD_V, HG_EXPAND), F32),
            pltpu.VMEM((tb, W), F32),
            pltpu.VMEM((tb // C, W), F32),
        ],
        compiler_params=pltpu.CompilerParams(
            dimension_semantics=("parallel", "arbitrary"),
            vmem_limit_bytes=VMEM_LIMIT),
        name="hgrn",
    )(hf, hq, hv, gh, ng, lmat, sel, masks)


def _out_kernel(ya_ref, yh_ref, x_ref, wa_ref, wh_ref, fg_ref, o_ref):
    y = jnp.dot(ya_ref[0], wa_ref[...], preferred_element_type=F32)
    y = y + jnp.dot(yh_ref[0], wh_ref[...], preferred_element_type=F32)
    o_ref[0] = _rms(x_ref[0] + y, fg_ref[...])


def _out_call(ya, yh, x, wa, wh, fg):
    B, S, _ = x.shape
    tm = OUT_TM
    grid = (B, S // tm)
    tok = lambda w: pl.BlockSpec((1, tm, w), lambda b, i: (b, i, 0))
    full = lambda a: pl.BlockSpec(a.shape, lambda b, i: (0,) * a.ndim, pipeline_mode=pl.Buffered(1))
    return pl.pallas_call(
        _out_kernel,
        grid=grid,
        in_specs=[tok(MLA_WIDTH), tok(HG_WIDTH), tok(D_MODEL), full(wa), full(wh), full(fg)],
        out_specs=tok(D_MODEL),
        out_shape=jax.ShapeDtypeStruct((B, S, D_MODEL), F32),
        compiler_params=pltpu.CompilerParams(
            dimension_semantics=("parallel", "parallel"), vmem_limit_bytes=VMEM_LIMIT),
        name="outproj",
    )(ya, yh, x, wa, wh, fg)


def _pack_weights(w_in, w_q_b, w_kv_b):
    o = 0
    w_qlat = w_in[:, o:o + MLA_Q_RANK]; o += MLA_Q_RANK
    w_kvlat = w_in[:, o:o + MLA_KV_RANK]; o += MLA_KV_RANK
    w_kr = w_in[:, o:o + MLA_ROPE]; o += MLA_ROPE
    w_rest = w_in[:, o:]
    z_kr = jnp.zeros((w_in.shape[0], LANES - MLA_ROPE), w_in.dtype)
    win = jnp.concatenate([w_qlat, w_kvlat, w_kr, z_kr, w_rest], axis=1).astype(BF16)
    assert win.shape[1] == _C_END

    wq = w_q_b.reshape(MLA_Q_RANK, MLA_HEADS, MLA_NOPE + MLA_ROPE)
    qn_, q1, q2 = wq[..., :MLA_NOPE], wq[..., MLA_NOPE:MLA_NOPE + HALF], wq[..., MLA_NOPE + HALF:]
    zq = lambda n: jnp.zeros((MLA_Q_RANK, MLA_HEADS, n), wq.dtype)
    pad = LANES - MLA_NOPE - MLA_ROPE
    wqm = jnp.transpose(jnp.concatenate([q1, q2, qn_, zq(pad)], axis=-1), (1, 2, 0)).astype(BF16)

    wkv = w_kv_b.reshape(MLA_KV_RANK, MLA_HEADS, MLA_NOPE + MLA_V)
    zk = lambda n: jnp.zeros((MLA_KV_RANK, MLA_HEADS, n), wkv.dtype)
    wk = jnp.concatenate([zk(MLA_ROPE), wkv[..., :MLA_NOPE], zk(pad)],
                         axis=-1).reshape(MLA_KV_RANK, -1).astype(BF16)
    wvt = jnp.transpose(wkv[..., MLA_NOPE:], (1, 2, 0))
    wvt = jnp.concatenate([wvt, jnp.zeros((MLA_HEADS, VT_ROWS - MLA_V, MLA_KV_RANK), wvt.dtype)],
                          axis=1).astype(BF16)
    return win, wqm, wk, wvt


def _rope_table():
    inv = ROPE_THETA ** (-jnp.arange(HALF, dtype=F32) / HALF)
    return inv.reshape(HALF, 1)


def kernel(x, positions, ln_g, w_in, q_a_norm_g, w_q_b, kv_a_norm_g, w_kv_b,
           hg_lower_bounds, hg_norm_g, w_out, final_norm_g):
    B, S, _ = x.shape
    assert ln_g.shape[0] == 1, "single-layer stack"
    win, wqm, wk, wvt = _pack_weights(w_in[0], w_q_b[0], w_kv_b[0])
    pos3 = positions.reshape(B, 1, S)
    q, k, vt, gm, hq, hf, hv, gh = _proj_call(
        x, pos3, ln_g[0:1], win, q_a_norm_g[0:1], wqm, kv_a_norm_g[0:1], wk, wvt,
        hg_lower_bounds, _rope_table())
    ya = _attn_call(q, k, vt, gm)
    yh = _hgrn_call(hf, hq, hv, gh, hg_norm_g[0:1])
    wo = w_out[0].astype(BF16)
    return _out_call(ya, yh, x, wo[:MLA_WIDTH], wo[MLA_WIDTH:], final_norm_g.reshape(1, D_MODEL))
```

```python
import functools
import math

import numpy as np
import jax
import jax.numpy as jnp
from jax import lax
from jax.experimental import pallas as pl
from jax.experimental.pallas import tpu as pltpu

F32 = jnp.float32
BF16 = jnp.bfloat16

D_MODEL = 1024
MLA_HEADS = 8
MLA_NOPE = 64
MLA_ROPE = 32
MLA_V = 64
MLA_Q_RANK = 256
MLA_KV_RANK = 128
MLA_WIDTH = MLA_HEADS * MLA_V
HG_HEADS = 4
HG_EXPAND = 128
HG_HEAD_V = 128
HG_WIDTH = HG_HEADS * HG_HEAD_V
HG_FDIM = HG_HEADS * HG_EXPAND
ROPE_THETA = 10000.0
EPS = 1e-6
HALF = MLA_ROPE // 2

LANES = 128
HEAD_SLAB = LANES
NEG_BIG = -1e30
LOG2E = math.log2(math.e)
VT_ROWS = 80

PROJ_TM = 1024
ATT_TQ = 1024
ATT_TK = 256
ATT_NH = 8
ATT_AHEAD = 8
ATT_UNROLL = 4
ATT_CW = 256
HG_CHUNK = 64
HG_SUB = 8
HG_LEVELS = 4
HG_TB = 512
PACK_ROWS = 128
OUT_TM = 2048
VMEM_LIMIT = 56 * 1024 * 1024

_C_QLAT = 0
_C_KVLAT = _C_QLAT + MLA_Q_RANK
_C_KRM = _C_KVLAT + MLA_KV_RANK
_C_GM = _C_KRM + LANES
_C_HQ = _C_GM + MLA_WIDTH
_C_HF = _C_HQ + HG_FDIM
_C_HI = _C_HF + HG_FDIM
_C_GH = _C_HI + HG_WIDTH
_C_END = _C_GH + HG_WIDTH

_NT = (((1,), (1,)), ((), ()))
_TN = (((0,), (0,)), ((), ()))


def _rms(x, g):
    return x * lax.rsqrt(jnp.mean(x * x, axis=-1, keepdims=True) + EPS) * g


def _silu(x):
    return x * jax.nn.sigmoid(x)


def _swap_rope_halves(slab):
    lane = lax.broadcasted_iota(jnp.int32, slab.shape, 1)
    return jnp.where(lane < HALF, pltpu.roll(slab, LANES - HALF, 1), pltpu.roll(slab, HALF, 1))


def _proj_kernel(x_ref, pos_ref, lng_ref, win_ref, qg_ref, wqt_ref,
                 kvg_ref, wk_ref, wvt_ref, lbraw_ref, rope_ref,
                 q_out, k_out, vt_out, gm_out, hq_out, hf_out, hv_out, gh_out):
    x = x_ref[0]
    tm = x.shape[0]
    h = _rms(x, lng_ref[...]).astype(BF16)

    def in_proj(c0, width):
        return jnp.dot(h, win_ref[:, c0:c0 + width], preferred_element_type=F32)

    lat = in_proj(0, _C_GM)

    ang_t = rope_ref[...] * pos_ref[0].astype(F32)
    cos_t = jnp.cos(ang_t)
    sin_t = jnp.sin(ang_t)
    cos_h = cos_t.T
    sin_h = sin_t.T
    pad = LANES - MLA_ROPE
    cos = jnp.concatenate([cos_h, cos_h, jnp.ones((tm, pad), F32)], axis=1)
    sin = jnp.concatenate([-sin_h, sin_h, jnp.zeros((tm, pad), F32)], axis=1)

    scale = LOG2E / math.sqrt(MLA_NOPE + MLA_ROPE)
    cos_q = jnp.concatenate([cos_t, cos_t, jnp.ones((pad, tm), F32)], axis=0) * scale
    sin_q = jnp.concatenate([-sin_t, sin_t, jnp.zeros((pad, tm), F32)], axis=0) * scale
    qn = _rms(lat[:, _C_QLAT:_C_QLAT + MLA_Q_RANK], qg_ref[...]).astype(BF16)
    for hd in range(MLA_HEADS):
        qt = lax.dot_general(wqt_ref[hd], qn, _NT, preferred_element_type=F32)
        qt_swap = jnp.concatenate([qt[HALF:MLA_ROPE], qt[:HALF], qt[MLA_ROPE:]], axis=0)
        q_out[0, hd] = (qt * cos_q + qt_swap * sin_q).astype(BF16)
    kvn = _rms(lat[:, _C_KVLAT:_C_KVLAT + MLA_KV_RANK], kvg_ref[...]).astype(BF16)
    kk = jnp.dot(kvn, wk_ref[...], preferred_element_type=F32)
    kr = lat[:, _C_KRM:_C_KRM + LANES]
    kr = kr * cos + _swap_rope_halves(kr) * sin
    ones_row = (lax.broadcasted_iota(jnp.int32, (VT_ROWS, 1), 0) == MLA_V).astype(F32)
    for hd in range(MLA_HEADS):
        sl = slice(hd * HEAD_SLAB, (hd + 1) * HEAD_SLAB)
        k_out[0, hd] = (kk[:, sl] + kr).astype(BF16)
        vt = lax.dot_general(wvt_ref[hd], kvn, _NT, preferred_element_type=F32)
        vt_out[0, hd] = (vt + ones_row).astype(BF16)
    gm_out[0] = _silu(in_proj(_C_GM, MLA_WIDTH))

    a0 = lbraw_ref[0:1, :]
    a1 = lbraw_ref[1:2, :]
    mx = jnp.maximum(a0, a1)
    e0 = jnp.exp(a0 - mx)
    e1 = jnp.exp(a1 - mx)
    lb = e0 / (e0 + e1)
    def put_heads(out, val):
        for hd in range(HG_HEADS):
            out[0, hd] = val[:, hd * LANES:(hd + 1) * LANES]

    put_heads(hq_out, _silu(in_proj(_C_HQ, HG_FDIM)))
    put_heads(hf_out, lb + (1.0 - lb) * jax.nn.sigmoid(in_proj(_C_HF, HG_FDIM)))
    put_heads(hv_out, in_proj(_C_HI, HG_WIDTH).astype(BF16))
    put_heads(gh_out, _silu(in_proj(_C_GH, HG_WIDTH)))


def _proj_call(x, pos3, lng, win, qg, wqm, kvg, wk, wvt, lbraw, rope_tab):
    B, S, _ = x.shape
    tm = PROJ_TM
    grid = (B, S // tm)
    tok = lambda w: pl.BlockSpec((1, tm, w), lambda b, i: (b, i, 0))
    full = lambda a: pl.BlockSpec(a.shape, lambda b, i: (0,) * a.ndim, pipeline_mode=pl.Buffered(1))
    head = pl.BlockSpec((1, MLA_HEADS, tm, HEAD_SLAB), lambda b, i: (b, 0, i, 0))
    out_shape = (
        jax.ShapeDtypeStruct((B, MLA_HEADS, HEAD_SLAB, S), BF16),
        jax.ShapeDtypeStruct((B, MLA_HEADS, S, HEAD_SLAB), BF16),
        jax.ShapeDtypeStruct((B, MLA_HEADS, VT_ROWS, S), BF16),
        jax.ShapeDtypeStruct((B, S, MLA_WIDTH), F32),
        jax.ShapeDtypeStruct((B, HG_HEADS, S, LANES), F32),
        jax.ShapeDtypeStruct((B, HG_HEADS, S, LANES), F32),
        jax.ShapeDtypeStruct((B, HG_HEADS, S, LANES), BF16),
        jax.ShapeDtypeStruct((B, HG_HEADS, S, LANES), F32),
    )
    hg = pl.BlockSpec((1, HG_HEADS, tm, LANES), lambda b, i: (b, 0, i, 0))
    return pl.pallas_call(
        _proj_kernel,
        grid=grid,
        in_specs=[tok(D_MODEL), pl.BlockSpec((1, 1, tm), lambda b, i: (b, 0, i)), full(lng), full(win),
                  full(qg), full(wqm), full(kvg), full(wk), full(wvt), full(lbraw), full(rope_tab)],
        out_specs=(pl.BlockSpec((1, MLA_HEADS, HEAD_SLAB, tm), lambda b, i: (b, 0, 0, i)), head,
                   pl.BlockSpec((1, MLA_HEADS, VT_ROWS, tm), lambda b, i: (b, 0, 0, i)),
                   tok(MLA_WIDTH), hg, hg, hg, hg),
        out_shape=out_shape,
        compiler_params=pltpu.CompilerParams(
            dimension_semantics=("parallel", "parallel"), vmem_limit_bytes=VMEM_LIMIT),
        name="proj",
    )(x, pos3, lng, win, qg, wqm, kvg, wk, wvt, lbraw, rope_tab)


def _attn_kernel(q_ref, k_ref, vt_ref, g_ref, o_ref, s_scr, m_scr, acc_scr):
    qi = pl.program_id(2)
    tq, tk, nh, cw = ATT_TQ, ATT_TK, ATT_NH, ATT_CW
    m_scr[...] = jnp.full(m_scr.shape, NEG_BIG, F32)
    acc_scr[...] = jnp.zeros(acc_scr.shape, F32)

    def scores(h, j, r0, diag):
        k = k_ref[0, h, pl.ds(r0, tk), :]
        q = q_ref[0, h, :, j * cw:(j + 1) * cw]
        s = jnp.dot(k, q, preferred_element_type=F32)
        if diag:
            key = lax.broadcasted_iota(jnp.int32, (tk, cw), 0)
            qry = lax.broadcasted_iota(jnp.int32, (tk, cw), 1)
            s = jnp.where(key <= qry, s, NEG_BIG)
        s_scr[h, j] = s

    def softmax_pv(h, j, r0):
        c = j * cw
        vt = vt_ref[0, h, :, pl.ds(r0, tk)]
        m_old = m_scr[h, :, c:c + cw]
        m_new = jnp.maximum(m_old, jnp.max(s_scr[h, j], axis=0, keepdims=True))
        p = jnp.exp2(s_scr[h, j] - m_new).astype(BF16)
        m_scr[h, :, c:c + cw] = m_new
        pv = jnp.dot(vt, p, preferred_element_type=F32)
        acc_scr[h, :, c:c + cw] = jnp.exp2(m_old - m_new) * acc_scr[h, :, c:c + cw] + pv

    def step(r0, c0, masked):
        units = [(h, j) for h in range(nh) for j in range(c0 // cw, tq // cw)]
        for u, (h, j) in enumerate(units[:ATT_AHEAD]):
            scores(h, j, r0, masked and j == c0 // cw)
        for u, (h, j) in enumerate(units):
            if u + ATT_AHEAD < len(units):
                h2, j2 = units[u + ATT_AHEAD]
                scores(h2, j2, r0, masked and j2 == c0 // cw)
            softmax_pv(h, j, r0)

    def full_steps(kb, carry):
        for i in range(ATT_UNROLL):
            step(pl.multiple_of((kb * ATT_UNROLL + i) * tk, tk), 0, False)
        return carry

    lax.fori_loop(0, qi * (tq // tk // ATT_UNROLL), full_steps, 0)
    for j in range(tq // tk):
        step(pl.multiple_of(qi * tq + j * tk, tk), j * tk, True)

    outs = []
    for h in range(nh):
        acc = acc_scr[h]
        o = acc[:MLA_V] / acc[MLA_V:MLA_V + 1]
        outs.append(o.T)
    o_ref[0] = (jnp.concatenate(outs, axis=1) * g_ref[0]).astype(BF16)


def _attn_call(qt, k, vt, gm):
    B, H, S, _ = k.shape
    tq, nh = ATT_TQ, ATT_NH
    assert ATT_CW == ATT_TK and (tq // ATT_TK) % ATT_UNROLL == 0
    grid = (B, H // nh, S // tq)
    return pl.pallas_call(
        _attn_kernel,
        grid=grid,
        in_specs=[
            pl.BlockSpec((1, nh, HEAD_SLAB, tq), lambda b, j, i: (b, j, 0, i)),
            pl.BlockSpec((1, nh, S, HEAD_SLAB), lambda b, j, i: (b, j, 0, 0),
                         pipeline_mode=pl.Buffered(1)),
            pl.BlockSpec((1, nh, VT_ROWS, S), lambda b, j, i: (b, j, 0, 0),
                         pipeline_mode=pl.Buffered(1)),
            pl.BlockSpec((1, tq, nh * MLA_V), lambda b, j, i: (b, i, j)),
        ],
        out_specs=pl.BlockSpec((1, tq, nh * MLA_V), lambda b, j, i: (b, i, j)),
        out_shape=jax.ShapeDtypeStruct((B, S, MLA_WIDTH), BF16),
        scratch_shapes=[
            pltpu.VMEM((nh, tq // ATT_CW, ATT_TK, ATT_CW), F32),
            pltpu.VMEM((nh, 1, tq), F32),
            pltpu.VMEM((nh, VT_ROWS, tq), F32),
        ],
        compiler_params=pltpu.CompilerParams(
            dimension_semantics=("parallel", "parallel", "arbitrary"),
            vmem_limit_bytes=VMEM_LIMIT),
        name="attn",
    )(qt, k, vt, gm)


def _hgrn_constants():
    C, SUB, NL = HG_CHUNK, HG_SUB, HG_LEVELS
    t = np.arange(C)
    lm = np.zeros((2 * NL, C, C), np.float32)
    masks = np.zeros((NL, C, C), np.float32)
    masks[0] = (t[:, None] // SUB) == (t[None, :] // SUB)
    for l in range(NL):
        bs = SUB << l
        same = (t[:, None] // bs) == (t[None, :] // bs)
        lm[l] = same & (t[None, :] <= t[:, None])
        lm[NL + l] = same & (t[None, :] > t[:, None])
        if l < NL - 1:
            masks[l + 1] = ((t[:, None] // bs) == (t[None, :] // bs) + 1) & ((t[None, :] // bs) % 2 == 0)
    lmat = lm.reshape(2 * NL * C, C)
    lmat = np.concatenate([lmat, lmat], axis=1)
    j = np.arange(C)
    sel = np.zeros((SUB, LANES, C), np.float32)
    sel[:] = (j[None, None, :] % SUB) == np.arange(SUB)[:, None, None]
    return (jnp.asarray(lmat, BF16), jnp.asarray(sel.reshape(SUB * LANES, C), BF16),
            jnp.asarray(masks, F32))


def _hgrn_kernel(f_ref, q_ref, v_ref, g_ref, ng_ref, lmat_ref, sel_ref, mask_ref, o_ref,
                 st_scr, qs_scr, ks_scr, c8_scr, k3_scr, p2_scr, a_scr, ds_scr, oi_scr, dec_scr):
    C, SUB, NL, TB, NH = HG_CHUNK, HG_SUB, HG_LEVELS, HG_TB, HG_HEADS
    NC = TB // C

    @pl.when(pl.program_id(1) == 0)
    def _():
        st_scr[...] = jnp.zeros_like(st_scr)

    lmat = lmat_ref[...]
    nb = C // SUB
    heads = range(NH)

    for c in range(NC):
        rows = slice(c * C, (c + 1) * C)
        blks = slice(c * nb, (c + 1) * nb)
        f = [f_ref[0, h, rows, :] for h in heads]
        g = jnp.log2(jnp.concatenate(f, axis=1))
        g1 = g.astype(BF16)
        g2 = (g - g1.astype(F32)).astype(BF16)
        e = jnp.dot(lmat, jnp.concatenate([g1, g2], axis=0), preferred_element_type=F32)
        for h in heads:
            hs = slice(h * LANES, (h + 1) * LANES)
            q = q_ref[0, h, rows, :]
            k = 1.0 - f[h]
            k3_scr[h, blks] = k.reshape(nb, SUB, LANES)
            for l in range(NL):
                cq = e[l * C:(l + 1) * C, hs]
                ck = e[(NL + l) * C:(NL + l + 1) * C, hs]
                qs_scr[l, h, rows, :] = (q * jnp.exp2(cq)).astype(BF16)
                ks_scr[l, h, rows, :] = (k * jnp.exp2(ck)).astype(BF16)
                if l == 0:
                    c8_scr[h, blks] = cq.reshape(nb, SUB, LANES)
                if l == NL - 1:
                    dec_scr[c:c + 1, hs] = jnp.exp2(cq[C - 1:C, :])

    tt = lax.broadcasted_iota(jnp.int32, (1, SUB, LANES), 1)
    for h in heads:
        q3 = q_ref[0, h].reshape(TB // SUB, SUB, LANES)
        c8 = c8_scr[h]
        for s in range(SUB):
            bcast = pl.ds(s, SUB, stride=0)
            dg = jnp.where(tt >= s, c8 - c8_scr[h, :, bcast, :], NEG_BIG)
            p = q3 * jnp.exp2(dg) * k3_scr[h, :, bcast, :]
            p2_scr[h, s] = p.reshape(TB, LANES).astype(BF16)

    sel_masks = [mask_ref[l] > 0.5 for l in range(NL)]
    for h in heads:
        p2 = jnp.concatenate([p2_scr[h, s] for s in range(SUB)], axis=1)
        ad = jnp.dot(p2, sel_ref[...], preferred_element_type=F32)
        for c in range(NC):
            rows = slice(c * C, (c + 1) * C)
            a = jnp.where(sel_masks[0], ad[rows], 0.0)
            for l in range(NL - 1):
                al = lax.dot_general(qs_scr[l, h, rows, :], ks_scr[l, h, rows, :], _NT,
                                     preferred_element_type=F32)
                a = jnp.where(sel_masks[l + 1], al, a)
            a_scr[h, rows, :] = a.astype(BF16)
    for h in heads:
        for c in range(NC):
            rows = slice(c * C, (c + 1) * C)
            v = v_ref[0, h, rows, :]
            oi_scr[h, rows, :] = jnp.dot(a_scr[h, rows, :], v, preferred_element_type=F32)
            ds_scr[h * NC + c] = lax.dot_general(v, ks_scr[NL - 1, h, rows, :], _TN,
                                                 preferred_element_type=F32)

    for c in range(NC):
        rows = slice(c * C, (c + 1) * C)
        for h in heads:
            hs = slice(h * LANES, (h + 1) * LANES)
            st = st_scr[h]
            o = oi_scr[h, rows, :] + lax.dot_general(qs_scr[NL - 1, h, rows, :], st.astype(BF16), _NT,
                                                     preferred_element_type=F32)
            st_scr[h] = st * dec_scr[c:c + 1, hs] + ds_scr[h * NC + c]
            o = o * lax.rsqrt(jnp.mean(o * o, axis=-1, keepdims=True) + EPS) * ng_ref[:, hs]
            o_ref[0, h, rows, :] = (o * g_ref[0, h, rows, :]).astype(BF16)


def _hgrn_call(hf, hq, hv, gh, ng):
    B, NH, S, _ = hf.shape
    tb, C, SUB, NL = HG_TB, HG_CHUNK, HG_SUB, HG_LEVELS
    assert SUB << (NL - 1) == C and NH == HG_HEADS
    lmat, sel, masks = _hgrn_constants()
    grid = (B, S // tb)
    blk = pl.BlockSpec((1, NH, tb, LANES), lambda b, t: (b, 0, t, 0))
    full = lambda a: pl.BlockSpec(a.shape, lambda b, t: (0,) * a.ndim, pipeline_mode=pl.Buffered(1))
    return pl.pallas_call(
        _hgrn_kernel,
        grid=grid,
        in_specs=[blk, blk, blk, blk, full(ng), full(lmat), full(sel), full(masks)],
        out_specs=blk,
        out_shape=jax.ShapeDtypeStruct((B, NH, S, LANES), BF16),
        scratch_shapes=[
            pltpu.VMEM((NH, HG_HEAD_V, HG_EXPAND), F32),
            pltpu.VMEM((NL, NH, tb, LANES), BF16),
            pltpu.VMEM((NL, NH, tb, LANES), BF16),
            pltpu.VMEM((NH, tb // SUB, SUB, LANES), F32),
            pltpu.VMEM((NH, tb // SUB, SUB, LANES), F32),
            pltpu.VMEM((NH, SUB, tb, LANES), BF16),
            pltpu.VMEM((NH, tb, C), BF16),
            pltpu.VMEM((NH * (tb // C), HG_HEAD_V, HG_EXPAND), F32),
            pltpu.VMEM((NH, tb, LANES), F32),
            pltpu.VMEM((tb // C, NH * LANES), F32),
        ],
        compiler_params=pltpu.CompilerParams(
            dimension_semantics=("parallel", "arbitrary"),
            vmem_limit_bytes=VMEM_LIMIT),
        name="hgrn",
    )(hf, hq, hv, gh, ng, lmat, sel, masks)


def _out_kernel(ya_ref, yh_ref, x_ref, wa_ref, wh_ref, fg_ref, o_ref):
    y = jnp.dot(ya_ref[0], wa_ref[...], preferred_element_type=F32)
    yh = jnp.concatenate([yh_ref[0, h] for h in range(HG_HEADS)], axis=1)
    y = y + jnp.dot(yh, wh_ref[...], preferred_element_type=F32)
    o_ref[0] = _rms(x_ref[0] + y, fg_ref[...])


def _out_call(ya, yh, x, wa, wh, fg):
    B, S, _ = x.shape
    tm = OUT_TM
    grid = (B, S // tm)
    tok = lambda w: pl.BlockSpec((1, tm, w), lambda b, i: (b, i, 0))
    full = lambda a: pl.BlockSpec(a.shape, lambda b, i: (0,) * a.ndim, pipeline_mode=pl.Buffered(1))
    return pl.pallas_call(
        _out_kernel,
        grid=grid,
        in_specs=[tok(MLA_WIDTH), pl.BlockSpec((1, HG_HEADS, tm, LANES), lambda b, i: (b, 0, i, 0)),
                  tok(D_MODEL), full(wa), full(wh), full(fg)],
        out_specs=tok(D_MODEL),
        out_shape=jax.ShapeDtypeStruct((B, S, D_MODEL), F32),
        compiler_params=pltpu.CompilerParams(
            dimension_semantics=("parallel", "parallel"), vmem_limit_bytes=VMEM_LIMIT),
        name="outproj",
    )(ya, yh, x, wa, wh, fg)


def _pack_win_kernel(w_ref, o_ref):
    w = w_ref[...]
    split = _C_KRM + MLA_ROPE
    zeros = jnp.zeros((w.shape[0], LANES - MLA_ROPE), w.dtype)
    o_ref[...] = jnp.concatenate([w[:, :split], zeros, w[:, split:]], axis=1).astype(BF16)


def _pack_win(w_in):
    rows, cols = w_in.shape
    tr = PACK_ROWS
    return pl.pallas_call(
        _pack_win_kernel,
        grid=(rows // tr,),
        in_specs=[pl.BlockSpec((tr, cols), lambda i: (i, 0))],
        out_specs=pl.BlockSpec((tr, _C_END), lambda i: (i, 0)),
        out_shape=jax.ShapeDtypeStruct((rows, _C_END), BF16),
        compiler_params=pltpu.CompilerParams(dimension_semantics=("parallel",)),
        name="packwin",
    )(w_in)


def _pack_weights(w_in, w_q_b, w_kv_b):
    win = _pack_win(w_in)

    wq = w_q_b.reshape(MLA_Q_RANK, MLA_HEADS, MLA_NOPE + MLA_ROPE)
    qn_, q1, q2 = wq[..., :MLA_NOPE], wq[..., MLA_NOPE:MLA_NOPE + HALF], wq[..., MLA_NOPE + HALF:]
    zq = lambda n: jnp.zeros((MLA_Q_RANK, MLA_HEADS, n), wq.dtype)
    pad = LANES - MLA_NOPE - MLA_ROPE
    wqm = jnp.transpose(jnp.concatenate([q1, q2, qn_, zq(pad)], axis=-1), (1, 2, 0)).astype(BF16)

    wkv = w_kv_b.reshape(MLA_KV_RANK, MLA_HEADS, MLA_NOPE + MLA_V)
    zk = lambda n: jnp.zeros((MLA_KV_RANK, MLA_HEADS, n), wkv.dtype)
    wk = jnp.concatenate([zk(MLA_ROPE), wkv[..., :MLA_NOPE], zk(pad)],
                         axis=-1).reshape(MLA_KV_RANK, -1).astype(BF16)
    wvt = jnp.transpose(wkv[..., MLA_NOPE:], (1, 2, 0))
    wvt = jnp.concatenate([wvt, jnp.zeros((MLA_HEADS, VT_ROWS - MLA_V, MLA_KV_RANK), wvt.dtype)],
                          axis=1).astype(BF16)
    return win, wqm, wk, wvt


def _rope_table():
    inv = ROPE_THETA ** (-jnp.arange(HALF, dtype=F32) / HALF)
    return inv.reshape(HALF, 1)


def kernel(x, positions, ln_g, w_in, q_a_norm_g, w_q_b, kv_a_norm_g, w_kv_b,
           hg_lower_bounds, hg_norm_g, w_out, final_norm_g):
    B, S, _ = x.shape
    assert ln_g.shape[0] == 1, "single-layer stack"
    win, wqm, wk, wvt = _pack_weights(w_in[0], w_q_b[0], w_kv_b[0])
    pos3 = positions.reshape(B, 1, S)
    q, k, vt, gm, hq, hf, hv, gh = _proj_call(
        x, pos3, ln_g[0:1], win, q_a_norm_g[0:1], wqm, kv_a_norm_g[0:1], wk, wvt,
        hg_lower_bounds, _rope_table())
    ya = _attn_call(q, k, vt, gm)
    yh = _hgrn_call(hf, hq, hv, gh, hg_norm_g[0:1])
    wo = w_out[0].astype(BF16)
    return _out_call(ya, yh, x, wo[:MLA_WIDTH], wo[MLA_WIDTH:], final_norm_g.reshape(1, D_MODEL))
```

```python
import functools
import math

import numpy as np
import jax
import jax.numpy as jnp
from jax import lax
from jax.experimental import pallas as pl
from jax.experimental.pallas import tpu as pltpu

F32 = jnp.float32
BF16 = jnp.bfloat16

D_MODEL = 1024
MLA_HEADS = 8
MLA_NOPE = 64
MLA_ROPE = 32
MLA_V = 64
MLA_Q_RANK = 256
MLA_KV_RANK = 128
MLA_WIDTH = MLA_HEADS * MLA_V
HG_HEADS = 4
HG_EXPAND = 128
HG_HEAD_V = 128
HG_WIDTH = HG_HEADS * HG_HEAD_V
HG_FDIM = HG_HEADS * HG_EXPAND
ROPE_THETA = 10000.0
EPS = 1e-6
HALF = MLA_ROPE // 2

LANES = 128
HEAD_SLAB = LANES
NEG_BIG = -1e30
LOG2E = math.log2(math.e)
VT_ROWS = 80

PROJ_TM = 1024
ATT_TQ = 1024
ATT_TK = 256
ATT_NH = 8
ATT_AHEAD = 8
ATT_UNROLL = 4
ATT_CW = 256
HG_CHUNK = 64
HG_SUB = 8
HG_LEVELS = 4
HG_TB = 512
PACK_ROWS = 128
OUT_TM = 2048
VMEM_LIMIT = 56 * 1024 * 1024

_C_QLAT = 0
_C_KVLAT = _C_QLAT + MLA_Q_RANK
_C_KRM = _C_KVLAT + MLA_KV_RANK
_C_GM = _C_KRM + LANES
_C_HQ = _C_GM + MLA_WIDTH
_C_HF = _C_HQ + HG_FDIM
_C_HI = _C_HF + HG_FDIM
_C_GH = _C_HI + HG_WIDTH
_C_END = _C_GH + HG_WIDTH

_NT = (((1,), (1,)), ((), ()))
_TN = (((0,), (0,)), ((), ()))


def _rms(x, g):
    return x * lax.rsqrt(jnp.mean(x * x, axis=-1, keepdims=True) + EPS) * g


def _silu(x):
    return x * jax.nn.sigmoid(x)


def _swap_rope_halves(slab):
    lane = lax.broadcasted_iota(jnp.int32, slab.shape, 1)
    return jnp.where(lane < HALF, pltpu.roll(slab, LANES - HALF, 1), pltpu.roll(slab, HALF, 1))


def _proj_kernel(x_ref, pos_ref, lng_ref, win_ref, qg_ref, wqt_ref,
                 kvg_ref, wk_ref, wvt_ref, lbraw_ref, rope_ref,
                 q_out, k_out, vt_out, gm_out, hq_out, hf_out, hv_out, gh_out):
    x = x_ref[0]
    tm = x.shape[0]
    h = _rms(x, lng_ref[...]).astype(BF16)

    def in_proj(c0, width):
        return jnp.dot(h, win_ref[:, c0:c0 + width], preferred_element_type=F32)

    lat = in_proj(0, _C_GM)

    ang_t = rope_ref[...] * pos_ref[0].astype(F32)
    cos_t = jnp.cos(ang_t)
    sin_t = jnp.sin(ang_t)
    cos_h = cos_t.T
    sin_h = sin_t.T
    pad = LANES - MLA_ROPE
    cos = jnp.concatenate([cos_h, cos_h, jnp.ones((tm, pad), F32)], axis=1)
    sin = jnp.concatenate([-sin_h, sin_h, jnp.zeros((tm, pad), F32)], axis=1)

    scale = LOG2E / math.sqrt(MLA_NOPE + MLA_ROPE)
    cos_q = jnp.concatenate([cos_t, cos_t, jnp.ones((pad, tm), F32)], axis=0) * scale
    sin_q = jnp.concatenate([-sin_t, sin_t, jnp.zeros((pad, tm), F32)], axis=0) * scale
    qn = _rms(lat[:, _C_QLAT:_C_QLAT + MLA_Q_RANK], qg_ref[...]).astype(BF16)
    for hd in range(MLA_HEADS):
        qt = lax.dot_general(wqt_ref[hd], qn, _NT, preferred_element_type=F32)
        qt_swap = jnp.concatenate([qt[HALF:MLA_ROPE], qt[:HALF], qt[MLA_ROPE:]], axis=0)
        q_out[0, hd] = (qt * cos_q + qt_swap * sin_q).astype(BF16)
    kvn = _rms(lat[:, _C_KVLAT:_C_KVLAT + MLA_KV_RANK], kvg_ref[...]).astype(BF16)
    kk = jnp.dot(kvn, wk_ref[...], preferred_element_type=F32)
    kr = lat[:, _C_KRM:_C_KRM + LANES]
    kr = kr * cos + _swap_rope_halves(kr) * sin
    ones_row = (lax.broadcasted_iota(jnp.int32, (VT_ROWS, 1), 0) == MLA_V).astype(F32)
    for hd in range(MLA_HEADS):
        sl = slice(hd * HEAD_SLAB, (hd + 1) * HEAD_SLAB)
        k_out[0, hd] = (kk[:, sl] + kr).astype(BF16)
        vt = lax.dot_general(wvt_ref[hd], kvn, _NT, preferred_element_type=F32)
        vt_out[0, hd] = (vt + ones_row).astype(BF16)
    gm_out[0] = _silu(in_proj(_C_GM, MLA_WIDTH))

    a0 = lbraw_ref[0:1, :]
    a1 = lbraw_ref[1:2, :]
    mx = jnp.maximum(a0, a1)
    e0 = jnp.exp(a0 - mx)
    e1 = jnp.exp(a1 - mx)
    lb = e0 / (e0 + e1)
    def put_heads(out, val):
        for hd in range(HG_HEADS):
            out[0, hd] = val[:, hd * LANES:(hd + 1) * LANES]

    put_heads(hq_out, _silu(in_proj(_C_HQ, HG_FDIM)))
    put_heads(hf_out, lb + (1.0 - lb) * jax.nn.sigmoid(in_proj(_C_HF, HG_FDIM)))
    put_heads(hv_out, in_proj(_C_HI, HG_WIDTH).astype(BF16))
    put_heads(gh_out, _silu(in_proj(_C_GH, HG_WIDTH)))


def _proj_call(x, pos3, lng, win, qg, wqm, kvg, wk, wvt, lbraw, rope_tab):
    B, S, _ = x.shape
    tm = PROJ_TM
    grid = (B, S // tm)
    tok = lambda w: pl.BlockSpec((1, tm, w), lambda b, i: (b, i, 0))
    full = lambda a: pl.BlockSpec(a.shape, lambda b, i: (0,) * a.ndim, pipeline_mode=pl.Buffered(1))
    head = pl.BlockSpec((1, MLA_HEADS, tm, HEAD_SLAB), lambda b, i: (b, 0, i, 0))
    out_shape = (
        jax.ShapeDtypeStruct((B, MLA_HEADS, HEAD_SLAB, S), BF16),
        jax.ShapeDtypeStruct((B, MLA_HEADS, S, HEAD_SLAB), BF16),
        jax.ShapeDtypeStruct((B, MLA_HEADS, VT_ROWS, S), BF16),
        jax.ShapeDtypeStruct((B, S, MLA_WIDTH), F32),
        jax.ShapeDtypeStruct((B, HG_HEADS, S, LANES), F32),
        jax.ShapeDtypeStruct((B, HG_HEADS, S, LANES), F32),
        jax.ShapeDtypeStruct((B, HG_HEADS, S, LANES), BF16),
        jax.ShapeDtypeStruct((B, HG_HEADS, S, LANES), F32),
    )
    hg = pl.BlockSpec((1, HG_HEADS, tm, LANES), lambda b, i: (b, 0, i, 0))
    return pl.pallas_call(
        _proj_kernel,
        grid=grid,
        in_specs=[tok(D_MODEL), pl.BlockSpec((1, 1, tm), lambda b, i: (b, 0, i)), full(lng), full(win),
                  full(qg), full(wqm), full(kvg), full(wk), full(wvt), full(lbraw), full(rope_tab)],
        out_specs=(pl.BlockSpec((1, MLA_HEADS, HEAD_SLAB, tm), lambda b, i: (b, 0, 0, i)), head,
                   pl.BlockSpec((1, MLA_HEADS, VT_ROWS, tm), lambda b, i: (b, 0, 0, i)),
                   tok(MLA_WIDTH), hg, hg, hg, hg),
        out_shape=out_shape,
        compiler_params=pltpu.CompilerParams(
            dimension_semantics=("parallel", "parallel"), vmem_limit_bytes=VMEM_LIMIT),
        name="proj",
    )(x, pos3, lng, win, qg, wqm, kvg, wk, wvt, lbraw, rope_tab)


def _attn_kernel(q_ref, k_ref, vt_ref, g_ref, o_ref, s_scr, m_scr, acc_scr):
    qi = pl.program_id(2)
    tq, tk, nh, cw = ATT_TQ, ATT_TK, ATT_NH, ATT_CW
    m_scr[...] = jnp.full(m_scr.shape, NEG_BIG, F32)
    acc_scr[...] = jnp.zeros(acc_scr.shape, F32)

    def scores(h, j, r0, diag):
        k = k_ref[0, h, pl.ds(r0, tk), :]
        q = q_ref[0, h, :, j * cw:(j + 1) * cw]
        s = jnp.dot(k, q, preferred_element_type=F32)
        if diag:
            key = lax.broadcasted_iota(jnp.int32, (tk, cw), 0)
            qry = lax.broadcasted_iota(jnp.int32, (tk, cw), 1)
            s = jnp.where(key <= qry, s, NEG_BIG)
        s_scr[h, j] = s

    def softmax_pv(h, j, r0):
        c = j * cw
        vt = vt_ref[0, h, :, pl.ds(r0, tk)]
        m_old = m_scr[h, :, c:c + cw]
        m_new = jnp.maximum(m_old, jnp.max(s_scr[h, j], axis=0, keepdims=True))
        p = jnp.exp2(s_scr[h, j] - m_new).astype(BF16)
        m_scr[h, :, c:c + cw] = m_new
        pv = jnp.dot(vt, p, preferred_element_type=F32)
        acc_scr[h, j] = jnp.exp2(m_old - m_new) * acc_scr[h, j] + pv

    def step(r0, c0, masked):
        units = [(h, j) for h in range(nh) for j in range(c0 // cw, tq // cw)]
        for u, (h, j) in enumerate(units[:ATT_AHEAD]):
            scores(h, j, r0, masked and j == c0 // cw)
        for u, (h, j) in enumerate(units):
            if u + ATT_AHEAD < len(units):
                h2, j2 = units[u + ATT_AHEAD]
                scores(h2, j2, r0, masked and j2 == c0 // cw)
            softmax_pv(h, j, r0)

    def full_steps(kb, carry):
        for i in range(ATT_UNROLL):
            step(pl.multiple_of((kb * ATT_UNROLL + i) * tk, tk), 0, False)
        return carry

    lax.fori_loop(0, qi * (tq // tk // ATT_UNROLL), full_steps, 0)
    for j in range(tq // tk):
        step(pl.multiple_of(qi * tq + j * tk, tk), j * tk, True)

    def normalized(h):
        acc = jnp.concatenate([acc_scr[h, j] for j in range(tq // cw)], axis=1)
        return acc[:MLA_V] / acc[MLA_V:MLA_V + 1]

    for h in range(0, nh, 2):
        pair = jnp.concatenate([normalized(h), normalized(h + 1)], axis=0).T
        cols = slice(h * MLA_V, (h + 2) * MLA_V)
        o_ref[0, :, cols] = (pair * g_ref[0, :, cols]).astype(BF16)


def _attn_call(qt, k, vt, gm):
    B, H, S, _ = k.shape
    tq, nh = ATT_TQ, ATT_NH
    assert ATT_CW == ATT_TK and (tq // ATT_TK) % ATT_UNROLL == 0
    grid = (B, H // nh, S // tq)
    return pl.pallas_call(
        _attn_kernel,
        grid=grid,
        in_specs=[
            pl.BlockSpec((1, nh, HEAD_SLAB, tq), lambda b, j, i: (b, j, 0, i)),
            pl.BlockSpec((1, nh, S, HEAD_SLAB), lambda b, j, i: (b, j, 0, 0),
                         pipeline_mode=pl.Buffered(1)),
            pl.BlockSpec((1, nh, VT_ROWS, S), lambda b, j, i: (b, j, 0, 0),
                         pipeline_mode=pl.Buffered(1)),
            pl.BlockSpec((1, tq, nh * MLA_V), lambda b, j, i: (b, i, j)),
        ],
        out_specs=pl.BlockSpec((1, tq, nh * MLA_V), lambda b, j, i: (b, i, j)),
        out_shape=jax.ShapeDtypeStruct((B, S, MLA_WIDTH), BF16),
        scratch_shapes=[
            pltpu.VMEM((nh, tq // ATT_CW, ATT_TK, ATT_CW), F32),
            pltpu.VMEM((nh, 1, tq), F32),
            pltpu.VMEM((nh, tq // ATT_CW, VT_ROWS, ATT_CW), F32),
        ],
        compiler_params=pltpu.CompilerParams(
            dimension_semantics=("parallel", "parallel", "arbitrary"),
            vmem_limit_bytes=VMEM_LIMIT),
        name="attn",
    )(qt, k, vt, gm)


def _hgrn_constants():
    C, SUB, NL = HG_CHUNK, HG_SUB, HG_LEVELS
    t = np.arange(C)
    lm = np.zeros((2 * NL, C, C), np.float32)
    masks = np.zeros((NL, C, C), np.float32)
    masks[0] = (t[:, None] // SUB) == (t[None, :] // SUB)
    for l in range(NL):
        bs = SUB << l
        same = (t[:, None] // bs) == (t[None, :] // bs)
        lm[l] = same & (t[None, :] <= t[:, None])
        lm[NL + l] = same & (t[None, :] > t[:, None])
        if l < NL - 1:
            masks[l + 1] = ((t[:, None] // bs) == (t[None, :] // bs) + 1) & ((t[None, :] // bs) % 2 == 0)
    lmat = lm.reshape(2 * NL * C, C)
    lmat = np.concatenate([lmat, lmat], axis=1)
    j = np.arange(C)
    sel = np.zeros((SUB, LANES, C), np.float32)
    sel[:] = (j[None, None, :] % SUB) == np.arange(SUB)[:, None, None]
    return (jnp.asarray(lmat, BF16), jnp.asarray(sel.reshape(SUB * LANES, C), BF16),
            jnp.asarray(masks, F32))


def _hgrn_kernel(f_ref, q_ref, v_ref, g_ref, ng_ref, lmat_ref, sel_ref, mask_ref, o_ref,
                 st_scr, qs_scr, ks_scr, c8_scr, k3_scr, p2_scr, a_scr, ds_scr, oi_scr, dec_scr):
    C, SUB, NL, TB, NH = HG_CHUNK, HG_SUB, HG_LEVELS, HG_TB, HG_HEADS
    NC = TB // C

    @pl.when(pl.program_id(1) == 0)
    def _():
        st_scr[...] = jnp.zeros_like(st_scr)

    lmat = lmat_ref[...]
    nb = C // SUB
    heads = range(NH)

    for c in range(NC):
        rows = slice(c * C, (c + 1) * C)
        blks = slice(c * nb, (c + 1) * nb)
        f = [f_ref[0, h, rows, :] for h in heads]
        g = jnp.log2(jnp.concatenate(f, axis=1))
        g1 = g.astype(BF16)
        g2 = (g - g1.astype(F32)).astype(BF16)
        e = jnp.dot(lmat, jnp.concatenate([g1, g2], axis=0), preferred_element_type=F32)
        for h in heads:
            hs = slice(h * LANES, (h + 1) * LANES)
            q = q_ref[0, h, rows, :]
            k = 1.0 - f[h]
            k3_scr[h, blks] = k.reshape(nb, SUB, LANES)
            for l in range(NL):
                cq = e[l * C:(l + 1) * C, hs]
                ck = e[(NL + l) * C:(NL + l + 1) * C, hs]
                qs_scr[l, h, rows, :] = (q * jnp.exp2(cq)).astype(BF16)
                ks_scr[l, h, rows, :] = (k * jnp.exp2(ck)).astype(BF16)
                if l == 0:
                    c8_scr[h, blks] = cq.reshape(nb, SUB, LANES)
                if l == NL - 1:
                    dec_scr[c:c + 1, hs] = jnp.exp2(cq[C - 1:C, :])

    tt = lax.broadcasted_iota(jnp.int32, (1, SUB, LANES), 1)
    for h in heads:
        q3 = q_ref[0, h].reshape(TB // SUB, SUB, LANES)
        c8 = c8_scr[h]
        for s in range(SUB):
            bcast = pl.ds(s, SUB, stride=0)
            dg = jnp.where(tt >= s, c8 - c8_scr[h, :, bcast, :], NEG_BIG)
            p = q3 * jnp.exp2(dg) * k3_scr[h, :, bcast, :]
            p2_scr[h, s] = p.reshape(TB, LANES).astype(BF16)

    sel_masks = [mask_ref[l] > 0.5 for l in range(NL)]
    for h in heads:
        p2 = jnp.concatenate([p2_scr[h, s] for s in range(SUB)], axis=1)
        ad = jnp.dot(p2, sel_ref[...], preferred_element_type=F32)
        for c in range(NC):
            rows = slice(c * C, (c + 1) * C)
            a = jnp.where(sel_masks[0], ad[rows], 0.0)
            for l in range(NL - 1):
                al = lax.dot_general(qs_scr[l, h, rows, :], ks_scr[l, h, rows, :], _NT,
                                     preferred_element_type=F32)
                a = jnp.where(sel_masks[l + 1], al, a)
            a_scr[h, rows, :] = a.astype(BF16)
    for h in heads:
        for c in range(NC):
            rows = slice(c * C, (c + 1) * C)
            v = v_ref[0, h, rows, :]
            oi_scr[h, rows, :] = jnp.dot(a_scr[h, rows, :], v, preferred_element_type=F32)
            ds_scr[h * NC + c] = lax.dot_general(v, ks_scr[NL - 1, h, rows, :], _TN,
                                                 preferred_element_type=F32)

    for c in range(NC):
        rows = slice(c * C, (c + 1) * C)
        for h in heads:
            hs = slice(h * LANES, (h + 1) * LANES)
            st = st_scr[h]
            o = oi_scr[h, rows, :] + lax.dot_general(qs_scr[NL - 1, h, rows, :], st.astype(BF16), _NT,
                                                     preferred_element_type=F32)
            st_scr[h] = st * dec_scr[c:c + 1, hs] + ds_scr[h * NC + c]
            o = o * lax.rsqrt(jnp.mean(o * o, axis=-1, keepdims=True) + EPS) * ng_ref[:, hs]
            o_ref[0, h, rows, :] = (o * g_ref[0, h, rows, :]).astype(BF16)


def _hgrn_call(hf, hq, hv, gh, ng):
    B, NH, S, _ = hf.shape
    tb, C, SUB, NL = HG_TB, HG_CHUNK, HG_SUB, HG_LEVELS
    assert SUB << (NL - 1) == C and NH == HG_HEADS
    lmat, sel, masks = _hgrn_constants()
    grid = (B, S // tb)
    blk = pl.BlockSpec((1, NH, tb, LANES), lambda b, t: (b, 0, t, 0))
    full = lambda a: pl.BlockSpec(a.shape, lambda b, t: (0,) * a.ndim, pipeline_mode=pl.Buffered(1))
    return pl.pallas_call(
        _hgrn_kernel,
        grid=grid,
        in_specs=[blk, blk, blk, blk, full(ng), full(lmat), full(sel), full(masks)],
        out_specs=blk,
        out_shape=jax.ShapeDtypeStruct((B, NH, S, LANES), BF16),
        scratch_shapes=[
            pltpu.VMEM((NH, HG_HEAD_V, HG_EXPAND), F32),
            pltpu.VMEM((NL, NH, tb, LANES), BF16),
            pltpu.VMEM((NL, NH, tb, LANES), BF16),
            pltpu.VMEM((NH, tb // SUB, SUB, LANES), F32),
            pltpu.VMEM((NH, tb // SUB, SUB, LANES), F32),
            pltpu.VMEM((NH, SUB, tb, LANES), BF16),
            pltpu.VMEM((NH, tb, C), BF16),
            pltpu.VMEM((NH * (tb // C), HG_HEAD_V, HG_EXPAND), F32),
            pltpu.VMEM((NH, tb, LANES), F32),
            pltpu.VMEM((tb // C, NH * LANES), F32),
        ],
        compiler_params=pltpu.CompilerParams(
            dimension_semantics=("parallel", "arbitrary"),
            vmem_limit_bytes=VMEM_LIMIT),
        name="hgrn",
    )(hf, hq, hv, gh, ng, lmat, sel, masks)


def _out_kernel(ya_ref, yh_ref, x_ref, wa_ref, wh_ref, fg_ref, o_ref):
    y = jnp.dot(ya_ref[0], wa_ref[...], preferred_element_type=F32)
    yh = jnp.concatenate([yh_ref[0, h] for h in range(HG_HEADS)], axis=1)
    y = y + jnp.dot(yh, wh_ref[...], preferred_element_type=F32)
    o_ref[0] = _rms(x_ref[0] + y, fg_ref[...])


def _out_call(ya, yh, x, wa, wh, fg):
    B, S, _ = x.shape
    tm = OUT_TM
    grid = (B, S // tm)
    tok = lambda w: pl.BlockSpec((1, tm, w), lambda b, i: (b, i, 0))
    full = lambda a: pl.BlockSpec(a.shape, lambda b, i: (0,) * a.ndim, pipeline_mode=pl.Buffered(1))
    return pl.pallas_call(
        _out_kernel,
        grid=grid,
        in_specs=[tok(MLA_WIDTH), pl.BlockSpec((1, HG_HEADS, tm, LANES), lambda b, i: (b, 0, i, 0)),
                  tok(D_MODEL), full(wa), full(wh), full(fg)],
        out_specs=tok(D_MODEL),
        out_shape=jax.ShapeDtypeStruct((B, S, D_MODEL), F32),
        compiler_params=pltpu.CompilerParams(
            dimension_semantics=("parallel", "parallel"), vmem_limit_bytes=VMEM_LIMIT),
        name="outproj",
    )(ya, yh, x, wa, wh, fg)


def _pack_win_kernel(w_ref, o_ref):
    w = w_ref[...]
    split = _C_KRM + MLA_ROPE
    zeros = jnp.zeros((w.shape[0], LANES - MLA_ROPE), w.dtype)
    o_ref[...] = jnp.concatenate([w[:, :split], zeros, w[:, split:]], axis=1).astype(BF16)


def _pack_win(w_in):
    rows, cols = w_in.shape
    tr = PACK_ROWS
    return pl.pallas_call(
        _pack_win_kernel,
        grid=(rows // tr,),
        in_specs=[pl.BlockSpec((tr, cols), lambda i: (i, 0))],
        out_specs=pl.BlockSpec((tr, _C_END), lambda i: (i, 0)),
        out_shape=jax.ShapeDtypeStruct((rows, _C_END), BF16),
        compiler_params=pltpu.CompilerParams(dimension_semantics=("parallel",)),
        name="packwin",
    )(w_in)


def _pack_weights(w_in, w_q_b, w_kv_b):
    win = _pack_win(w_in)

    wq = w_q_b.reshape(MLA_Q_RANK, MLA_HEADS, MLA_NOPE + MLA_ROPE)
    qn_, q1, q2 = wq[..., :MLA_NOPE], wq[..., MLA_NOPE:MLA_NOPE + HALF], wq[..., MLA_NOPE + HALF:]
    zq = lambda n: jnp.zeros((MLA_Q_RANK, MLA_HEADS, n), wq.dtype)
    pad = LANES - MLA_NOPE - MLA_ROPE
    wqm = jnp.transpose(jnp.concatenate([q1, q2, qn_, zq(pad)], axis=-1), (1, 2, 0)).astype(BF16)

    wkv = w_kv_b.reshape(MLA_KV_RANK, MLA_HEADS, MLA_NOPE + MLA_V)
    zk = lambda n: jnp.zeros((MLA_KV_RANK, MLA_HEADS, n), wkv.dtype)
    wk = jnp.concatenate([zk(MLA_ROPE), wkv[..., :MLA_NOPE], zk(pad)],
                         axis=-1).reshape(MLA_KV_RANK, -1).astype(BF16)
    wvt = jnp.transpose(wkv[..., MLA_NOPE:], (1, 2, 0))
    wvt = jnp.concatenate([wvt, jnp.zeros((MLA_HEADS, VT_ROWS - MLA_V, MLA_KV_RANK), wvt.dtype)],
                          axis=1).astype(BF16)
    return win, wqm, wk, wvt


def _rope_table():
    inv = ROPE_THETA ** (-jnp.arange(HALF, dtype=F32) / HALF)
    return inv.reshape(HALF, 1)


def kernel(x, positions, ln_g, w_in, q_a_norm_g, w_q_b, kv_a_norm_g, w_kv_b,
           hg_lower_bounds, hg_norm_g, w_out, final_norm_g):
    B, S, _ = x.shape
    assert ln_g.shape[0] == 1, "single-layer stack"
    win, wqm, wk, wvt = _pack_weights(w_in[0], w_q_b[0], w_kv_b[0])
    pos3 = positions.reshape(B, 1, S)
    q, k, vt, gm, hq, hf, hv, gh = _proj_call(
        x, pos3, ln_g[0:1], win, q_a_norm_g[0:1], wqm, kv_a_norm_g[0:1], wk, wvt,
        hg_lower_bounds, _rope_table())
    ya = _attn_call(q, k, vt, gm)
    yh = _hgrn_call(hf, hq, hv, gh, hg_norm_g[0:1])
    wo = w_out[0].astype(BF16)
    return _out_call(ya, yh, x, wo[:MLA_WIDTH], wo[MLA_WIDTH:], final_norm_g.reshape(1, D_MODEL))
```

```python
import functools
import math

import numpy as np
import jax
import jax.numpy as jnp
from jax import lax
from jax.experimental import pallas as pl
from jax.experimental.pallas import tpu as pltpu

F32 = jnp.float32
BF16 = jnp.bfloat16

D_MODEL = 1024
MLA_HEADS = 8
MLA_NOPE = 64
MLA_ROPE = 32
MLA_V = 64
MLA_Q_RANK = 256
MLA_KV_RANK = 128
MLA_WIDTH = MLA_HEADS * MLA_V
HG_HEADS = 4
HG_EXPAND = 128
HG_HEAD_V = 128
HG_WIDTH = HG_HEADS * HG_HEAD_V
HG_FDIM = HG_HEADS * HG_EXPAND
ROPE_THETA = 10000.0
EPS = 1e-6
HALF = MLA_ROPE // 2

LANES = 128
HEAD_SLAB = LANES
NEG_BIG = -1e30
LOG2E = math.log2(math.e)
VT_ROWS = 80

PROJ_TM = 1024
ATT_TQ = 1024
ATT_TK = 256
ATT_NH = 8
ATT_AHEAD = 8
ATT_UNROLL = 4
ATT_CW = 256
HG_CHUNK = 64
HG_SUB = 8
HG_LEVELS = 4
HG_TB = 512
PACK_ROWS = 128
OUT_TM = 2048
VMEM_LIMIT = 56 * 1024 * 1024

_C_QLAT = 0
_C_KVLAT = _C_QLAT + MLA_Q_RANK
_C_KRM = _C_KVLAT + MLA_KV_RANK
_C_GM = _C_KRM + LANES
_C_HQ = _C_GM + MLA_WIDTH
_C_HF = _C_HQ + HG_FDIM
_C_HI = _C_HF + HG_FDIM
_C_GH = _C_HI + HG_WIDTH
_C_END = _C_GH + HG_WIDTH

_NT = (((1,), (1,)), ((), ()))
_TN = (((0,), (0,)), ((), ()))


def _rms(x, g):
    return x * lax.rsqrt(jnp.mean(x * x, axis=-1, keepdims=True) + EPS) * g


def _silu(x):
    return x * jax.nn.sigmoid(x)


def _swap_rope_halves(slab):
    lane = lax.broadcasted_iota(jnp.int32, slab.shape, 1)
    return jnp.where(lane < HALF, pltpu.roll(slab, LANES - HALF, 1), pltpu.roll(slab, HALF, 1))


def _proj_kernel(x_ref, pos_ref, lng_ref, win_ref, qg_ref, wqt_ref,
                 kvg_ref, wk_ref, wvt_ref, lbraw_ref, rope_ref,
                 q_out, k_out, vt_out, gm_out, hq_out, hf_out, hv_out, gh_out):
    x = x_ref[0]
    tm = x.shape[0]
    h = _rms(x, lng_ref[...]).astype(BF16)

    def in_proj(c0, width):
        return jnp.dot(h, win_ref[:, c0:c0 + width], preferred_element_type=F32)

    lat = in_proj(0, _C_GM)

    ang_t = rope_ref[...] * pos_ref[0].astype(F32)
    cos_t = jnp.cos(ang_t)
    sin_t = jnp.sin(ang_t)
    cos_h = cos_t.T
    sin_h = sin_t.T
    pad = LANES - MLA_ROPE
    cos = jnp.concatenate([cos_h, cos_h, jnp.ones((tm, pad), F32)], axis=1)
    sin = jnp.concatenate([-sin_h, sin_h, jnp.zeros((tm, pad), F32)], axis=1)

    scale = LOG2E / math.sqrt(MLA_NOPE + MLA_ROPE)
    cos_q = jnp.concatenate([cos_t, cos_t, jnp.ones((pad, tm), F32)], axis=0) * scale
    sin_q = jnp.concatenate([-sin_t, sin_t, jnp.zeros((pad, tm), F32)], axis=0) * scale
    qn = _rms(lat[:, _C_QLAT:_C_QLAT + MLA_Q_RANK], qg_ref[...]).astype(BF16)
    for hd in range(MLA_HEADS):
        qt = lax.dot_general(wqt_ref[hd], qn, _NT, preferred_element_type=F32)
        qt_swap = jnp.concatenate([qt[HALF:MLA_ROPE], qt[:HALF], qt[MLA_ROPE:]], axis=0)
        q_out[0, hd] = (qt * cos_q + qt_swap * sin_q).astype(BF16)
    kvn = _rms(lat[:, _C_KVLAT:_C_KVLAT + MLA_KV_RANK], kvg_ref[...]).astype(BF16)
    kk = jnp.dot(kvn, wk_ref[...], preferred_element_type=F32)
    kr = lat[:, _C_KRM:_C_KRM + LANES]
    kr = kr * cos + _swap_rope_halves(kr) * sin
    vt_all = lax.dot_general(wvt_ref[...], kvn, _NT, preferred_element_type=F32).astype(BF16)
    tail = (lax.broadcasted_iota(jnp.int32, (VT_ROWS - MLA_V, tm), 0) == 0).astype(BF16)
    for hd in range(MLA_HEADS):
        sl = slice(hd * HEAD_SLAB, (hd + 1) * HEAD_SLAB)
        k_out[0, hd] = (kk[:, sl] + kr).astype(BF16)
        vt_out[0, hd] = jnp.concatenate([vt_all[hd * MLA_V:(hd + 1) * MLA_V], tail], axis=0)
    gm_out[0] = _silu(in_proj(_C_GM, MLA_WIDTH))

    a0 = lbraw_ref[0:1, :]
    a1 = lbraw_ref[1:2, :]
    mx = jnp.maximum(a0, a1)
    e0 = jnp.exp(a0 - mx)
    e1 = jnp.exp(a1 - mx)
    lb = e0 / (e0 + e1)
    def put_heads(out, val):
        for hd in range(HG_HEADS):
            out[0, hd] = val[:, hd * LANES:(hd + 1) * LANES]

    put_heads(hq_out, _silu(in_proj(_C_HQ, HG_FDIM)))
    put_heads(hf_out, lb + (1.0 - lb) * jax.nn.sigmoid(in_proj(_C_HF, HG_FDIM)))
    put_heads(hv_out, in_proj(_C_HI, HG_WIDTH).astype(BF16))
    put_heads(gh_out, _silu(in_proj(_C_GH, HG_WIDTH)))


def _proj_call(x, pos3, lng, win, qg, wqm, kvg, wk, wvt, lbraw, rope_tab):
    B, S, _ = x.shape
    tm = PROJ_TM
    grid = (B, S // tm)
    tok = lambda w: pl.BlockSpec((1, tm, w), lambda b, i: (b, i, 0))
    full = lambda a: pl.BlockSpec(a.shape, lambda b, i: (0,) * a.ndim, pipeline_mode=pl.Buffered(1))
    head = pl.BlockSpec((1, MLA_HEADS, tm, HEAD_SLAB), lambda b, i: (b, 0, i, 0))
    out_shape = (
        jax.ShapeDtypeStruct((B, MLA_HEADS, HEAD_SLAB, S), BF16),
        jax.ShapeDtypeStruct((B, MLA_HEADS, S, HEAD_SLAB), BF16),
        jax.ShapeDtypeStruct((B, MLA_HEADS, VT_ROWS, S), BF16),
        jax.ShapeDtypeStruct((B, S, MLA_WIDTH), F32),
        jax.ShapeDtypeStruct((B, HG_HEADS, S, LANES), F32),
        jax.ShapeDtypeStruct((B, HG_HEADS, S, LANES), F32),
        jax.ShapeDtypeStruct((B, HG_HEADS, S, LANES), BF16),
        jax.ShapeDtypeStruct((B, HG_HEADS, S, LANES), F32),
    )
    hg = pl.BlockSpec((1, HG_HEADS, tm, LANES), lambda b, i: (b, 0, i, 0))
    return pl.pallas_call(
        _proj_kernel,
        grid=grid,
        in_specs=[tok(D_MODEL), pl.BlockSpec((1, 1, tm), lambda b, i: (b, 0, i)), full(lng), full(win),
                  full(qg), full(wqm), full(kvg), full(wk), full(wvt), full(lbraw), full(rope_tab)],
        out_specs=(pl.BlockSpec((1, MLA_HEADS, HEAD_SLAB, tm), lambda b, i: (b, 0, 0, i)), head,
                   pl.BlockSpec((1, MLA_HEADS, VT_ROWS, tm), lambda b, i: (b, 0, 0, i)),
                   tok(MLA_WIDTH), hg, hg, hg, hg),
        out_shape=out_shape,
        compiler_params=pltpu.CompilerParams(
            dimension_semantics=("parallel", "parallel"), vmem_limit_bytes=VMEM_LIMIT),
        name="proj",
    )(x, pos3, lng, win, qg, wqm, kvg, wk, wvt, lbraw, rope_tab)


def _attn_kernel(q_ref, k_ref, vt_ref, g_ref, o_ref, s_scr, m_scr, acc_scr):
    qi = pl.program_id(2)
    tq, tk, nh, cw = ATT_TQ, ATT_TK, ATT_NH, ATT_CW
    m_scr[...] = jnp.full(m_scr.shape, NEG_BIG, F32)
    acc_scr[...] = jnp.zeros(acc_scr.shape, F32)

    def scores(h, j, r0, diag):
        k = k_ref[0, h, pl.ds(r0, tk), :]
        q = q_ref[0, h, :, j * cw:(j + 1) * cw]
        s = jnp.dot(k, q, preferred_element_type=F32)
        if diag:
            key = lax.broadcasted_iota(jnp.int32, (tk, cw), 0)
            qry = lax.broadcasted_iota(jnp.int32, (tk, cw), 1)
            s = jnp.where(key <= qry, s, NEG_BIG)
        s_scr[h, j] = s

    def softmax_pv(h, j, r0):
        c = j * cw
        vt = vt_ref[0, h, :, pl.ds(r0, tk)]
        m_old = m_scr[h, :, c:c + cw]
        m_new = jnp.maximum(m_old, jnp.max(s_scr[h, j], axis=0, keepdims=True))
        p = jnp.exp2(s_scr[h, j] - m_new).astype(BF16)
        m_scr[h, :, c:c + cw] = m_new
        pv = jnp.dot(vt, p, preferred_element_type=F32)
        acc_scr[h, j] = jnp.exp2(m_old - m_new) * acc_scr[h, j] + pv

    def step(r0, c0, masked):
        units = [(h, j) for h in range(nh) for j in range(c0 // cw, tq // cw)]
        for u, (h, j) in enumerate(units[:ATT_AHEAD]):
            scores(h, j, r0, masked and j == c0 // cw)
        for u, (h, j) in enumerate(units):
            if u + ATT_AHEAD < len(units):
                h2, j2 = units[u + ATT_AHEAD]
                scores(h2, j2, r0, masked and j2 == c0 // cw)
            softmax_pv(h, j, r0)

    def full_steps(kb, carry):
        for i in range(ATT_UNROLL):
            step(pl.multiple_of((kb * ATT_UNROLL + i) * tk, tk), 0, False)
        return carry

    lax.fori_loop(0, qi * (tq // tk // ATT_UNROLL), full_steps, 0)
    for j in range(tq // tk):
        step(pl.multiple_of(qi * tq + j * tk, tk), j * tk, True)

    def normalized(h):
        acc = jnp.concatenate([acc_scr[h, j] for j in range(tq // cw)], axis=1)
        return acc[:MLA_V] / acc[MLA_V:MLA_V + 1]

    for h in range(0, nh, 2):
        pair = jnp.concatenate([normalized(h), normalized(h + 1)], axis=0).T
        cols = slice(h * MLA_V, (h + 2) * MLA_V)
        o_ref[0, :, cols] = (pair * g_ref[0, :, cols]).astype(BF16)


def _attn_call(qt, k, vt, gm):
    B, H, S, _ = k.shape
    tq, nh = ATT_TQ, ATT_NH
    assert ATT_CW == ATT_TK and (tq // ATT_TK) % ATT_UNROLL == 0
    grid = (B, H // nh, S // tq)
    return pl.pallas_call(
        _attn_kernel,
        grid=grid,
        in_specs=[
            pl.BlockSpec((1, nh, HEAD_SLAB, tq), lambda b, j, i: (b, j, 0, i)),
            pl.BlockSpec((1, nh, S, HEAD_SLAB), lambda b, j, i: (b, j, 0, 0),
                         pipeline_mode=pl.Buffered(1)),
            pl.BlockSpec((1, nh, VT_ROWS, S), lambda b, j, i: (b, j, 0, 0),
                         pipeline_mode=pl.Buffered(1)),
            pl.BlockSpec((1, tq, nh * MLA_V), lambda b, j, i: (b, i, j)),
        ],
        out_specs=pl.BlockSpec((1, tq, nh * MLA_V), lambda b, j, i: (b, i, j)),
        out_shape=jax.ShapeDtypeStruct((B, S, MLA_WIDTH), BF16),
        scratch_shapes=[
            pltpu.VMEM((nh, tq // ATT_CW, ATT_TK, ATT_CW), F32),
            pltpu.VMEM((nh, 1, tq), F32),
            pltpu.VMEM((nh, tq // ATT_CW, VT_ROWS, ATT_CW), F32),
        ],
        compiler_params=pltpu.CompilerParams(
            dimension_semantics=("parallel", "parallel", "arbitrary"),
            vmem_limit_bytes=VMEM_LIMIT),
        name="attn",
    )(qt, k, vt, gm)


def _hgrn_constants():
    C, SUB, NL = HG_CHUNK, HG_SUB, HG_LEVELS
    t = np.arange(C)
    lm = np.zeros((2 * NL, C, C), np.float32)
    masks = np.zeros((NL, C, C), np.float32)
    masks[0] = (t[:, None] // SUB) == (t[None, :] // SUB)
    for l in range(NL):
        bs = SUB << l
        same = (t[:, None] // bs) == (t[None, :] // bs)
        lm[l] = same & (t[None, :] <= t[:, None])
        lm[NL + l] = same & (t[None, :] > t[:, None])
        if l < NL - 1:
            masks[l + 1] = ((t[:, None] // bs) == (t[None, :] // bs) + 1) & ((t[None, :] // bs) % 2 == 0)
    lmat = lm.reshape(2 * NL * C, C)
    lmat = np.concatenate([lmat, lmat], axis=1)
    j = np.arange(C)
    sel = np.zeros((SUB, LANES, C), np.float32)
    sel[:] = (j[None, None, :] % SUB) == np.arange(SUB)[:, None, None]
    return (jnp.asarray(lmat, BF16), jnp.asarray(sel.reshape(SUB * LANES, C), BF16),
            jnp.asarray(masks, F32))


def _hgrn_kernel(f_ref, q_ref, v_ref, g_ref, ng_ref, lmat_ref, sel_ref, mask_ref, o_ref,
                 st_scr, qs_scr, ks_scr, c8_scr, r8_scr, p2_scr, a_scr, ds_scr, oi_scr, dec_scr):
    C, SUB, NL, TB, NH = HG_CHUNK, HG_SUB, HG_LEVELS, HG_TB, HG_HEADS
    NC = TB // C

    @pl.when(pl.program_id(1) == 0)
    def _():
        st_scr[...] = jnp.zeros_like(st_scr)

    lmat = lmat_ref[...]
    nb = C // SUB
    heads = range(NH)

    for c in range(NC):
        rows = slice(c * C, (c + 1) * C)
        blks = slice(c * nb, (c + 1) * nb)
        f = [f_ref[0, h, rows, :] for h in heads]
        g = jnp.log2(jnp.concatenate(f, axis=1))
        g1 = g.astype(BF16)
        g2 = (g - g1.astype(F32)).astype(BF16)
        e = jnp.dot(lmat, jnp.concatenate([g1, g2], axis=0), preferred_element_type=F32)
        for h in heads:
            hs = slice(h * LANES, (h + 1) * LANES)
            q = q_ref[0, h, rows, :]
            k = 1.0 - f[h]
            for l in range(NL):
                cq = e[l * C:(l + 1) * C, hs]
                ck = e[(NL + l) * C:(NL + l + 1) * C, hs]
                qs_scr[l, h, rows, :] = (q * jnp.exp2(cq)).astype(BF16)
                ks_scr[l, h, rows, :] = (k * jnp.exp2(ck)).astype(BF16)
                if l == 0:
                    c8_scr[h, blks] = cq.reshape(nb, SUB, LANES)
                    r8_scr[h, blks] = (cq - jnp.log2(k)).reshape(nb, SUB, LANES)
                if l == NL - 1:
                    dec_scr[c:c + 1, hs] = jnp.exp2(cq[C - 1:C, :])

    tt = lax.broadcasted_iota(jnp.int32, (1, SUB, LANES), 1)
    for h in heads:
        q3 = q_ref[0, h].reshape(TB // SUB, SUB, LANES)
        c8 = c8_scr[h]
        for s in range(SUB):
            bcast = pl.ds(s, SUB, stride=0)
            dg = jnp.where(tt >= s, c8 - r8_scr[h, :, bcast, :], NEG_BIG)
            p2_scr[h, s] = (q3 * jnp.exp2(dg)).reshape(TB, LANES).astype(BF16)

    sel_masks = [mask_ref[l] > 0.5 for l in range(NL)]
    for h in heads:
        p2 = jnp.concatenate([p2_scr[h, s] for s in range(SUB)], axis=1)
        ad = jnp.dot(p2, sel_ref[...], preferred_element_type=F32)
        for c in range(NC):
            rows = slice(c * C, (c + 1) * C)
            a = jnp.where(sel_masks[0], ad[rows], 0.0)
            for l in range(NL - 1):
                al = lax.dot_general(qs_scr[l, h, rows, :], ks_scr[l, h, rows, :], _NT,
                                     preferred_element_type=F32)
                a = jnp.where(sel_masks[l + 1], al, a)
            a_scr[h, rows, :] = a.astype(BF16)
    for h in heads:
        for c in range(NC):
            rows = slice(c * C, (c + 1) * C)
            v = v_ref[0, h, rows, :]
            oi_scr[h, rows, :] = jnp.dot(a_scr[h, rows, :], v, preferred_element_type=F32)
            ds_scr[h * NC + c] = lax.dot_general(v, ks_scr[NL - 1, h, rows, :], _TN,
                                                 preferred_element_type=F32)

    for c in range(NC):
        rows = slice(c * C, (c + 1) * C)
        for h in heads:
            hs = slice(h * LANES, (h + 1) * LANES)
            st = st_scr[h]
            o = oi_scr[h, rows, :] + lax.dot_general(qs_scr[NL - 1, h, rows, :], st.astype(BF16), _NT,
                                                     preferred_element_type=F32)
            st_scr[h] = st * dec_scr[c:c + 1, hs] + ds_scr[h * NC + c]
            o = o * lax.rsqrt(jnp.mean(o * o, axis=-1, keepdims=True) + EPS) * ng_ref[:, hs]
            o_ref[0, h, rows, :] = (o * g_ref[0, h, rows, :]).astype(BF16)


def _hgrn_call(hf, hq, hv, gh, ng):
    B, NH, S, _ = hf.shape
    tb, C, SUB, NL = HG_TB, HG_CHUNK, HG_SUB, HG_LEVELS
    assert SUB << (NL - 1) == C and NH == HG_HEADS
    lmat, sel, masks = _hgrn_constants()
    grid = (B, S // tb)
    blk = pl.BlockSpec((1, NH, tb, LANES), lambda b, t: (b, 0, t, 0))
    full = lambda a: pl.BlockSpec(a.shape, lambda b, t: (0,) * a.ndim, pipeline_mode=pl.Buffered(1))
    return pl.pallas_call(
        _hgrn_kernel,
        grid=grid,
        in_specs=[blk, blk, blk, blk, full(ng), full(lmat), full(sel), full(masks)],
        out_specs=blk,
        out_shape=jax.ShapeDtypeStruct((B, NH, S, LANES), BF16),
        scratch_shapes=[
            pltpu.VMEM((NH, HG_HEAD_V, HG_EXPAND), F32),
            pltpu.VMEM((NL, NH, tb, LANES), BF16),
            pltpu.VMEM((NL, NH, tb, LANES), BF16),
            pltpu.VMEM((NH, tb // SUB, SUB, LANES), F32),
            pltpu.VMEM((NH, tb // SUB, SUB, LANES), F32),
            pltpu.VMEM((NH, SUB, tb, LANES), BF16),
            pltpu.VMEM((NH, tb, C), BF16),
            pltpu.VMEM((NH * (tb // C), HG_HEAD_V, HG_EXPAND), F32),
            pltpu.VMEM((NH, tb, LANES), F32),
            pltpu.VMEM((tb // C, NH * LANES), F32),
        ],
        compiler_params=pltpu.CompilerParams(
            dimension_semantics=("parallel", "arbitrary"),
            vmem_limit_bytes=VMEM_LIMIT),
        name="hgrn",
    )(hf, hq, hv, gh, ng, lmat, sel, masks)


def _out_kernel(ya_ref, yh_ref, x_ref, wa_ref, wh_ref, fg_ref, o_ref):
    y = jnp.dot(ya_ref[0], wa_ref[...], preferred_element_type=F32)
    yh = jnp.concatenate([yh_ref[0, h] for h in range(HG_HEADS)], axis=1)
    y = y + jnp.dot(yh, wh_ref[...], preferred_element_type=F32)
    o_ref[0] = _rms(x_ref[0] + y, fg_ref[...])


def _out_call(ya, yh, x, wa, wh, fg):
    B, S, _ = x.shape
    tm = OUT_TM
    grid = (B, S // tm)
    tok = lambda w: pl.BlockSpec((1, tm, w), lambda b, i: (b, i, 0))
    full = lambda a: pl.BlockSpec(a.shape, lambda b, i: (0,) * a.ndim, pipeline_mode=pl.Buffered(1))
    return pl.pallas_call(
        _out_kernel,
        grid=grid,
        in_specs=[tok(MLA_WIDTH), pl.BlockSpec((1, HG_HEADS, tm, LANES), lambda b, i: (b, 0, i, 0)),
                  tok(D_MODEL), full(wa), full(wh), full(fg)],
        out_specs=tok(D_MODEL),
        out_shape=jax.ShapeDtypeStruct((B, S, D_MODEL), F32),
        compiler_params=pltpu.CompilerParams(
            dimension_semantics=("parallel", "parallel"), vmem_limit_bytes=VMEM_LIMIT),
        name="outproj",
    )(ya, yh, x, wa, wh, fg)


def _pack_win_kernel(w_ref, o_ref):
    w = w_ref[...]
    split = _C_KRM + MLA_ROPE
    zeros = jnp.zeros((w.shape[0], LANES - MLA_ROPE), w.dtype)
    o_ref[...] = jnp.concatenate([w[:, :split], zeros, w[:, split:]], axis=1).astype(BF16)


def _pack_win(w_in):
    rows, cols = w_in.shape
    tr = PACK_ROWS
    return pl.pallas_call(
        _pack_win_kernel,
        grid=(rows // tr,),
        in_specs=[pl.BlockSpec((tr, cols), lambda i: (i, 0))],
        out_specs=pl.BlockSpec((tr, _C_END), lambda i: (i, 0)),
        out_shape=jax.ShapeDtypeStruct((rows, _C_END), BF16),
        compiler_params=pltpu.CompilerParams(dimension_semantics=("parallel",)),
        name="packwin",
    )(w_in)


def _pack_weights(w_in, w_q_b, w_kv_b):
    win = _pack_win(w_in)

    wq = w_q_b.reshape(MLA_Q_RANK, MLA_HEADS, MLA_NOPE + MLA_ROPE)
    qn_, q1, q2 = wq[..., :MLA_NOPE], wq[..., MLA_NOPE:MLA_NOPE + HALF], wq[..., MLA_NOPE + HALF:]
    zq = lambda n: jnp.zeros((MLA_Q_RANK, MLA_HEADS, n), wq.dtype)
    pad = LANES - MLA_NOPE - MLA_ROPE
    wqm = jnp.transpose(jnp.concatenate([q1, q2, qn_, zq(pad)], axis=-1), (1, 2, 0)).astype(BF16)

    wkv = w_kv_b.reshape(MLA_KV_RANK, MLA_HEADS, MLA_NOPE + MLA_V)
    zk = lambda n: jnp.zeros((MLA_KV_RANK, MLA_HEADS, n), wkv.dtype)
    wk = jnp.concatenate([zk(MLA_ROPE), wkv[..., :MLA_NOPE], zk(pad)],
                         axis=-1).reshape(MLA_KV_RANK, -1).astype(BF16)
    wvt = jnp.transpose(wkv[..., MLA_NOPE:], (1, 2, 0)).reshape(MLA_HEADS * MLA_V, MLA_KV_RANK).astype(BF16)
    return win, wqm, wk, wvt


def _rope_table():
    inv = ROPE_THETA ** (-jnp.arange(HALF, dtype=F32) / HALF)
    return inv.reshape(HALF, 1)


def kernel(x, positions, ln_g, w_in, q_a_norm_g, w_q_b, kv_a_norm_g, w_kv_b,
           hg_lower_bounds, hg_norm_g, w_out, final_norm_g):
    B, S, _ = x.shape
    assert ln_g.shape[0] == 1, "single-layer stack"
    win, wqm, wk, wvt = _pack_weights(w_in[0], w_q_b[0], w_kv_b[0])
    pos3 = positions.reshape(B, 1, S)
    q, k, vt, gm, hq, hf, hv, gh = _proj_call(
        x, pos3, ln_g[0:1], win, q_a_norm_g[0:1], wqm, kv_a_norm_g[0:1], wk, wvt,
        hg_lower_bounds, _rope_table())
    ya = _attn_call(q, k, vt, gm)
    yh = _hgrn_call(hf, hq, hv, gh, hg_norm_g[0:1])
    wo = w_out[0].astype(BF16)
    return _out_call(ya, yh, x, wo[:MLA_WIDTH], wo[MLA_WIDTH:], final_norm_g.reshape(1, D_MODEL))
```

```python
import functools
import math

import numpy as np
import jax
import jax.numpy as jnp
from jax import lax
from jax.experimental import pallas as pl
from jax.experimental.pallas import tpu as pltpu

F32 = jnp.float32
BF16 = jnp.bfloat16

D_MODEL = 1024
MLA_HEADS = 8
MLA_NOPE = 64
MLA_ROPE = 32
MLA_V = 64
MLA_Q_RANK = 256
MLA_KV_RANK = 128
MLA_WIDTH = MLA_HEADS * MLA_V
HG_HEADS = 4
HG_EXPAND = 128
HG_HEAD_V = 128
HG_WIDTH = HG_HEADS * HG_HEAD_V
HG_FDIM = HG_HEADS * HG_EXPAND
ROPE_THETA = 10000.0
EPS = 1e-6
HALF = MLA_ROPE // 2

LANES = 128
HEAD_SLAB = LANES
NEG_BIG = -1e30
LOG2E = math.log2(math.e)
VT_ROWS = 80

PROJ_TM = 1024
ATT_TQ = 1024
ATT_TK = 256
ATT_NH = 8
ATT_AHEAD = 8
ATT_UNROLL = 4
ATT_CW = 256
HG_CHUNK = 64
HG_SUB = 8
HG_LEVELS = 4
HG_TB = 512
PACK_ROWS = 128
OUT_TM = 2048
VMEM_LIMIT = 56 * 1024 * 1024

_C_QLAT = 0
_C_KVLAT = _C_QLAT + MLA_Q_RANK
_C_KRM = _C_KVLAT + MLA_KV_RANK
_C_GM = _C_KRM + LANES
_C_HQ = _C_GM + MLA_WIDTH
_C_HF = _C_HQ + HG_FDIM
_C_HI = _C_HF + HG_FDIM
_C_GH = _C_HI + HG_WIDTH
_C_END = _C_GH + HG_WIDTH

_NT = (((1,), (1,)), ((), ()))
_TN = (((0,), (0,)), ((), ()))


def _rms(x, g):
    return x * lax.rsqrt(jnp.mean(x * x, axis=-1, keepdims=True) + EPS) * g


def _silu(x):
    return x * jax.nn.sigmoid(x)


def _swap_rope_halves(slab):
    lane = lax.broadcasted_iota(jnp.int32, slab.shape, 1)
    return jnp.where(lane < HALF, pltpu.roll(slab, LANES - HALF, 1), pltpu.roll(slab, HALF, 1))


def _proj_kernel(x_ref, pos_ref, lng_ref, win_ref, qg_ref, wqt_ref,
                 kvg_ref, wk_ref, wvt_ref, lbraw_ref, rope_ref,
                 q_out, k_out, vt_out, gm_out, hq_out, hf_out, hv_out, gh_out):
    x = x_ref[0]
    tm = x.shape[0]
    h = _rms(x, lng_ref[...]).astype(BF16)

    def in_proj(c0, width):
        return jnp.dot(h, win_ref[:, c0:c0 + width], preferred_element_type=F32)

    lat = in_proj(0, _C_GM)

    ang_t = rope_ref[...] * pos_ref[0].astype(F32)
    cos_t = jnp.cos(ang_t)
    sin_t = jnp.sin(ang_t)
    cos_h = cos_t.T
    sin_h = sin_t.T
    pad = LANES - MLA_ROPE
    cos = jnp.concatenate([cos_h, cos_h, jnp.ones((tm, pad), F32)], axis=1)
    sin = jnp.concatenate([-sin_h, sin_h, jnp.zeros((tm, pad), F32)], axis=1)

    scale = LOG2E / math.sqrt(MLA_NOPE + MLA_ROPE)
    ones_t = lambda n: jnp.ones((n, tm), F32)
    zeros_t = lambda n: jnp.zeros((n, tm), F32)
    cos_q = [jnp.concatenate([ones_t(MLA_NOPE), cos_t, cos_t, ones_t(pad - MLA_NOPE)], axis=0) * scale,
             jnp.concatenate([cos_t, cos_t, ones_t(pad)], axis=0) * scale]
    sin_q = [jnp.concatenate([zeros_t(MLA_NOPE), -sin_t, sin_t, zeros_t(pad - MLA_NOPE)], axis=0) * scale,
             jnp.concatenate([-sin_t, sin_t, zeros_t(pad)], axis=0) * scale]
    qn = _rms(lat[:, _C_QLAT:_C_QLAT + MLA_Q_RANK], qg_ref[...]).astype(BF16)
    for hd in range(MLA_HEADS):
        qt = lax.dot_general(wqt_ref[hd], qn, _NT, preferred_element_type=F32)
        r0 = MLA_NOPE if hd % 2 == 0 else 0
        pieces = [qt[:r0], qt[r0 + HALF:r0 + MLA_ROPE], qt[r0:r0 + HALF], qt[r0 + MLA_ROPE:]]
        qt_swap = jnp.concatenate([p for p in pieces if p.shape[0]], axis=0)
        q_out[0, hd] = (qt * cos_q[hd % 2] + qt_swap * sin_q[hd % 2]).astype(BF16)
    kvn = _rms(lat[:, _C_KVLAT:_C_KVLAT + MLA_KV_RANK], kvg_ref[...]).astype(BF16)
    kn = jnp.dot(kvn, wk_ref[...], preferred_element_type=F32)
    kr = lat[:, _C_KRM:_C_KRM + LANES]
    kr_odd = kr * cos + _swap_rope_halves(kr) * sin
    kr_even = pltpu.roll(kr_odd, MLA_NOPE, 1)
    lane = lax.broadcasted_iota(jnp.int32, (tm, LANES), 1)
    vt_all = lax.dot_general(wvt_ref[...], kvn, _NT, preferred_element_type=F32).astype(BF16)
    tail = (lax.broadcasted_iota(jnp.int32, (VT_ROWS - MLA_V, tm), 0) == 0).astype(BF16)
    for hd in range(MLA_HEADS):
        col = kn[:, (hd // 2) * LANES:(hd // 2 + 1) * LANES]
        k_slab = jnp.where(lane < MLA_NOPE, col, kr_even) if hd % 2 == 0 else jnp.where(lane >= MLA_NOPE, col, kr_odd)
        k_out[0, hd] = k_slab.astype(BF16)
        vt_out[0, hd] = jnp.concatenate([vt_all[hd * MLA_V:(hd + 1) * MLA_V], tail], axis=0)
    gm_out[0] = _silu(in_proj(_C_GM, MLA_WIDTH))

    a0 = lbraw_ref[0:1, :]
    a1 = lbraw_ref[1:2, :]
    mx = jnp.maximum(a0, a1)
    e0 = jnp.exp(a0 - mx)
    e1 = jnp.exp(a1 - mx)
    lb = e0 / (e0 + e1)
    def put_heads(out, val):
        for hd in range(HG_HEADS):
            out[0, hd] = val[:, hd * LANES:(hd + 1) * LANES]

    put_heads(hq_out, _silu(in_proj(_C_HQ, HG_FDIM)))
    put_heads(hf_out, lb + (1.0 - lb) * jax.nn.sigmoid(in_proj(_C_HF, HG_FDIM)))
    put_heads(hv_out, in_proj(_C_HI, HG_WIDTH).astype(BF16))
    put_heads(gh_out, _silu(in_proj(_C_GH, HG_WIDTH)))


def _proj_call(x, pos3, lng, win, qg, wqm, kvg, wk, wvt, lbraw, rope_tab):
    B, S, _ = x.shape
    tm = PROJ_TM
    grid = (B, S // tm)
    tok = lambda w: pl.BlockSpec((1, tm, w), lambda b, i: (b, i, 0))
    full = lambda a: pl.BlockSpec(a.shape, lambda b, i: (0,) * a.ndim, pipeline_mode=pl.Buffered(1))
    head = pl.BlockSpec((1, MLA_HEADS, tm, HEAD_SLAB), lambda b, i: (b, 0, i, 0))
    out_shape = (
        jax.ShapeDtypeStruct((B, MLA_HEADS, HEAD_SLAB, S), BF16),
        jax.ShapeDtypeStruct((B, MLA_HEADS, S, HEAD_SLAB), BF16),
        jax.ShapeDtypeStruct((B, MLA_HEADS, VT_ROWS, S), BF16),
        jax.ShapeDtypeStruct((B, S, MLA_WIDTH), F32),
        jax.ShapeDtypeStruct((B, HG_HEADS, S, LANES), F32),
        jax.ShapeDtypeStruct((B, HG_HEADS, S, LANES), F32),
        jax.ShapeDtypeStruct((B, HG_HEADS, S, LANES), BF16),
        jax.ShapeDtypeStruct((B, HG_HEADS, S, LANES), F32),
    )
    hg = pl.BlockSpec((1, HG_HEADS, tm, LANES), lambda b, i: (b, 0, i, 0))
    return pl.pallas_call(
        _proj_kernel,
        grid=grid,
        in_specs=[tok(D_MODEL), pl.BlockSpec((1, 1, tm), lambda b, i: (b, 0, i)), full(lng), full(win),
                  full(qg), full(wqm), full(kvg), full(wk), full(wvt), full(lbraw), full(rope_tab)],
        out_specs=(pl.BlockSpec((1, MLA_HEADS, HEAD_SLAB, tm), lambda b, i: (b, 0, 0, i)), head,
                   pl.BlockSpec((1, MLA_HEADS, VT_ROWS, tm), lambda b, i: (b, 0, 0, i)),
                   tok(MLA_WIDTH), hg, hg, hg, hg),
        out_shape=out_shape,
        compiler_params=pltpu.CompilerParams(
            dimension_semantics=("parallel", "parallel"), vmem_limit_bytes=VMEM_LIMIT),
        name="proj",
    )(x, pos3, lng, win, qg, wqm, kvg, wk, wvt, lbraw, rope_tab)


def _attn_kernel(q_ref, k_ref, vt_ref, g_ref, o_ref, s_scr, m_scr, acc_scr):
    qi = pl.program_id(2)
    tq, tk, nh, cw = ATT_TQ, ATT_TK, ATT_NH, ATT_CW
    m_scr[...] = jnp.full(m_scr.shape, NEG_BIG, F32)
    acc_scr[...] = jnp.zeros(acc_scr.shape, F32)

    def scores(h, j, r0, diag):
        k = k_ref[0, h, pl.ds(r0, tk), :]
        q = q_ref[0, h, :, j * cw:(j + 1) * cw]
        s = jnp.dot(k, q, preferred_element_type=F32)
        if diag:
            key = lax.broadcasted_iota(jnp.int32, (tk, cw), 0)
            qry = lax.broadcasted_iota(jnp.int32, (tk, cw), 1)
            s = jnp.where(key <= qry, s, NEG_BIG)
        s_scr[h, j] = s

    def softmax_pv(h, j, r0):
        c = j * cw
        vt = vt_ref[0, h, :, pl.ds(r0, tk)]
        m_old = m_scr[h, :, c:c + cw]
        m_new = jnp.maximum(m_old, jnp.max(s_scr[h, j], axis=0, keepdims=True))
        p = jnp.exp2(s_scr[h, j] - m_new).astype(BF16)
        m_scr[h, :, c:c + cw] = m_new
        pv = jnp.dot(vt, p, preferred_element_type=F32)
        acc_scr[h, j] = jnp.exp2(m_old - m_new) * acc_scr[h, j] + pv

    def step(r0, c0, masked):
        units = [(h, j) for h in range(nh) for j in range(c0 // cw, tq // cw)]
        for u, (h, j) in enumerate(units[:ATT_AHEAD]):
            scores(h, j, r0, masked and j == c0 // cw)
        for u, (h, j) in enumerate(units):
            if u + ATT_AHEAD < len(units):
                h2, j2 = units[u + ATT_AHEAD]
                scores(h2, j2, r0, masked and j2 == c0 // cw)
            softmax_pv(h, j, r0)

    def full_steps(kb, carry):
        for i in range(ATT_UNROLL):
            step(pl.multiple_of((kb * ATT_UNROLL + i) * tk, tk), 0, False)
        return carry

    lax.fori_loop(0, qi * (tq // tk // ATT_UNROLL), full_steps, 0)
    for j in range(tq // tk):
        step(pl.multiple_of(qi * tq + j * tk, tk), j * tk, True)

    def normalized(h):
        acc = jnp.concatenate([acc_scr[h, j] for j in range(tq // cw)], axis=1)
        return acc[:MLA_V] / acc[MLA_V:MLA_V + 1]

    for h in range(0, nh, 2):
        pair = jnp.concatenate([normalized(h), normalized(h + 1)], axis=0).T
        cols = slice(h * MLA_V, (h + 2) * MLA_V)
        o_ref[0, :, cols] = (pair * g_ref[0, :, cols]).astype(BF16)


def _attn_call(qt, k, vt, gm):
    B, H, S, _ = k.shape
    tq, nh = ATT_TQ, ATT_NH
    assert ATT_CW == ATT_TK and (tq // ATT_TK) % ATT_UNROLL == 0
    grid = (B, H // nh, S // tq)
    return pl.pallas_call(
        _attn_kernel,
        grid=grid,
        in_specs=[
            pl.BlockSpec((1, nh, HEAD_SLAB, tq), lambda b, j, i: (b, j, 0, i)),
            pl.BlockSpec((1, nh, S, HEAD_SLAB), lambda b, j, i: (b, j, 0, 0),
                         pipeline_mode=pl.Buffered(1)),
            pl.BlockSpec((1, nh, VT_ROWS, S), lambda b, j, i: (b, j, 0, 0),
                         pipeline_mode=pl.Buffered(1)),
            pl.BlockSpec((1, tq, nh * MLA_V), lambda b, j, i: (b, i, j)),
        ],
        out_specs=pl.BlockSpec((1, tq, nh * MLA_V), lambda b, j, i: (b, i, j)),
        out_shape=jax.ShapeDtypeStruct((B, S, MLA_WIDTH), BF16),
        scratch_shapes=[
            pltpu.VMEM((nh, tq // ATT_CW, ATT_TK, ATT_CW), F32),
            pltpu.VMEM((nh, 1, tq), F32),
            pltpu.VMEM((nh, tq // ATT_CW, VT_ROWS, ATT_CW), F32),
        ],
        compiler_params=pltpu.CompilerParams(
            dimension_semantics=("parallel", "parallel", "arbitrary"),
            vmem_limit_bytes=VMEM_LIMIT),
        name="attn",
    )(qt, k, vt, gm)


def _hgrn_constants():
    C, SUB, NL = HG_CHUNK, HG_SUB, HG_LEVELS
    t = np.arange(C)
    lm = np.zeros((2 * NL, C, C), np.float32)
    masks = np.zeros((NL, C, C), np.float32)
    masks[0] = (t[:, None] // SUB) == (t[None, :] // SUB)
    for l in range(NL):
        bs = SUB << l
        same = (t[:, None] // bs) == (t[None, :] // bs)
        lm[l] = same & (t[None, :] <= t[:, None])
        lm[NL + l] = same & (t[None, :] > t[:, None])
        if l < NL - 1:
            masks[l + 1] = ((t[:, None] // bs) == (t[None, :] // bs) + 1) & ((t[None, :] // bs) % 2 == 0)
    lmat = lm.reshape(2 * NL * C, C)
    lmat = np.concatenate([lmat, lmat], axis=1)
    j = np.arange(C)
    sel = np.zeros((SUB, LANES, C), np.float32)
    sel[:] = (j[None, None, :] % SUB) == np.arange(SUB)[:, None, None]
    return (jnp.asarray(lmat, BF16), jnp.asarray(sel.reshape(SUB * LANES, C), BF16),
            jnp.asarray(masks, F32))


def _hgrn_kernel(f_ref, q_ref, v_ref, g_ref, ng_ref, lmat_ref, sel_ref, mask_ref, o_ref,
                 st_scr, qs_scr, ks_scr, c8_scr, k3_scr, p2_scr, a_scr, ds_scr, oi_scr, dec_scr):
    C, SUB, NL, TB, NH = HG_CHUNK, HG_SUB, HG_LEVELS, HG_TB, HG_HEADS
    NC = TB // C

    @pl.when(pl.program_id(1) == 0)
    def _():
        st_scr[...] = jnp.zeros_like(st_scr)

    lmat = lmat_ref[...]
    nb = C // SUB
    heads = range(NH)

    for c in range(NC):
        rows = slice(c * C, (c + 1) * C)
        blks = slice(c * nb, (c + 1) * nb)
        f = [f_ref[0, h, rows, :] for h in heads]
        g = jnp.log2(jnp.concatenate(f, axis=1))
        g1 = g.astype(BF16)
        g2 = (g - g1.astype(F32)).astype(BF16)
        e = jnp.dot(lmat, jnp.concatenate([g1, g2], axis=0), preferred_element_type=F32)
        for h in heads:
            hs = slice(h * LANES, (h + 1) * LANES)
            q = q_ref[0, h, rows, :]
            k = 1.0 - f[h]
            k3_scr[h, blks] = k.reshape(nb, SUB, LANES)
            for l in range(NL):
                cq = e[l * C:(l + 1) * C, hs]
                ck = e[(NL + l) * C:(NL + l + 1) * C, hs]
                qs_scr[l, h, rows, :] = (q * jnp.exp2(cq)).astype(BF16)
                ks_scr[l, h, rows, :] = (k * jnp.exp2(ck)).astype(BF16)
                if l == 0:
                    c8_scr[h, blks] = cq.reshape(nb, SUB, LANES)
                if l == NL - 1:
                    dec_scr[c:c + 1, hs] = jnp.exp2(cq[C - 1:C, :])

    tt = lax.broadcasted_iota(jnp.int32, (1, SUB, LANES), 1)
    for h in heads:
        q3 = q_ref[0, h].reshape(TB // SUB, SUB, LANES)
        c8 = c8_scr[h]
        for s in range(SUB):
            bcast = pl.ds(s, SUB, stride=0)
            dg = jnp.where(tt >= s, c8 - c8_scr[h, :, bcast, :], NEG_BIG)
            p = q3 * jnp.exp2(dg) * k3_scr[h, :, bcast, :]
            p2_scr[h, s] = p.reshape(TB, LANES).astype(BF16)

    sel_masks = [mask_ref[l] > 0.5 for l in range(NL)]
    for h in heads:
        p2 = jnp.concatenate([p2_scr[h, s] for s in range(SUB)], axis=1)
        ad = jnp.dot(p2, sel_ref[...], preferred_element_type=F32)
        for c in range(NC):
            rows = slice(c * C, (c + 1) * C)
            a = jnp.where(sel_masks[0], ad[rows], 0.0)
            for l in range(NL - 1):
                al = lax.dot_general(qs_scr[l, h, rows, :], ks_scr[l, h, rows, :], _NT,
                                     preferred_element_type=F32)
                a = jnp.where(sel_masks[l + 1], al, a)
            a_scr[h, rows, :] = a.astype(BF16)
    for h in heads:
        for c in range(NC):
            rows = slice(c * C, (c + 1) * C)
            v = v_ref[0, h, rows, :]
            oi_scr[h, rows, :] = jnp.dot(a_scr[h, rows, :], v, preferred_element_type=F32)
            ds_scr[h * NC + c] = lax.dot_general(v, ks_scr[NL - 1, h, rows, :], _TN,
                                                 preferred_element_type=F32)

    for c in range(NC):
        rows = slice(c * C, (c + 1) * C)
        for h in heads:
            hs = slice(h * LANES, (h + 1) * LANES)
            st = st_scr[h]
            o = oi_scr[h, rows, :] + lax.dot_general(qs_scr[NL - 1, h, rows, :], st.astype(BF16), _NT,
                                                     preferred_element_type=F32)
            st_scr[h] = st * dec_scr[c:c + 1, hs] + ds_scr[h * NC + c]
            o = o * lax.rsqrt(jnp.mean(o * o, axis=-1, keepdims=True) + EPS) * ng_ref[:, hs]
            o_ref[0, h, rows, :] = (o * g_ref[0, h, rows, :]).astype(BF16)


def _hgrn_call(hf, hq, hv, gh, ng):
    B, NH, S, _ = hf.shape
    tb, C, SUB, NL = HG_TB, HG_CHUNK, HG_SUB, HG_LEVELS
    assert SUB << (NL - 1) == C and NH == HG_HEADS
    lmat, sel, masks = _hgrn_constants()
    grid = (B, S // tb)
    blk = pl.BlockSpec((1, NH, tb, LANES), lambda b, t: (b, 0, t, 0))
    full = lambda a: pl.BlockSpec(a.shape, lambda b, t: (0,) * a.ndim, pipeline_mode=pl.Buffered(1))
    return pl.pallas_call(
        _hgrn_kernel,
        grid=grid,
        in_specs=[blk, blk, blk, blk, full(ng), full(lmat), full(sel), full(masks)],
        out_specs=blk,
        out_shape=jax.ShapeDtypeStruct((B, NH, S, LANES), BF16),
        scratch_shapes=[
            pltpu.VMEM((NH, HG_HEAD_V, HG_EXPAND), F32),
            pltpu.VMEM((NL, NH, tb, LANES), BF16),
            pltpu.VMEM((NL, NH, tb, LANES), BF16),
            pltpu.VMEM((NH, tb // SUB, SUB, LANES), F32),
            pltpu.VMEM((NH, tb // SUB, SUB, LANES), F32),
            pltpu.VMEM((NH, SUB, tb, LANES), BF16),
            pltpu.VMEM((NH, tb, C), BF16),
            pltpu.VMEM((NH * (tb // C), HG_HEAD_V, HG_EXPAND), F32),
            pltpu.VMEM((NH, tb, LANES), F32),
            pltpu.VMEM((tb // C, NH * LANES), F32),
        ],
        compiler_params=pltpu.CompilerParams(
            dimension_semantics=("parallel", "arbitrary"),
            vmem_limit_bytes=VMEM_LIMIT),
        name="hgrn",
    )(hf, hq, hv, gh, ng, lmat, sel, masks)


def _out_kernel(ya_ref, yh_ref, x_ref, wa_ref, wh_ref, fg_ref, o_ref):
    y = jnp.dot(ya_ref[0], wa_ref[...], preferred_element_type=F32)
    yh = jnp.concatenate([yh_ref[0, h] for h in range(HG_HEADS)], axis=1)
    y = y + jnp.dot(yh, wh_ref[...], preferred_element_type=F32)
    o_ref[0] = _rms(x_ref[0] + y, fg_ref[...])


def _out_call(ya, yh, x, wa, wh, fg):
    B, S, _ = x.shape
    tm = OUT_TM
    grid = (B, S // tm)
    tok = lambda w: pl.BlockSpec((1, tm, w), lambda b, i: (b, i, 0))
    full = lambda a: pl.BlockSpec(a.shape, lambda b, i: (0,) * a.ndim, pipeline_mode=pl.Buffered(1))
    return pl.pallas_call(
        _out_kernel,
        grid=grid,
        in_specs=[tok(MLA_WIDTH), pl.BlockSpec((1, HG_HEADS, tm, LANES), lambda b, i: (b, 0, i, 0)),
                  tok(D_MODEL), full(wa), full(wh), full(fg)],
        out_specs=tok(D_MODEL),
        out_shape=jax.ShapeDtypeStruct((B, S, D_MODEL), F32),
        compiler_params=pltpu.CompilerParams(
            dimension_semantics=("parallel", "parallel"), vmem_limit_bytes=VMEM_LIMIT),
        name="outproj",
    )(ya, yh, x, wa, wh, fg)


def _pack_win_kernel(w_ref, o_ref):
    w = w_ref[...]
    split = _C_KRM + MLA_ROPE
    zeros = jnp.zeros((w.shape[0], LANES - MLA_ROPE), w.dtype)
    o_ref[...] = jnp.concatenate([w[:, :split], zeros, w[:, split:]], axis=1).astype(BF16)


def _pack_win(w_in):
    rows, cols = w_in.shape
    tr = PACK_ROWS
    return pl.pallas_call(
        _pack_win_kernel,
        grid=(rows // tr,),
        in_specs=[pl.BlockSpec((tr, cols), lambda i: (i, 0))],
        out_specs=pl.BlockSpec((tr, _C_END), lambda i: (i, 0)),
        out_shape=jax.ShapeDtypeStruct((rows, _C_END), BF16),
        compiler_params=pltpu.CompilerParams(dimension_semantics=("parallel",)),
        name="packwin",
    )(w_in)


def _pack_weights(w_in, w_q_b, w_kv_b):
    win = _pack_win(w_in)

    wq = w_q_b.reshape(MLA_Q_RANK, MLA_HEADS, MLA_NOPE + MLA_ROPE)
    qn_, q1, q2 = wq[..., :MLA_NOPE], wq[..., MLA_NOPE:MLA_NOPE + HALF], wq[..., MLA_NOPE + HALF:]
    zq = lambda n: jnp.zeros((MLA_Q_RANK, MLA_HEADS, n), wq.dtype)
    pad = LANES - MLA_NOPE - MLA_ROPE
    w_even = jnp.concatenate([qn_, q1, q2, zq(pad)], axis=-1)
    w_odd = jnp.concatenate([q1, q2, zq(pad), qn_], axis=-1)
    parity = (jnp.arange(MLA_HEADS) % 2 == 0)[None, :, None]
    wqm = jnp.transpose(jnp.where(parity, w_even, w_odd), (1, 2, 0)).astype(BF16)

    wkv = w_kv_b.reshape(MLA_KV_RANK, MLA_HEADS, MLA_NOPE + MLA_V)
    wk = wkv[..., :MLA_NOPE].reshape(MLA_KV_RANK, MLA_HEADS * MLA_NOPE).astype(BF16)
    wvt = jnp.transpose(wkv[..., MLA_NOPE:], (1, 2, 0)).reshape(MLA_HEADS * MLA_V, MLA_KV_RANK).astype(BF16)
    return win, wqm, wk, wvt


def _rope_table():
    inv = ROPE_THETA ** (-jnp.arange(HALF, dtype=F32) / HALF)
    return inv.reshape(HALF, 1)


def kernel(x, positions, ln_g, w_in, q_a_norm_g, w_q_b, kv_a_norm_g, w_kv_b,
           hg_lower_bounds, hg_norm_g, w_out, final_norm_g):
    B, S, _ = x.shape
    assert ln_g.shape[0] == 1, "single-layer stack"
    win, wqm, wk, wvt = _pack_weights(w_in[0], w_q_b[0], w_kv_b[0])
    pos3 = positions.reshape(B, 1, S)
    q, k, vt, gm, hq, hf, hv, gh = _proj_call(
        x, pos3, ln_g[0:1], win, q_a_norm_g[0:1], wqm, kv_a_norm_g[0:1], wk, wvt,
        hg_lower_bounds, _rope_table())
    ya = _attn_call(q, k, vt, gm)
    yh = _hgrn_call(hf, hq, hv, gh, hg_norm_g[0:1])
    wo = w_out[0].astype(BF16)
    return _out_call(ya, yh, x, wo[:MLA_WIDTH], wo[MLA_WIDTH:], final_norm_g.reshape(1, D_MODEL))
```

```python
import functools
import math

import numpy as np
import jax
import jax.numpy as jnp
from jax import lax
from jax.experimental import pallas as pl
from jax.experimental.pallas import tpu as pltpu

F32 = jnp.float32
BF16 = jnp.bfloat16

D_MODEL = 1024
MLA_HEADS = 8
MLA_NOPE = 64
MLA_ROPE = 32
MLA_V = 64
MLA_Q_RANK = 256
MLA_KV_RANK = 128
MLA_WIDTH = MLA_HEADS * MLA_V
HG_HEADS = 4
HG_EXPAND = 128
HG_HEAD_V = 128
HG_WIDTH = HG_HEADS * HG_HEAD_V
HG_FDIM = HG_HEADS * HG_EXPAND
ROPE_THETA = 10000.0
EPS = 1e-6
HALF = MLA_ROPE // 2

LANES = 128
HEAD_SLAB = LANES
NEG_BIG = -1e30
LOG2E = math.log2(math.e)
VT_ROWS = 80

PROJ_TM = 1024
ATT_TQ = 1024
ATT_TK = 256
ATT_NH = 8
ATT_AHEAD = 8
ATT_UNROLL = 4
ATT_CW = 256
HG_CHUNK = 64
HG_SUB = 8
HG_LEVELS = 4
HG_TB = 512
PACK_ROWS = 128
OUT_TM = 2048
OUT_SUB = 512
VMEM_LIMIT = 56 * 1024 * 1024

_C_QLAT = 0
_C_KVLAT = _C_QLAT + MLA_Q_RANK
_C_KRM = _C_KVLAT + MLA_KV_RANK
_C_GM = _C_KRM + LANES
_C_HQ = _C_GM + MLA_WIDTH
_C_HF = _C_HQ + HG_FDIM
_C_HI = _C_HF + HG_FDIM
_C_GH = _C_HI + HG_WIDTH
_C_END = _C_GH + HG_WIDTH

_NT = (((1,), (1,)), ((), ()))
_TN = (((0,), (0,)), ((), ()))


def _rms(x, g):
    return x * lax.rsqrt(jnp.mean(x * x, axis=-1, keepdims=True) + EPS) * g


def _silu(x):
    return x * jax.nn.sigmoid(x)


def _swap_rope_halves(slab):
    lane = lax.broadcasted_iota(jnp.int32, slab.shape, 1)
    return jnp.where(lane < HALF, pltpu.roll(slab, LANES - HALF, 1), pltpu.roll(slab, HALF, 1))


def _proj_kernel(x_ref, pos_ref, lng_ref, win_ref, qg_ref, wqt_ref,
                 kvg_ref, wk_ref, wvt_ref, lbraw_ref, rope_ref,
                 q_out, k_out, vt_out, gm_out, hq_out, hf_out, hv_out, gh_out):
    x = x_ref[0]
    tm = x.shape[0]
    h = _rms(x, lng_ref[...]).astype(BF16)

    def in_proj(c0, width):
        return jnp.dot(h, win_ref[:, c0:c0 + width], preferred_element_type=F32)

    lat = in_proj(0, _C_GM)

    ang_t = rope_ref[...] * pos_ref[0].astype(F32)
    cos_t = jnp.cos(ang_t)
    sin_t = jnp.sin(ang_t)
    cos_h = cos_t.T
    sin_h = sin_t.T
    pad = LANES - MLA_ROPE
    cos = jnp.concatenate([cos_h, cos_h, jnp.ones((tm, pad), F32)], axis=1)
    sin = jnp.concatenate([-sin_h, sin_h, jnp.zeros((tm, pad), F32)], axis=1)

    scale = LOG2E / math.sqrt(MLA_NOPE + MLA_ROPE)
    cos_q = jnp.concatenate([cos_t, cos_t], axis=0) * scale
    sin_q = jnp.concatenate([-sin_t, sin_t], axis=0) * scale
    zero_rows = jnp.zeros((LANES - MLA_NOPE - MLA_ROPE, tm), BF16)
    qn = _rms(lat[:, _C_QLAT:_C_QLAT + MLA_Q_RANK], qg_ref[...]).astype(BF16)
    for hd in range(MLA_HEADS):
        qt = lax.dot_general(wqt_ref[hd], qn, _NT, preferred_element_type=F32)
        nope = (qt[:MLA_NOPE] * scale).astype(BF16)
        x12 = qt[MLA_NOPE:]
        x21 = jnp.concatenate([x12[HALF:], x12[:HALF]], axis=0)
        rot = (x12 * cos_q + x21 * sin_q).astype(BF16)
        slab = [nope, rot, zero_rows] if hd % 2 == 0 else [rot, zero_rows, nope]
        q_out[0, hd] = jnp.concatenate(slab, axis=0)
    kvn = _rms(lat[:, _C_KVLAT:_C_KVLAT + MLA_KV_RANK], kvg_ref[...]).astype(BF16)
    kn = jnp.dot(kvn, wk_ref[...], preferred_element_type=F32)
    kr = lat[:, _C_KRM:_C_KRM + LANES]
    kr_odd = kr * cos + _swap_rope_halves(kr) * sin
    kr_even = pltpu.roll(kr_odd, MLA_NOPE, 1)
    lane = lax.broadcasted_iota(jnp.int32, (tm, LANES), 1)
    vt_all = lax.dot_general(wvt_ref[...], kvn, _NT, preferred_element_type=F32).astype(BF16)
    tail = (lax.broadcasted_iota(jnp.int32, (VT_ROWS - MLA_V, tm), 0) == 0).astype(BF16)
    for hd in range(MLA_HEADS):
        col = kn[:, (hd // 2) * LANES:(hd // 2 + 1) * LANES]
        k_slab = jnp.where(lane < MLA_NOPE, col, kr_even) if hd % 2 == 0 else jnp.where(lane >= MLA_NOPE, col, kr_odd)
        k_out[0, hd] = k_slab.astype(BF16)
        vt_out[0, hd] = jnp.concatenate([vt_all[hd * MLA_V:(hd + 1) * MLA_V], tail], axis=0)
    gm_out[0] = _silu(in_proj(_C_GM, MLA_WIDTH))

    a0 = lbraw_ref[0:1, :]
    a1 = lbraw_ref[1:2, :]
    mx = jnp.maximum(a0, a1)
    e0 = jnp.exp(a0 - mx)
    e1 = jnp.exp(a1 - mx)
    lb = e0 / (e0 + e1)
    def put_heads(out, val):
        for hd in range(HG_HEADS):
            out[0, hd] = val[:, hd * LANES:(hd + 1) * LANES]

    put_heads(hq_out, _silu(in_proj(_C_HQ, HG_FDIM)))
    put_heads(hf_out, lb + (1.0 - lb) * jax.nn.sigmoid(in_proj(_C_HF, HG_FDIM)))
    put_heads(hv_out, in_proj(_C_HI, HG_WIDTH).astype(BF16))
    put_heads(gh_out, _silu(in_proj(_C_GH, HG_WIDTH)))


def _proj_call(x, pos3, lng, win, qg, wqm, kvg, wk, wvt, lbraw, rope_tab):
    B, S, _ = x.shape
    tm = PROJ_TM
    grid = (B, S // tm)
    tok = lambda w: pl.BlockSpec((1, tm, w), lambda b, i: (b, i, 0))
    full = lambda a: pl.BlockSpec(a.shape, lambda b, i: (0,) * a.ndim, pipeline_mode=pl.Buffered(1))
    head = pl.BlockSpec((1, MLA_HEADS, tm, HEAD_SLAB), lambda b, i: (b, 0, i, 0))
    out_shape = (
        jax.ShapeDtypeStruct((B, MLA_HEADS, HEAD_SLAB, S), BF16),
        jax.ShapeDtypeStruct((B, MLA_HEADS, S, HEAD_SLAB), BF16),
        jax.ShapeDtypeStruct((B, MLA_HEADS, VT_ROWS, S), BF16),
        jax.ShapeDtypeStruct((B, S, MLA_WIDTH), F32),
        jax.ShapeDtypeStruct((B, HG_HEADS, S, LANES), F32),
        jax.ShapeDtypeStruct((B, HG_HEADS, S, LANES), F32),
        jax.ShapeDtypeStruct((B, HG_HEADS, S, LANES), BF16),
        jax.ShapeDtypeStruct((B, HG_HEADS, S, LANES), F32),
    )
    hg = pl.BlockSpec((1, HG_HEADS, tm, LANES), lambda b, i: (b, 0, i, 0))
    return pl.pallas_call(
        _proj_kernel,
        grid=grid,
        in_specs=[tok(D_MODEL), pl.BlockSpec((1, 1, tm), lambda b, i: (b, 0, i)), full(lng), full(win),
                  full(qg), full(wqm), full(kvg), full(wk), full(wvt), full(lbraw), full(rope_tab)],
        out_specs=(pl.BlockSpec((1, MLA_HEADS, HEAD_SLAB, tm), lambda b, i: (b, 0, 0, i)), head,
                   pl.BlockSpec((1, MLA_HEADS, VT_ROWS, tm), lambda b, i: (b, 0, 0, i)),
                   tok(MLA_WIDTH), hg, hg, hg, hg),
        out_shape=out_shape,
        compiler_params=pltpu.CompilerParams(
            dimension_semantics=("parallel", "parallel"), vmem_limit_bytes=VMEM_LIMIT),
        name="proj",
    )(x, pos3, lng, win, qg, wqm, kvg, wk, wvt, lbraw, rope_tab)


def _attn_kernel(q_ref, k_ref, vt_ref, g_ref, o_ref, s_scr, m_scr, acc_scr):
    qi = pl.program_id(2)
    tq, tk, nh, cw = ATT_TQ, ATT_TK, ATT_NH, ATT_CW
    m_scr[...] = jnp.full(m_scr.shape, NEG_BIG, F32)
    acc_scr[...] = jnp.zeros(acc_scr.shape, F32)

    def scores(h, j, r0, diag):
        k = k_ref[0, h, pl.ds(r0, tk), :]
        q = q_ref[0, h, :, j * cw:(j + 1) * cw]
        s = jnp.dot(k, q, preferred_element_type=F32)
        if diag:
            key = lax.broadcasted_iota(jnp.int32, (tk, cw), 0)
            qry = lax.broadcasted_iota(jnp.int32, (tk, cw), 1)
            s = jnp.where(key <= qry, s, NEG_BIG)
        s_scr[h, j] = s

    def softmax_pv(h, j, r0):
        c = j * cw
        vt = vt_ref[0, h, :, pl.ds(r0, tk)]
        m_old = m_scr[h, :, c:c + cw]
        m_new = jnp.maximum(m_old, jnp.max(s_scr[h, j], axis=0, keepdims=True))
        p = jnp.exp2(s_scr[h, j] - m_new).astype(BF16)
        m_scr[h, :, c:c + cw] = m_new
        pv = jnp.dot(vt, p, preferred_element_type=F32)
        acc_scr[h, j] = jnp.exp2(m_old - m_new) * acc_scr[h, j] + pv

    def step(r0, c0, masked):
        units = [(h, j) for h in range(nh) for j in range(c0 // cw, tq // cw)]
        for u, (h, j) in enumerate(units[:ATT_AHEAD]):
            scores(h, j, r0, masked and j == c0 // cw)
        for u, (h, j) in enumerate(units):
            if u + ATT_AHEAD < len(units):
                h2, j2 = units[u + ATT_AHEAD]
                scores(h2, j2, r0, masked and j2 == c0 // cw)
            softmax_pv(h, j, r0)

    def full_steps(kb, carry):
        for i in range(ATT_UNROLL):
            step(pl.multiple_of((kb * ATT_UNROLL + i) * tk, tk), 0, False)
        return carry

    lax.fori_loop(0, qi * (tq // tk // ATT_UNROLL), full_steps, 0)
    for j in range(tq // tk):
        step(pl.multiple_of(qi * tq + j * tk, tk), j * tk, True)

    def normalized(h):
        acc = jnp.concatenate([acc_scr[h, j] for j in range(tq // cw)], axis=1)
        return acc[:MLA_V] / acc[MLA_V:MLA_V + 1]

    for h in range(0, nh, 2):
        pair = jnp.concatenate([normalized(h), normalized(h + 1)], axis=0).T
        cols = slice(h * MLA_V, (h + 2) * MLA_V)
        o_ref[0, :, cols] = (pair * g_ref[0, :, cols]).astype(BF16)


def _attn_call(qt, k, vt, gm):
    B, H, S, _ = k.shape
    tq, nh = ATT_TQ, ATT_NH
    assert ATT_CW == ATT_TK and (tq // ATT_TK) % ATT_UNROLL == 0
    grid = (B, H // nh, S // tq)
    return pl.pallas_call(
        _attn_kernel,
        grid=grid,
        in_specs=[
            pl.BlockSpec((1, nh, HEAD_SLAB, tq), lambda b, j, i: (b, j, 0, i)),
            pl.BlockSpec((1, nh, S, HEAD_SLAB), lambda b, j, i: (b, j, 0, 0),
                         pipeline_mode=pl.Buffered(1)),
            pl.BlockSpec((1, nh, VT_ROWS, S), lambda b, j, i: (b, j, 0, 0),
                         pipeline_mode=pl.Buffered(1)),
            pl.BlockSpec((1, tq, nh * MLA_V), lambda b, j, i: (b, i, j)),
        ],
        out_specs=pl.BlockSpec((1, tq, nh * MLA_V), lambda b, j, i: (b, i, j)),
        out_shape=jax.ShapeDtypeStruct((B, S, MLA_WIDTH), BF16),
        scratch_shapes=[
            pltpu.VMEM((nh, tq // ATT_CW, ATT_TK, ATT_CW), F32),
            pltpu.VMEM((nh, 1, tq), F32),
            pltpu.VMEM((nh, tq // ATT_CW, VT_ROWS, ATT_CW), F32),
        ],
        compiler_params=pltpu.CompilerParams(
            dimension_semantics=("parallel", "parallel", "arbitrary"),
            vmem_limit_bytes=VMEM_LIMIT),
        name="attn",
    )(qt, k, vt, gm)


def _hgrn_constants():
    C, SUB, NL = HG_CHUNK, HG_SUB, HG_LEVELS
    t = np.arange(C)
    lm = np.zeros((2 * NL, C, C), np.float32)
    masks = np.zeros((NL, C, C), np.float32)
    masks[0] = (t[:, None] // SUB) == (t[None, :] // SUB)
    for l in range(NL):
        bs = SUB << l
        same = (t[:, None] // bs) == (t[None, :] // bs)
        lm[l] = same & (t[None, :] <= t[:, None])
        lm[NL + l] = same & (t[None, :] > t[:, None])
        if l < NL - 1:
            masks[l + 1] = ((t[:, None] // bs) == (t[None, :] // bs) + 1) & ((t[None, :] // bs) % 2 == 0)
    lmat = lm.reshape(2 * NL * C, C)
    lmat = np.concatenate([lmat, lmat], axis=1)
    j = np.arange(C)
    sel = np.zeros((SUB, LANES, C), np.float32)
    sel[:] = (j[None, None, :] % SUB) == np.arange(SUB)[:, None, None]
    return (jnp.asarray(lmat, BF16), jnp.asarray(sel.reshape(SUB * LANES, C), BF16),
            jnp.asarray(masks, F32))


def _hgrn_kernel(f_ref, q_ref, v_ref, g_ref, ng_ref, lmat_ref, sel_ref, mask_ref, o_ref,
                 st_scr, qs_scr, ks_scr, c8_scr, k3_scr, p2_scr, a_scr, ds_scr, oi_scr, dec_scr):
    C, SUB, NL, TB, NH = HG_CHUNK, HG_SUB, HG_LEVELS, HG_TB, HG_HEADS
    NC = TB // C

    @pl.when(pl.program_id(1) == 0)
    def _():
        st_scr[...] = jnp.zeros_like(st_scr)

    lmat = lmat_ref[...]
    nb = C // SUB
    heads = range(NH)

    for c in range(NC):
        rows = slice(c * C, (c + 1) * C)
        blks = slice(c * nb, (c + 1) * nb)
        f = [f_ref[0, h, rows, :] for h in heads]
        g = jnp.log2(jnp.concatenate(f, axis=1))
        g1 = g.astype(BF16)
        g2 = (g - g1.astype(F32)).astype(BF16)
        e = jnp.dot(lmat, jnp.concatenate([g1, g2], axis=0), preferred_element_type=F32)
        for h in heads:
            hs = slice(h * LANES, (h + 1) * LANES)
            q = q_ref[0, h, rows, :]
            k = 1.0 - f[h]
            k3_scr[h, blks] = k.reshape(nb, SUB, LANES)
            for l in range(NL):
                cq = e[l * C:(l + 1) * C, hs]
                ck = e[(NL + l) * C:(NL + l + 1) * C, hs]
                qs_scr[l, h, rows, :] = (q * jnp.exp2(cq)).astype(BF16)
                ks_scr[l, h, rows, :] = (k * jnp.exp2(ck)).astype(BF16)
                if l == 0:
                    c8_scr[h, blks] = cq.reshape(nb, SUB, LANES)
                if l == NL - 1:
                    dec_scr[c:c + 1, hs] = jnp.exp2(cq[C - 1:C, :])

    tt = lax.broadcasted_iota(jnp.int32, (1, SUB, LANES), 1)
    for h in heads:
        q3 = q_ref[0, h].reshape(TB // SUB, SUB, LANES)
        c8 = c8_scr[h]
        for s in range(SUB):
            bcast = pl.ds(s, SUB, stride=0)
            dg = jnp.where(tt >= s, c8 - c8_scr[h, :, bcast, :], NEG_BIG)
            p = q3 * jnp.exp2(dg) * k3_scr[h, :, bcast, :]
            p2_scr[h, s] = p.reshape(TB, LANES).astype(BF16)

    sel_masks = [mask_ref[l] > 0.5 for l in range(NL)]
    for h in heads:
        p2 = jnp.concatenate([p2_scr[h, s] for s in range(SUB)], axis=1)
        ad = jnp.dot(p2, sel_ref[...], preferred_element_type=F32)
        for c in range(NC):
            rows = slice(c * C, (c + 1) * C)
            a = jnp.where(sel_masks[0], ad[rows], 0.0)
            for l in range(NL - 1):
                al = lax.dot_general(qs_scr[l, h, rows, :], ks_scr[l, h, rows, :], _NT,
                                     preferred_element_type=F32)
                a = jnp.where(sel_masks[l + 1], al, a)
            a_scr[h, rows, :] = a.astype(BF16)
    for h in heads:
        for c in range(NC):
            rows = slice(c * C, (c + 1) * C)
            v = v_ref[0, h, rows, :]
            oi_scr[h, rows, :] = jnp.dot(a_scr[h, rows, :], v, preferred_element_type=F32)
            ds_scr[h * NC + c] = lax.dot_general(v, ks_scr[NL - 1, h, rows, :], _TN,
                                                 preferred_element_type=F32)

    for c in range(NC):
        rows = slice(c * C, (c + 1) * C)
        for h in heads:
            hs = slice(h * LANES, (h + 1) * LANES)
            st = st_scr[h]
            o = oi_scr[h, rows, :] + lax.dot_general(qs_scr[NL - 1, h, rows, :], st.astype(BF16), _NT,
                                                     preferred_element_type=F32)
            st_scr[h] = st * dec_scr[c:c + 1, hs] + ds_scr[h * NC + c]
            o = o * lax.rsqrt(jnp.mean(o * o, axis=-1, keepdims=True) + EPS) * ng_ref[:, hs]
            o_ref[0, h, rows, :] = (o * g_ref[0, h, rows, :]).astype(BF16)


def _hgrn_call(hf, hq, hv, gh, ng):
    B, NH, S, _ = hf.shape
    tb, C, SUB, NL = HG_TB, HG_CHUNK, HG_SUB, HG_LEVELS
    assert SUB << (NL - 1) == C and NH == HG_HEADS
    lmat, sel, masks = _hgrn_constants()
    grid = (B, S // tb)
    blk = pl.BlockSpec((1, NH, tb, LANES), lambda b, t: (b, 0, t, 0))
    full = lambda a: pl.BlockSpec(a.shape, lambda b, t: (0,) * a.ndim, pipeline_mode=pl.Buffered(1))
    return pl.pallas_call(
        _hgrn_kernel,
        grid=grid,
        in_specs=[blk, blk, blk, blk, full(ng), full(lmat), full(sel), full(masks)],
        out_specs=blk,
        out_shape=jax.ShapeDtypeStruct((B, NH, S, LANES), BF16),
        scratch_shapes=[
            pltpu.VMEM((NH, HG_HEAD_V, HG_EXPAND), F32),
            pltpu.VMEM((NL, NH, tb, LANES), BF16),
            pltpu.VMEM((NL, NH, tb, LANES), BF16),
            pltpu.VMEM((NH, tb // SUB, SUB, LANES), F32),
            pltpu.VMEM((NH, tb // SUB, SUB, LANES), F32),
            pltpu.VMEM((NH, SUB, tb, LANES), BF16),
            pltpu.VMEM((NH, tb, C), BF16),
            pltpu.VMEM((NH * (tb // C), HG_HEAD_V, HG_EXPAND), F32),
            pltpu.VMEM((NH, tb, LANES), F32),
            pltpu.VMEM((tb // C, NH * LANES), F32),
        ],
        compiler_params=pltpu.CompilerParams(
            dimension_semantics=("parallel", "arbitrary"),
            vmem_limit_bytes=VMEM_LIMIT),
        name="hgrn",
    )(hf, hq, hv, gh, ng, lmat, sel, masks)


def _out_kernel(ya_ref, yh_ref, x_ref, wa_ref, wh_ref, fg_ref, o_ref):
    for r in range(0, o_ref.shape[1], OUT_SUB):
        rows = slice(r, r + OUT_SUB)
        y = jnp.dot(ya_ref[0, rows, :], wa_ref[...], preferred_element_type=F32)
        yh = jnp.concatenate([yh_ref[0, h, rows, :] for h in range(HG_HEADS)], axis=1)
        y = y + jnp.dot(yh, wh_ref[...], preferred_element_type=F32)
        o_ref[0, rows, :] = _rms(x_ref[0, rows, :] + y, fg_ref[...])


def _out_call(ya, yh, x, wa, wh, fg):
    B, S, _ = x.shape
    tm = OUT_TM
    grid = (B, S // tm)
    tok = lambda w: pl.BlockSpec((1, tm, w), lambda b, i: (b, i, 0))
    full = lambda a: pl.BlockSpec(a.shape, lambda b, i: (0,) * a.ndim, pipeline_mode=pl.Buffered(1))
    return pl.pallas_call(
        _out_kernel,
        grid=grid,
        in_specs=[tok(MLA_WIDTH), pl.BlockSpec((1, HG_HEADS, tm, LANES), lambda b, i: (b, 0, i, 0)),
                  tok(D_MODEL), full(wa), full(wh), full(fg)],
        out_specs=tok(D_MODEL),
        out_shape=jax.ShapeDtypeStruct((B, S, D_MODEL), F32),
        compiler_params=pltpu.CompilerParams(
            dimension_semantics=("parallel", "parallel"), vmem_limit_bytes=VMEM_LIMIT),
        name="outproj",
    )(ya, yh, x, wa, wh, fg)


def _pack_win_kernel(w_ref, o_ref):
    w = w_ref[...]
    split = _C_KRM + MLA_ROPE
    zeros = jnp.zeros((w.shape[0], LANES - MLA_ROPE), w.dtype)
    o_ref[...] = jnp.concatenate([w[:, :split], zeros, w[:, split:]], axis=1).astype(BF16)


def _pack_win(w_in):
    rows, cols = w_in.shape
    tr = PACK_ROWS
    return pl.pallas_call(
        _pack_win_kernel,
        grid=(rows // tr,),
        in_specs=[pl.BlockSpec((tr, cols), lambda i: (i, 0))],
        out_specs=pl.BlockSpec((tr, _C_END), lambda i: (i, 0)),
        out_shape=jax.ShapeDtypeStruct((rows, _C_END), BF16),
        compiler_params=pltpu.CompilerParams(dimension_semantics=("parallel",)),
        name="packwin",
    )(w_in)


def _pack_weights(w_in, w_q_b, w_kv_b):
    win = _pack_win(w_in)

    wq = w_q_b.reshape(MLA_Q_RANK, MLA_HEADS, MLA_NOPE + MLA_ROPE)
    wqm = jnp.transpose(wq, (1, 2, 0)).astype(BF16)

    wkv = w_kv_b.reshape(MLA_KV_RANK, MLA_HEADS, MLA_NOPE + MLA_V)
    wk = wkv[..., :MLA_NOPE].reshape(MLA_KV_RANK, MLA_HEADS * MLA_NOPE).astype(BF16)
    wvt = jnp.transpose(wkv[..., MLA_NOPE:], (1, 2, 0)).reshape(MLA_HEADS * MLA_V, MLA_KV_RANK).astype(BF16)
    return win, wqm, wk, wvt


def _rope_table():
    inv = ROPE_THETA ** (-jnp.arange(HALF, dtype=F32) / HALF)
    return inv.reshape(HALF, 1)


def kernel(x, positions, ln_g, w_in, q_a_norm_g, w_q_b, kv_a_norm_g, w_kv_b,
           hg_lower_bounds, hg_norm_g, w_out, final_norm_g):
    B, S, _ = x.shape
    assert ln_g.shape[0] == 1, "single-layer stack"
    win, wqm, wk, wvt = _pack_weights(w_in[0], w_q_b[0], w_kv_b[0])
    pos3 = positions.reshape(B, 1, S)
    q, k, vt, gm, hq, hf, hv, gh = _proj_call(
        x, pos3, ln_g[0:1], win, q_a_norm_g[0:1], wqm, kv_a_norm_g[0:1], wk, wvt,
        hg_lower_bounds, _rope_table())
    ya = _attn_call(q, k, vt, gm)
    yh = _hgrn_call(hf, hq, hv, gh, hg_norm_g[0:1])
    wo = w_out[0].astype(BF16)
    return _out_call(ya, yh, x, wo[:MLA_WIDTH], wo[MLA_WIDTH:], final_norm_g.reshape(1, D_MODEL))
```

```python
import functools
import math

import numpy as np
import jax
import jax.numpy as jnp
from jax import lax
from jax.experimental import pallas as pl
from jax.experimental.pallas import tpu as pltpu

F32 = jnp.float32
BF16 = jnp.bfloat16

D_MODEL = 1024
MLA_HEADS = 8
MLA_NOPE = 64
MLA_ROPE = 32
MLA_V = 64
MLA_Q_RANK = 256
MLA_KV_RANK = 128
MLA_WIDTH = MLA_HEADS * MLA_V
HG_HEADS = 4
HG_EXPAND = 128
HG_HEAD_V = 128
HG_WIDTH = HG_HEADS * HG_HEAD_V
HG_FDIM = HG_HEADS * HG_EXPAND
ROPE_THETA = 10000.0
EPS = 1e-6
HALF = MLA_ROPE // 2

LANES = 128
HEAD_SLAB = LANES
NEG_BIG = -1e30
LOG2E = math.log2(math.e)
VT_ROWS = 80

PROJ_TM = 1024
ATT_TQ = 1024
ATT_TK = 256
ATT_NH = 8
ATT_AHEAD = 8
ATT_UNROLL = 4
ATT_CW = 256
HG_CHUNK = 64
HG_SUB = 8
HG_LEVELS = 4
HG_TB = 512
OUT_TM = 2048
VMEM_LIMIT = 56 * 1024 * 1024

_C_QLAT = 0
_C_KVLAT = _C_QLAT + MLA_Q_RANK
_C_KRM = _C_KVLAT + MLA_KV_RANK
_C_GM = _C_KRM + LANES
_C_HQ = _C_GM + MLA_WIDTH
_C_HF = _C_HQ + HG_FDIM
_C_HI = _C_HF + HG_FDIM
_C_GH = _C_HI + HG_WIDTH
_C_END = _C_GH + HG_WIDTH

_NT = (((1,), (1,)), ((), ()))
_TN = (((0,), (0,)), ((), ()))


def _rms(x, g):
    return x * lax.rsqrt(jnp.mean(x * x, axis=-1, keepdims=True) + EPS) * g


def _silu(x):
    return x * jax.nn.sigmoid(x)


def _swap_rope_halves(slab):
    lane = lax.broadcasted_iota(jnp.int32, slab.shape, 1)
    return jnp.where(lane < HALF, pltpu.roll(slab, LANES - HALF, 1), pltpu.roll(slab, HALF, 1))


def _proj_kernel(x_ref, pos_ref, lng_ref, win_ref, qg_ref, wqt_ref,
                 kvg_ref, wk_ref, wvt_ref, lbraw_ref, rope_ref,
                 q_out, k_out, vt_out, gm_out, hq_out, hf_out, hv_out, gh_out):
    x = x_ref[0]
    tm = x.shape[0]
    h = _rms(x, lng_ref[...]).astype(BF16)

    def in_proj(c0, width):
        return jnp.dot(h, win_ref[:, c0:c0 + width], preferred_element_type=F32)

    lat = in_proj(0, _C_GM)

    ang_t = rope_ref[...] * pos_ref[0].astype(F32)
    cos_t = jnp.cos(ang_t)
    sin_t = jnp.sin(ang_t)
    cos_h = cos_t.T
    sin_h = sin_t.T
    pad = LANES - MLA_ROPE
    cos = jnp.concatenate([cos_h, cos_h, jnp.ones((tm, pad), F32)], axis=1)
    sin = jnp.concatenate([-sin_h, sin_h, jnp.zeros((tm, pad), F32)], axis=1)

    scale = LOG2E / math.sqrt(MLA_NOPE + MLA_ROPE)
    cos_q = jnp.concatenate([cos_t, cos_t], axis=0) * scale
    sin_q = jnp.concatenate([-sin_t, sin_t], axis=0) * scale
    zero_rows = jnp.zeros((LANES - MLA_NOPE - MLA_ROPE, tm), BF16)
    qn = _rms(lat[:, _C_QLAT:_C_QLAT + MLA_Q_RANK], qg_ref[...]).astype(BF16)
    for hd in range(MLA_HEADS):
        qt = lax.dot_general(wqt_ref[hd], qn, _NT, preferred_element_type=F32)
        nope = (qt[:MLA_NOPE] * scale).astype(BF16)
        x12 = qt[MLA_NOPE:]
        x21 = jnp.concatenate([x12[HALF:], x12[:HALF]], axis=0)
        rot = (x12 * cos_q + x21 * sin_q).astype(BF16)
        slab = [nope, rot, zero_rows] if hd % 2 == 0 else [rot, zero_rows, nope]
        q_out[0, hd] = jnp.concatenate(slab, axis=0)
    kvn = _rms(lat[:, _C_KVLAT:_C_KVLAT + MLA_KV_RANK], kvg_ref[...]).astype(BF16)
    kn = jnp.dot(kvn, wk_ref[...], preferred_element_type=F32)
    kr = lat[:, _C_KRM:_C_KRM + LANES]
    kr_odd = kr * cos + _swap_rope_halves(kr) * sin
    kr_even = pltpu.roll(kr_odd, MLA_NOPE, 1)
    lane = lax.broadcasted_iota(jnp.int32, (tm, LANES), 1)
    vt_all = lax.dot_general(wvt_ref[...], kvn, _NT, preferred_element_type=F32).astype(BF16)
    tail = (lax.broadcasted_iota(jnp.int32, (VT_ROWS - MLA_V, tm), 0) == 0).astype(BF16)
    for hd in range(MLA_HEADS):
        col = kn[:, (hd // 2) * LANES:(hd // 2 + 1) * LANES]
        k_slab = jnp.where(lane < MLA_NOPE, col, kr_even) if hd % 2 == 0 else jnp.where(lane >= MLA_NOPE, col, kr_odd)
        k_out[0, hd] = k_slab.astype(BF16)
        vt_out[0, hd] = jnp.concatenate([vt_all[hd * MLA_V:(hd + 1) * MLA_V], tail], axis=0)
    gm_out[0] = _silu(in_proj(_C_GM, MLA_WIDTH))

    a0 = lbraw_ref[0:1, :]
    a1 = lbraw_ref[1:2, :]
    mx = jnp.maximum(a0, a1)
    e0 = jnp.exp(a0 - mx)
    e1 = jnp.exp(a1 - mx)
    lb = e0 / (e0 + e1)
    def put_heads(out, val):
        for hd in range(HG_HEADS):
            out[0, hd] = val[:, hd * LANES:(hd + 1) * LANES]

    put_heads(hq_out, _silu(in_proj(_C_HQ, HG_FDIM)))
    put_heads(hf_out, lb + (1.0 - lb) * jax.nn.sigmoid(in_proj(_C_HF, HG_FDIM)))
    put_heads(hv_out, in_proj(_C_HI, HG_WIDTH).astype(BF16))
    put_heads(gh_out, _silu(in_proj(_C_GH, HG_WIDTH)))


def _proj_call(x, pos3, lng, win, qg, wqm, kvg, wk, wvt, lbraw, rope_tab):
    B, S, _ = x.shape
    tm = PROJ_TM
    grid = (B, S // tm)
    tok = lambda w: pl.BlockSpec((1, tm, w), lambda b, i: (b, i, 0))
    full = lambda a: pl.BlockSpec(a.shape, lambda b, i: (0,) * a.ndim, pipeline_mode=pl.Buffered(1))
    head = pl.BlockSpec((1, MLA_HEADS, tm, HEAD_SLAB), lambda b, i: (b, 0, i, 0))
    out_shape = (
        jax.ShapeDtypeStruct((B, MLA_HEADS, HEAD_SLAB, S), BF16),
        jax.ShapeDtypeStruct((B, MLA_HEADS, S, HEAD_SLAB), BF16),
        jax.ShapeDtypeStruct((B, MLA_HEADS, VT_ROWS, S), BF16),
        jax.ShapeDtypeStruct((B, S, MLA_WIDTH), F32),
        jax.ShapeDtypeStruct((B, HG_HEADS, S, LANES), F32),
        jax.ShapeDtypeStruct((B, HG_HEADS, S, LANES), F32),
        jax.ShapeDtypeStruct((B, HG_HEADS, S, LANES), BF16),
        jax.ShapeDtypeStruct((B, HG_HEADS, S, LANES), F32),
    )
    hg = pl.BlockSpec((1, HG_HEADS, tm, LANES), lambda b, i: (b, 0, i, 0))
    return pl.pallas_call(
        _proj_kernel,
        grid=grid,
        in_specs=[tok(D_MODEL), pl.BlockSpec((1, 1, tm), lambda b, i: (b, 0, i)), full(lng), full(win),
                  full(qg), full(wqm), full(kvg), full(wk), full(wvt), full(lbraw), full(rope_tab)],
        out_specs=(pl.BlockSpec((1, MLA_HEADS, HEAD_SLAB, tm), lambda b, i: (b, 0, 0, i)), head,
                   pl.BlockSpec((1, MLA_HEADS, VT_ROWS, tm), lambda b, i: (b, 0, 0, i)),
                   tok(MLA_WIDTH), hg, hg, hg, hg),
        out_shape=out_shape,
        compiler_params=pltpu.CompilerParams(
            dimension_semantics=("parallel", "parallel"), vmem_limit_bytes=VMEM_LIMIT),
        name="proj",
    )(x, pos3, lng, win, qg, wqm, kvg, wk, wvt, lbraw, rope_tab)


def _attn_kernel(q_ref, k_ref, vt_ref, g_ref, o_ref, s_scr, m_scr, acc_scr):
    qi = pl.program_id(2)
    tq, tk, nh, cw = ATT_TQ, ATT_TK, ATT_NH, ATT_CW
    m_scr[...] = jnp.full(m_scr.shape, NEG_BIG, F32)
    acc_scr[...] = jnp.zeros(acc_scr.shape, F32)

    def scores(h, j, r0, diag):
        k = k_ref[0, h, pl.ds(r0, tk), :]
        q = q_ref[0, h, :, j * cw:(j + 1) * cw]
        s = jnp.dot(k, q, preferred_element_type=F32)
        if diag:
            key = lax.broadcasted_iota(jnp.int32, (tk, cw), 0)
            qry = lax.broadcasted_iota(jnp.int32, (tk, cw), 1)
            s = jnp.where(key <= qry, s, NEG_BIG)
        s_scr[h, j] = s

    def softmax_pv(h, j, r0):
        c = j * cw
        vt = vt_ref[0, h, :, pl.ds(r0, tk)]
        m_old = m_scr[h, :, c:c + cw]
        m_new = jnp.maximum(m_old, jnp.max(s_scr[h, j], axis=0, keepdims=True))
        p = jnp.exp2(s_scr[h, j] - m_new).astype(BF16)
        m_scr[h, :, c:c + cw] = m_new
        pv = jnp.dot(vt, p, preferred_element_type=F32)
        acc_scr[h, j] = jnp.exp2(m_old - m_new) * acc_scr[h, j] + pv

    def step(r0, c0, masked):
        units = [(h, j) for h in range(nh) for j in range(c0 // cw, tq // cw)]
        for u, (h, j) in enumerate(units[:ATT_AHEAD]):
            scores(h, j, r0, masked and j == c0 // cw)
        for u, (h, j) in enumerate(units):
            if u + ATT_AHEAD < len(units):
                h2, j2 = units[u + ATT_AHEAD]
                scores(h2, j2, r0, masked and j2 == c0 // cw)
            softmax_pv(h, j, r0)

    def full_steps(kb, carry):
        for i in range(ATT_UNROLL):
            step(pl.multiple_of((kb * ATT_UNROLL + i) * tk, tk), 0, False)
        return carry

    lax.fori_loop(0, qi * (tq // tk // ATT_UNROLL), full_steps, 0)
    for j in range(tq // tk):
        step(pl.multiple_of(qi * tq + j * tk, tk), j * tk, True)

    def normalized(h):
        acc = jnp.concatenate([acc_scr[h, j] for j in range(tq // cw)], axis=1)
        return acc[:MLA_V] / acc[MLA_V:MLA_V + 1]

    for h in range(0, nh, 2):
        pair = jnp.concatenate([normalized(h), normalized(h + 1)], axis=0).T
        cols = slice(h * MLA_V, (h + 2) * MLA_V)
        o_ref[0, :, cols] = (pair * g_ref[0, :, cols]).astype(BF16)


def _attn_call(qt, k, vt, gm):
    B, H, S, _ = k.shape
    tq, nh = ATT_TQ, ATT_NH
    assert ATT_CW == ATT_TK and (tq // ATT_TK) % ATT_UNROLL == 0
    grid = (B, H // nh, S // tq)
    return pl.pallas_call(
        _attn_kernel,
        grid=grid,
        in_specs=[
            pl.BlockSpec((1, nh, HEAD_SLAB, tq), lambda b, j, i: (b, j, 0, i)),
            pl.BlockSpec((1, nh, S, HEAD_SLAB), lambda b, j, i: (b, j, 0, 0),
                         pipeline_mode=pl.Buffered(1)),
            pl.BlockSpec((1, nh, VT_ROWS, S), lambda b, j, i: (b, j, 0, 0),
                         pipeline_mode=pl.Buffered(1)),
            pl.BlockSpec((1, tq, nh * MLA_V), lambda b, j, i: (b, i, j)),
        ],
        out_specs=pl.BlockSpec((1, tq, nh * MLA_V), lambda b, j, i: (b, i, j)),
        out_shape=jax.ShapeDtypeStruct((B, S, MLA_WIDTH), BF16),
        scratch_shapes=[
            pltpu.VMEM((nh, tq // ATT_CW, ATT_TK, ATT_CW), F32),
            pltpu.VMEM((nh, 1, tq), F32),
            pltpu.VMEM((nh, tq // ATT_CW, VT_ROWS, ATT_CW), F32),
        ],
        compiler_params=pltpu.CompilerParams(
            dimension_semantics=("parallel", "parallel", "arbitrary"),
            vmem_limit_bytes=VMEM_LIMIT),
        name="attn",
    )(qt, k, vt, gm)


def _hgrn_constants():
    C, SUB, NL = HG_CHUNK, HG_SUB, HG_LEVELS
    t = np.arange(C)
    lm = np.zeros((2 * NL, C, C), np.float32)
    masks = np.zeros((NL, C, C), np.float32)
    masks[0] = (t[:, None] // SUB) == (t[None, :] // SUB)
    for l in range(NL):
        bs = SUB << l
        same = (t[:, None] // bs) == (t[None, :] // bs)
        lm[l] = same & (t[None, :] <= t[:, None])
        lm[NL + l] = same & (t[None, :] > t[:, None])
        if l < NL - 1:
            masks[l + 1] = ((t[:, None] // bs) == (t[None, :] // bs) + 1) & ((t[None, :] // bs) % 2 == 0)
    lmat = lm.reshape(2 * NL * C, C)
    lmat = np.concatenate([lmat, lmat], axis=1)
    j = np.arange(C)
    sel = np.zeros((SUB, LANES, C), np.float32)
    sel[:] = (j[None, None, :] % SUB) == np.arange(SUB)[:, None, None]
    return (jnp.asarray(lmat, BF16), jnp.asarray(sel.reshape(SUB * LANES, C), BF16),
            jnp.asarray(masks, F32))


def _hgrn_kernel(f_ref, q_ref, v_ref, g_ref, ng_ref, lmat_ref, sel_ref, mask_ref, o_ref,
                 st_scr, qs_scr, ks_scr, c8_scr, k3_scr, p2_scr, a_scr, ds_scr, oi_scr, dec_scr):
    C, SUB, NL, TB, NH = HG_CHUNK, HG_SUB, HG_LEVELS, HG_TB, HG_HEADS
    NC = TB // C

    @pl.when(pl.program_id(1) == 0)
    def _():
        st_scr[...] = jnp.zeros_like(st_scr)

    lmat = lmat_ref[...]
    nb = C // SUB
    heads = range(NH)

    for c in range(NC):
        rows = slice(c * C, (c + 1) * C)
        blks = slice(c * nb, (c + 1) * nb)
        f = [f_ref[0, h, rows, :] for h in heads]
        g = jnp.log2(jnp.concatenate(f, axis=1))
        g1 = g.astype(BF16)
        g2 = (g - g1.astype(F32)).astype(BF16)
        e = jnp.dot(lmat, jnp.concatenate([g1, g2], axis=0), preferred_element_type=F32)
        for h in heads:
            hs = slice(h * LANES, (h + 1) * LANES)
            q = q_ref[0, h, rows, :]
            k = 1.0 - f[h]
            k3_scr[h, blks] = k.reshape(nb, SUB, LANES)
            for l in range(NL):
                cq = e[l * C:(l + 1) * C, hs]
                ck = e[(NL + l) * C:(NL + l + 1) * C, hs]
                qs_scr[l, h, rows, :] = (q * jnp.exp2(cq)).astype(BF16)
                ks_scr[l, h, rows, :] = (k * jnp.exp2(ck)).astype(BF16)
                if l == 0:
                    c8_scr[h, blks] = cq.reshape(nb, SUB, LANES)
                if l == NL - 1:
                    dec_scr[c:c + 1, hs] = jnp.exp2(cq[C - 1:C, :])

    tt = lax.broadcasted_iota(jnp.int32, (1, SUB, LANES), 1)
    for h in heads:
        q3 = q_ref[0, h].reshape(TB // SUB, SUB, LANES)
        c8 = c8_scr[h]
        for s in range(SUB):
            bcast = pl.ds(s, SUB, stride=0)
            dg = jnp.where(tt >= s, c8 - c8_scr[h, :, bcast, :], NEG_BIG)
            p = q3 * jnp.exp2(dg) * k3_scr[h, :, bcast, :]
            p2_scr[h, s] = p.reshape(TB, LANES).astype(BF16)

    sel_masks = [mask_ref[l] > 0.5 for l in range(NL)]
    for h in heads:
        p2 = jnp.concatenate([p2_scr[h, s] for s in range(SUB)], axis=1)
        ad = jnp.dot(p2, sel_ref[...], preferred_element_type=F32)
        for c in range(NC):
            rows = slice(c * C, (c + 1) * C)
            a = jnp.where(sel_masks[0], ad[rows], 0.0)
            for l in range(NL - 1):
                al = lax.dot_general(qs_scr[l, h, rows, :], ks_scr[l, h, rows, :], _NT,
                                     preferred_element_type=F32)
                a = jnp.where(sel_masks[l + 1], al, a)
            a_scr[h, rows, :] = a.astype(BF16)
    for h in heads:
        for c in range(NC):
            rows = slice(c * C, (c + 1) * C)
            v = v_ref[0, h, rows, :]
            oi_scr[h, rows, :] = jnp.dot(a_scr[h, rows, :], v, preferred_element_type=F32)
            ds_scr[h * NC + c] = lax.dot_general(v, ks_scr[NL - 1, h, rows, :], _TN,
                                                 preferred_element_type=F32)

    for c in range(NC):
        rows = slice(c * C, (c + 1) * C)
        for h in heads:
            hs = slice(h * LANES, (h + 1) * LANES)
            st = st_scr[h]
            o = oi_scr[h, rows, :] + lax.dot_general(qs_scr[NL - 1, h, rows, :], st.astype(BF16), _NT,
                                                     preferred_element_type=F32)
            st_scr[h] = st * dec_scr[c:c + 1, hs] + ds_scr[h * NC + c]
            o = o * lax.rsqrt(jnp.mean(o * o, axis=-1, keepdims=True) + EPS) * ng_ref[:, hs]
            o_ref[0, h, rows, :] = (o * g_ref[0, h, rows, :]).astype(BF16)


def _hgrn_call(hf, hq, hv, gh, ng):
    B, NH, S, _ = hf.shape
    tb, C, SUB, NL = HG_TB, HG_CHUNK, HG_SUB, HG_LEVELS
    assert SUB << (NL - 1) == C and NH == HG_HEADS
    lmat, sel, masks = _hgrn_constants()
    grid = (B, S // tb)
    blk = pl.BlockSpec((1, NH, tb, LANES), lambda b, t: (b, 0, t, 0))
    full = lambda a: pl.BlockSpec(a.shape, lambda b, t: (0,) * a.ndim, pipeline_mode=pl.Buffered(1))
    return pl.pallas_call(
        _hgrn_kernel,
        grid=grid,
        in_specs=[blk, blk, blk, blk, full(ng), full(lmat), full(sel), full(masks)],
        out_specs=blk,
        out_shape=jax.ShapeDtypeStruct((B, NH, S, LANES), BF16),
        scratch_shapes=[
            pltpu.VMEM((NH, HG_HEAD_V, HG_EXPAND), F32),
            pltpu.VMEM((NL, NH, tb, LANES), BF16),
            pltpu.VMEM((NL, NH, tb, LANES), BF16),
            pltpu.VMEM((NH, tb // SUB, SUB, LANES), F32),
            pltpu.VMEM((NH, tb // SUB, SUB, LANES), F32),
            pltpu.VMEM((NH, SUB, tb, LANES), BF16),
            pltpu.VMEM((NH, tb, C), BF16),
            pltpu.VMEM((NH * (tb // C), HG_HEAD_V, HG_EXPAND), F32),
            pltpu.VMEM((NH, tb, LANES), F32),
            pltpu.VMEM((tb // C, NH * LANES), F32),
        ],
        compiler_params=pltpu.CompilerParams(
            dimension_semantics=("parallel", "arbitrary"),
            vmem_limit_bytes=VMEM_LIMIT),
        name="hgrn",
    )(hf, hq, hv, gh, ng, lmat, sel, masks)


def _out_kernel(ya_ref, yh_ref, x_ref, wa_ref, wh_ref, fg_ref, o_ref):
    y = jnp.dot(ya_ref[0], wa_ref[...], preferred_element_type=F32)
    yh = jnp.concatenate([yh_ref[0, h] for h in range(HG_HEADS)], axis=1)
    y = y + jnp.dot(yh, wh_ref[...], preferred_element_type=F32)
    o_ref[0] = _rms(x_ref[0] + y, fg_ref[...])


def _out_call(ya, yh, x, wa, wh, fg):
    B, S, _ = x.shape
    tm = OUT_TM
    grid = (B, S // tm)
    tok = lambda w: pl.BlockSpec((1, tm, w), lambda b, i: (b, i, 0))
    full = lambda a: pl.BlockSpec(a.shape, lambda b, i: (0,) * a.ndim, pipeline_mode=pl.Buffered(1))
    return pl.pallas_call(
        _out_kernel,
        grid=grid,
        in_specs=[tok(MLA_WIDTH), pl.BlockSpec((1, HG_HEADS, tm, LANES), lambda b, i: (b, 0, i, 0)),
                  tok(D_MODEL), full(wa), full(wh), full(fg)],
        out_specs=tok(D_MODEL),
        out_shape=jax.ShapeDtypeStruct((B, S, D_MODEL), F32),
        compiler_params=pltpu.CompilerParams(
            dimension_semantics=("parallel", "parallel"), vmem_limit_bytes=VMEM_LIMIT),
        name="outproj",
    )(ya, yh, x, wa, wh, fg)


def _pack_win_kernel(wt_ref, o_ref):
    j = pl.program_id(0)
    kr_blk = _C_KRM // LANES
    shift = LANES - MLA_ROPE
    start = pl.multiple_of(jnp.where(j <= kr_blk, j * LANES, j * LANES - shift), 8)
    blk = wt_ref[pl.ds(start, LANES), :].T
    lane = lax.broadcasted_iota(jnp.int32, blk.shape, 1)
    blk = jnp.where(jnp.logical_and(j == kr_blk, lane >= MLA_ROPE), 0.0, blk)
    o_ref[...] = blk.astype(BF16)


def _pack_win(w_in):
    rows, cols = w_in.shape
    wt = jnp.transpose(w_in)
    pad_rows = _C_END - (LANES - MLA_ROPE) - cols
    assert pad_rows == 0
    return pl.pallas_call(
        _pack_win_kernel,
        grid=(_C_END // LANES,),
        in_specs=[pl.BlockSpec((cols, rows), lambda j: (0, 0), pipeline_mode=pl.Buffered(1))],
        out_specs=pl.BlockSpec((rows, LANES), lambda j: (0, j)),
        out_shape=jax.ShapeDtypeStruct((rows, _C_END), BF16),
        compiler_params=pltpu.CompilerParams(dimension_semantics=("arbitrary",), vmem_limit_bytes=VMEM_LIMIT),
        name="packwin",
    )(wt)


def _pack_weights(w_in, w_q_b, w_kv_b):
    win = _pack_win(w_in)

    wq = w_q_b.reshape(MLA_Q_RANK, MLA_HEADS, MLA_NOPE + MLA_ROPE)
    wqm = jnp.transpose(wq, (1, 2, 0)).astype(BF16)

    wkv = w_kv_b.reshape(MLA_KV_RANK, MLA_HEADS, MLA_NOPE + MLA_V)
    wk = wkv[..., :MLA_NOPE].reshape(MLA_KV_RANK, MLA_HEADS * MLA_NOPE).astype(BF16)
    wvt = jnp.transpose(wkv[..., MLA_NOPE:], (1, 2, 0)).reshape(MLA_HEADS * MLA_V, MLA_KV_RANK).astype(BF16)
    return win, wqm, wk, wvt


def _rope_table():
    inv = ROPE_THETA ** (-jnp.arange(HALF, dtype=F32) / HALF)
    return inv.reshape(HALF, 1)


def kernel(x, positions, ln_g, w_in, q_a_norm_g, w_q_b, kv_a_norm_g, w_kv_b,
           hg_lower_bounds, hg_norm_g, w_out, final_norm_g):
    B, S, _ = x.shape
    assert ln_g.shape[0] == 1, "single-layer stack"
    win, wqm, wk, wvt = _pack_weights(w_in[0], w_q_b[0], w_kv_b[0])
    pos3 = positions.reshape(B, 1, S)
    q, k, vt, gm, hq, hf, hv, gh = _proj_call(
        x, pos3, ln_g[0:1], win, q_a_norm_g[0:1], wqm, kv_a_norm_g[0:1], wk, wvt,
        hg_lower_bounds, _rope_table())
    ya = _attn_call(q, k, vt, gm)
    yh = _hgrn_call(hf, hq, hv, gh, hg_norm_g[0:1])
    wo = w_out[0].astype(BF16)
    return _out_call(ya, yh, x, wo[:MLA_WIDTH], wo[MLA_WIDTH:], final_norm_g.reshape(1, D_MODEL))
```

```python
import math

import numpy as np
import jax
import jax.numpy as jnp
from jax import lax
from jax.experimental import pallas as pl
from jax.experimental.pallas import tpu as pltpu

F32 = jnp.float32
BF16 = jnp.bfloat16

D_MODEL = 1024
MLA_HEADS = 8
MLA_NOPE = 64
MLA_ROPE = 32
MLA_V = 64
MLA_Q_RANK = 256
MLA_KV_RANK = 128
MLA_WIDTH = MLA_HEADS * MLA_V
HG_HEADS = 4
HG_EXPAND = 128
HG_HEAD_V = 128
HG_WIDTH = HG_HEADS * HG_HEAD_V
HG_FDIM = HG_HEADS * HG_EXPAND
ROPE_THETA = 10000.0
EPS = 1e-6
HALF = MLA_ROPE // 2

LANES = 128
HEAD_SLAB = LANES
NEG_BIG = -1e30
LOG2E = math.log2(math.e)
VT_ROWS = 80

PROJ_TM = 1024
ATT_TQ = 1024
ATT_TK = 256
ATT_NH = 8
ATT_AHEAD = 8
ATT_UNROLL = 4
ATT_CW = 256
HG_CHUNK = 64
HG_SUB = 8
HG_LEVELS = 4
HG_TB = 512
OUT_TM = 2048
VMEM_LIMIT = 56 * 1024 * 1024

_C_QLAT = 0
_C_KVLAT = _C_QLAT + MLA_Q_RANK
_C_KRM = _C_KVLAT + MLA_KV_RANK
_C_GM = _C_KRM + LANES
_C_HQ = _C_GM + MLA_WIDTH
_C_HF = _C_HQ + HG_FDIM
_C_HI = _C_HF + HG_FDIM
_C_GH = _C_HI + HG_WIDTH
_C_END = _C_GH + HG_WIDTH

_NT = (((1,), (1,)), ((), ()))
_TN = (((0,), (0,)), ((), ()))


def _rms(x, g):
    return x * lax.rsqrt(jnp.mean(x * x, axis=-1, keepdims=True) + EPS) * g


def _silu(x):
    return x * jax.nn.sigmoid(x)


def _swap_rope_halves(slab):
    lane = lax.broadcasted_iota(jnp.int32, slab.shape, 1)
    return jnp.where(lane < HALF, pltpu.roll(slab, LANES - HALF, 1), pltpu.roll(slab, HALF, 1))


def _proj_kernel(x_ref, pos_ref, lng_ref, win_ref, qg_ref, wqt_ref,
                 kvg_ref, wk_ref, wvt_ref, lbraw_ref, rope_ref,
                 q_out, k_out, vt_out, gm_out, hq_out, hf_out, hv_out, gh_out):
    x = x_ref[0]
    tm = x.shape[0]
    h = _rms(x, lng_ref[...]).astype(BF16)

    def in_proj(c0, width):
        return jnp.dot(h, win_ref[:, c0:c0 + width], preferred_element_type=F32)

    lat = in_proj(0, _C_GM)

    ang_t = rope_ref[...] * pos_ref[0].astype(F32)
    cos_t = jnp.cos(ang_t)
    sin_t = jnp.sin(ang_t)
    cos_h = cos_t.T
    sin_h = sin_t.T
    pad = LANES - MLA_ROPE
    cos = jnp.concatenate([cos_h, cos_h, jnp.ones((tm, pad), F32)], axis=1)
    sin = jnp.concatenate([-sin_h, sin_h, jnp.zeros((tm, pad), F32)], axis=1)

    scale = LOG2E / math.sqrt(MLA_NOPE + MLA_ROPE)
    cos_q = jnp.concatenate([cos_t, cos_t], axis=0) * scale
    sin_q = jnp.concatenate([-sin_t, sin_t], axis=0) * scale
    zero_rows = jnp.zeros((LANES - MLA_NOPE - MLA_ROPE, tm), BF16)
    qn = _rms(lat[:, _C_QLAT:_C_QLAT + MLA_Q_RANK], qg_ref[...]).astype(BF16)
    for hd in range(MLA_HEADS):
        qt = lax.dot_general(wqt_ref[hd], qn, _NT, preferred_element_type=F32)
        nope = (qt[:MLA_NOPE] * scale).astype(BF16)
        x12 = qt[MLA_NOPE:]
        x21 = jnp.concatenate([x12[HALF:], x12[:HALF]], axis=0)
        rot = (x12 * cos_q + x21 * sin_q).astype(BF16)
        slab = [nope, rot, zero_rows] if hd % 2 == 0 else [rot, zero_rows, nope]
        q_out[0, hd] = jnp.concatenate(slab, axis=0)
    kvn = _rms(lat[:, _C_KVLAT:_C_KVLAT + MLA_KV_RANK], kvg_ref[...]).astype(BF16)
    kn = jnp.dot(kvn, wk_ref[...], preferred_element_type=F32)
    kr = lat[:, _C_KRM:_C_KRM + LANES]
    kr_odd = kr * cos + _swap_rope_halves(kr) * sin
    kr_even = pltpu.roll(kr_odd, MLA_NOPE, 1)
    lane = lax.broadcasted_iota(jnp.int32, (tm, LANES), 1)
    vt_all = lax.dot_general(wvt_ref[...], kvn, _NT, preferred_element_type=F32).astype(BF16)
    tail = (lax.broadcasted_iota(jnp.int32, (VT_ROWS - MLA_V, tm), 0) == 0).astype(BF16)
    for hd in range(MLA_HEADS):
        col = kn[:, (hd // 2) * LANES:(hd // 2 + 1) * LANES]
        k_slab = jnp.where(lane < MLA_NOPE, col, kr_even) if hd % 2 == 0 else jnp.where(lane >= MLA_NOPE, col, kr_odd)
        k_out[0, hd] = k_slab.astype(BF16)
        vt_out[0, hd] = jnp.concatenate([vt_all[hd * MLA_V:(hd + 1) * MLA_V], tail], axis=0)
    gm_out[0] = _silu(in_proj(_C_GM, MLA_WIDTH))

    a0 = lbraw_ref[0:1, :]
    a1 = lbraw_ref[1:2, :]
    mx = jnp.maximum(a0, a1)
    e0 = jnp.exp(a0 - mx)
    e1 = jnp.exp(a1 - mx)
    lb = e0 / (e0 + e1)
    def put_heads(out, val):
        for hd in range(HG_HEADS):
            out[0, hd] = val[:, hd * LANES:(hd + 1) * LANES]

    put_heads(hq_out, _silu(in_proj(_C_HQ, HG_FDIM)))
    put_heads(hf_out, lb + (1.0 - lb) * jax.nn.sigmoid(in_proj(_C_HF, HG_FDIM)))
    put_heads(hv_out, in_proj(_C_HI, HG_WIDTH).astype(BF16))
    put_heads(gh_out, _silu(in_proj(_C_GH, HG_WIDTH)))


def _proj_call(x, pos3, lng, win, qg, wqm, kvg, wk, wvt, lbraw, rope_tab):
    B, S, _ = x.shape
    tm = PROJ_TM
    grid = (B, S // tm)
    tok = lambda w: pl.BlockSpec((1, tm, w), lambda b, i: (b, i, 0))
    full = lambda a: pl.BlockSpec(a.shape, lambda b, i: (0,) * a.ndim, pipeline_mode=pl.Buffered(1))
    head = pl.BlockSpec((1, MLA_HEADS, tm, HEAD_SLAB), lambda b, i: (b, 0, i, 0))
    out_shape = (
        jax.ShapeDtypeStruct((B, MLA_HEADS, HEAD_SLAB, S), BF16),
        jax.ShapeDtypeStruct((B, MLA_HEADS, S, HEAD_SLAB), BF16),
        jax.ShapeDtypeStruct((B, MLA_HEADS, VT_ROWS, S), BF16),
        jax.ShapeDtypeStruct((B, S, MLA_WIDTH), F32),
        jax.ShapeDtypeStruct((B, HG_HEADS, S, LANES), F32),
        jax.ShapeDtypeStruct((B, HG_HEADS, S, LANES), F32),
        jax.ShapeDtypeStruct((B, HG_HEADS, S, LANES), BF16),
        jax.ShapeDtypeStruct((B, HG_HEADS, S, LANES), F32),
    )
    hg = pl.BlockSpec((1, HG_HEADS, tm, LANES), lambda b, i: (b, 0, i, 0))
    return pl.pallas_call(
        _proj_kernel,
        grid=grid,
        in_specs=[tok(D_MODEL), pl.BlockSpec((1, 1, tm), lambda b, i: (b, 0, i)), full(lng), full(win),
                  full(qg), full(wqm), full(kvg), full(wk), full(wvt), full(lbraw), full(rope_tab)],
        out_specs=(pl.BlockSpec((1, MLA_HEADS, HEAD_SLAB, tm), lambda b, i: (b, 0, 0, i)), head,
                   pl.BlockSpec((1, MLA_HEADS, VT_ROWS, tm), lambda b, i: (b, 0, 0, i)),
                   tok(MLA_WIDTH), hg, hg, hg, hg),
        out_shape=out_shape,
        compiler_params=pltpu.CompilerParams(
            dimension_semantics=("parallel", "parallel"), vmem_limit_bytes=VMEM_LIMIT),
        name="proj",
    )(x, pos3, lng, win, qg, wqm, kvg, wk, wvt, lbraw, rope_tab)


def _attn_kernel(q_ref, k_ref, vt_ref, g_ref, o_ref, s_scr, m_scr, acc_scr):
    qi = pl.program_id(2)
    tq, tk, nh, cw = ATT_TQ, ATT_TK, ATT_NH, ATT_CW
    m_scr[...] = jnp.full(m_scr.shape, NEG_BIG, F32)
    acc_scr[...] = jnp.zeros(acc_scr.shape, F32)

    def scores(h, j, r0, diag):
        k = k_ref[0, h, pl.ds(r0, tk), :]
        q = q_ref[0, h, :, j * cw:(j + 1) * cw]
        s = jnp.dot(k, q, preferred_element_type=F32)
        if diag:
            key = lax.broadcasted_iota(jnp.int32, (tk, cw), 0)
            qry = lax.broadcasted_iota(jnp.int32, (tk, cw), 1)
            s = jnp.where(key <= qry, s, NEG_BIG)
        s_scr[h, j] = s

    def softmax_pv(h, j, r0):
        c = j * cw
        vt = vt_ref[0, h, :, pl.ds(r0, tk)]
        m_old = m_scr[h, :, c:c + cw]
        m_new = jnp.maximum(m_old, jnp.max(s_scr[h, j], axis=0, keepdims=True))
        p = jnp.exp2(s_scr[h, j] - m_new).astype(BF16)
        m_scr[h, :, c:c + cw] = m_new
        pv = jnp.dot(vt, p, preferred_element_type=F32)
        acc_scr[h, j] = jnp.exp2(m_old - m_new) * acc_scr[h, j] + pv

    def step(r0, c0, masked):
        units = [(h, j) for h in range(nh) for j in range(c0 // cw, tq // cw)]
        for u, (h, j) in enumerate(units[:ATT_AHEAD]):
            scores(h, j, r0, masked and j == c0 // cw)
        for u, (h, j) in enumerate(units):
            if u + ATT_AHEAD < len(units):
                h2, j2 = units[u + ATT_AHEAD]
                scores(h2, j2, r0, masked and j2 == c0 // cw)
            softmax_pv(h, j, r0)

    def full_steps(kb, carry):
        for i in range(ATT_UNROLL):
            step(pl.multiple_of((kb * ATT_UNROLL + i) * tk, tk), 0, False)
        return carry

    lax.fori_loop(0, qi * (tq // tk // ATT_UNROLL), full_steps, 0)
    for j in range(tq // tk):
        step(pl.multiple_of(qi * tq + j * tk, tk), j * tk, True)

    def normalized(h):
        acc = jnp.concatenate([acc_scr[h, j] for j in range(tq // cw)], axis=1)
        return acc[:MLA_V] / acc[MLA_V:MLA_V + 1]

    for h in range(0, nh, 2):
        pair = jnp.concatenate([normalized(h), normalized(h + 1)], axis=0).T
        cols = slice(h * MLA_V, (h + 2) * MLA_V)
        o_ref[0, :, cols] = (pair * g_ref[0, :, cols]).astype(BF16)


def _attn_call(qt, k, vt, gm):
    B, H, S, _ = k.shape
    tq, nh = ATT_TQ, ATT_NH
    assert ATT_CW == ATT_TK and (tq // ATT_TK) % ATT_UNROLL == 0
    grid = (B, H // nh, S // tq)
    return pl.pallas_call(
        _attn_kernel,
        grid=grid,
        in_specs=[
            pl.BlockSpec((1, nh, HEAD_SLAB, tq), lambda b, j, i: (b, j, 0, i)),
            pl.BlockSpec((1, nh, S, HEAD_SLAB), lambda b, j, i: (b, j, 0, 0)),
            pl.BlockSpec((1, nh, VT_ROWS, S), lambda b, j, i: (b, j, 0, 0)),
            pl.BlockSpec((1, tq, nh * MLA_V), lambda b, j, i: (b, i, j)),
        ],
        out_specs=pl.BlockSpec((1, tq, nh * MLA_V), lambda b, j, i: (b, i, j)),
        out_shape=jax.ShapeDtypeStruct((B, S, MLA_WIDTH), BF16),
        scratch_shapes=[
            pltpu.VMEM((nh, tq // ATT_CW, ATT_TK, ATT_CW), F32),
            pltpu.VMEM((nh, 1, tq), F32),
            pltpu.VMEM((nh, tq // ATT_CW, VT_ROWS, ATT_CW), F32),
        ],
        compiler_params=pltpu.CompilerParams(
            dimension_semantics=("parallel", "parallel", "arbitrary"),
            vmem_limit_bytes=VMEM_LIMIT),
        name="attn",
    )(qt, k, vt, gm)


def _hgrn_constants():
    C, SUB, NL = HG_CHUNK, HG_SUB, HG_LEVELS
    t = np.arange(C)
    lm = np.zeros((2 * NL, C, C), np.float32)
    masks = np.zeros((NL, C, C), np.float32)
    masks[0] = (t[:, None] // SUB) == (t[None, :] // SUB)
    for l in range(NL):
        bs = SUB << l
        same = (t[:, None] // bs) == (t[None, :] // bs)
        lm[l] = same & (t[None, :] <= t[:, None])
        lm[NL + l] = same & (t[None, :] > t[:, None])
        if l < NL - 1:
            masks[l + 1] = ((t[:, None] // bs) == (t[None, :] // bs) + 1) & ((t[None, :] // bs) % 2 == 0)
    lmat = lm.reshape(2 * NL * C, C)
    lmat = np.concatenate([lmat, lmat], axis=1)
    j = np.arange(C)
    sel = np.zeros((SUB, LANES, C), np.float32)
    sel[:] = (j[None, None, :] % SUB) == np.arange(SUB)[:, None, None]
    return (jnp.asarray(lmat, BF16), jnp.asarray(sel.reshape(SUB * LANES, C), BF16),
            jnp.asarray(masks, F32))


def _hgrn_kernel(f_ref, q_ref, v_ref, g_ref, ng_ref, lmat_ref, sel_ref, mask_ref, o_ref,
                 st_scr, qs_scr, ks_scr, c8_scr, k3_scr, p2_scr, a_scr, ds_scr, oi_scr, dec_scr):
    C, SUB, NL, TB, NH = HG_CHUNK, HG_SUB, HG_LEVELS, HG_TB, HG_HEADS
    NC = TB // C

    @pl.when(pl.program_id(1) == 0)
    def _():
        st_scr[...] = jnp.zeros_like(st_scr)

    lmat = lmat_ref[...]
    nb = C // SUB
    heads = range(NH)

    for c in range(NC):
        rows = slice(c * C, (c + 1) * C)
        blks = slice(c * nb, (c + 1) * nb)
        f = [f_ref[0, h, rows, :] for h in heads]
        g = jnp.log2(jnp.concatenate(f, axis=1))
        g1 = g.astype(BF16)
        g2 = (g - g1.astype(F32)).astype(BF16)
        e = jnp.dot(lmat, jnp.concatenate([g1, g2], axis=0), preferred_element_type=F32)
        for h in heads:
            hs = slice(h * LANES, (h + 1) * LANES)
            q = q_ref[0, h, rows, :]
            k = 1.0 - f[h]
            k3_scr[h, blks] = k.reshape(nb, SUB, LANES)
            for l in range(NL):
                cq = e[l * C:(l + 1) * C, hs]
                ck = e[(NL + l) * C:(NL + l + 1) * C, hs]
                qs_scr[l, h, rows, :] = (q * jnp.exp2(cq)).astype(BF16)
                ks_scr[l, h, rows, :] = (k * jnp.exp2(ck)).astype(BF16)
                if l == 0:
                    c8_scr[h, blks] = cq.reshape(nb, SUB, LANES)
                if l == NL - 1:
                    dec_scr[c:c + 1, hs] = jnp.exp2(cq[C - 1:C, :])

    tt = lax.broadcasted_iota(jnp.int32, (1, SUB, LANES), 1)
    for h in heads:
        q3 = q_ref[0, h].reshape(TB // SUB, SUB, LANES)
        c8 = c8_scr[h]
        for s in range(SUB):
            bcast = pl.ds(s, SUB, stride=0)
            dg = jnp.where(tt >= s, c8 - c8_scr[h, :, bcast, :], NEG_BIG)
            p = q3 * jnp.exp2(dg) * k3_scr[h, :, bcast, :]
            p2_scr[h, s] = p.reshape(TB, LANES).astype(BF16)

    sel_masks = [mask_ref[l] > 0.5 for l in range(NL)]
    for h in heads:
        p2 = jnp.concatenate([p2_scr[h, s] for s in range(SUB)], axis=1)
        ad = jnp.dot(p2, sel_ref[...], preferred_element_type=F32)
        for c in range(NC):
            rows = slice(c * C, (c + 1) * C)
            a = jnp.where(sel_masks[0], ad[rows], 0.0)
            for l in range(NL - 1):
                al = lax.dot_general(qs_scr[l, h, rows, :], ks_scr[l, h, rows, :], _NT,
                                     preferred_element_type=F32)
                a = jnp.where(sel_masks[l + 1], al, a)
            a_scr[h, rows, :] = a.astype(BF16)
    for h in heads:
        for c in range(NC):
            rows = slice(c * C, (c + 1) * C)
            v = v_ref[0, h, rows, :]
            oi_scr[h, rows, :] = jnp.dot(a_scr[h, rows, :], v, preferred_element_type=F32)
            ds_scr[h * NC + c] = lax.dot_general(v, ks_scr[NL - 1, h, rows, :], _TN,
                                                 preferred_element_type=F32)

    for c in range(NC):
        rows = slice(c * C, (c + 1) * C)
        for h in heads:
            hs = slice(h * LANES, (h + 1) * LANES)
            st = st_scr[h]
            o = oi_scr[h, rows, :] + lax.dot_general(qs_scr[NL - 1, h, rows, :], st.astype(BF16), _NT,
                                                     preferred_element_type=F32)
            st_scr[h] = st * dec_scr[c:c + 1, hs] + ds_scr[h * NC + c]
            o = o * lax.rsqrt(jnp.mean(o * o, axis=-1, keepdims=True) + EPS) * ng_ref[:, hs]
            o_ref[0, h, rows, :] = (o * g_ref[0, h, rows, :]).astype(BF16)


def _hgrn_call(hf, hq, hv, gh, ng):
    B, NH, S, _ = hf.shape
    tb, C, SUB, NL = HG_TB, HG_CHUNK, HG_SUB, HG_LEVELS
    assert SUB << (NL - 1) == C and NH == HG_HEADS
    lmat, sel, masks = _hgrn_constants()
    grid = (B, S // tb)
    blk = pl.BlockSpec((1, NH, tb, LANES), lambda b, t: (b, 0, t, 0))
    full = lambda a: pl.BlockSpec(a.shape, lambda b, t: (0,) * a.ndim, pipeline_mode=pl.Buffered(1))
    return pl.pallas_call(
        _hgrn_kernel,
        grid=grid,
        in_specs=[blk, blk, blk, blk, full(ng), full(lmat), full(sel), full(masks)],
        out_specs=blk,
        out_shape=jax.ShapeDtypeStruct((B, NH, S, LANES), BF16),
        scratch_shapes=[
            pltpu.VMEM((NH, HG_HEAD_V, HG_EXPAND), F32),
            pltpu.VMEM((NL, NH, tb, LANES), BF16),
            pltpu.VMEM((NL, NH, tb, LANES), BF16),
            pltpu.VMEM((NH, tb // SUB, SUB, LANES), F32),
            pltpu.VMEM((NH, tb // SUB, SUB, LANES), F32),
            pltpu.VMEM((NH, SUB, tb, LANES), BF16),
            pltpu.VMEM((NH, tb, C), BF16),
            pltpu.VMEM((NH * (tb // C), HG_HEAD_V, HG_EXPAND), F32),
            pltpu.VMEM((NH, tb, LANES), F32),
            pltpu.VMEM((tb // C, NH * LANES), F32),
        ],
        compiler_params=pltpu.CompilerParams(
            dimension_semantics=("parallel", "arbitrary"),
            vmem_limit_bytes=VMEM_LIMIT),
        name="hgrn",
    )(hf, hq, hv, gh, ng, lmat, sel, masks)


def _out_kernel(ya_ref, yh_ref, x_ref, wa_ref, wh_ref, fg_ref, o_ref):
    y = jnp.dot(ya_ref[0], wa_ref[...], preferred_element_type=F32)
    yh = jnp.concatenate([yh_ref[0, h] for h in range(HG_HEADS)], axis=1)
    y = y + jnp.dot(yh, wh_ref[...], preferred_element_type=F32)
    o_ref[0] = _rms(x_ref[0] + y, fg_ref[...])


def _out_call(ya, yh, x, wa, wh, fg):
    B, S, _ = x.shape
    tm = OUT_TM
    grid = (B, S // tm)
    tok = lambda w: pl.BlockSpec((1, tm, w), lambda b, i: (b, i, 0))
    full = lambda a: pl.BlockSpec(a.shape, lambda b, i: (0,) * a.ndim, pipeline_mode=pl.Buffered(1))
    return pl.pallas_call(
        _out_kernel,
        grid=grid,
        in_specs=[tok(MLA_WIDTH), pl.BlockSpec((1, HG_HEADS, tm, LANES), lambda b, i: (b, 0, i, 0)),
                  tok(D_MODEL), full(wa), full(wh), full(fg)],
        out_specs=tok(D_MODEL),
        out_shape=jax.ShapeDtypeStruct((B, S, D_MODEL), F32),
        compiler_params=pltpu.CompilerParams(
            dimension_semantics=("parallel", "parallel"), vmem_limit_bytes=VMEM_LIMIT),
        name="outproj",
    )(ya, yh, x, wa, wh, fg)


def _pack_win_kernel(wt_ref, o_ref):
    j = pl.program_id(0)
    kr_blk = _C_KRM // LANES
    shift = LANES - MLA_ROPE
    start = pl.multiple_of(jnp.where(j <= kr_blk, j * LANES, j * LANES - shift), 8)
    blk = wt_ref[pl.ds(start, LANES), :].T
    lane = lax.broadcasted_iota(jnp.int32, blk.shape, 1)
    blk = jnp.where(jnp.logical_and(j == kr_blk, lane >= MLA_ROPE), 0.0, blk)
    o_ref[...] = blk.astype(BF16)


def _pack_win(w_in):
    rows, cols = w_in.shape
    wt = jnp.transpose(w_in)
    pad_rows = _C_END - (LANES - MLA_ROPE) - cols
    assert pad_rows == 0
    return pl.pallas_call(
        _pack_win_kernel,
        grid=(_C_END // LANES,),
        in_specs=[pl.BlockSpec((cols, rows), lambda j: (0, 0), pipeline_mode=pl.Buffered(1))],
        out_specs=pl.BlockSpec((rows, LANES), lambda j: (0, j)),
        out_shape=jax.ShapeDtypeStruct((rows, _C_END), BF16),
        compiler_params=pltpu.CompilerParams(dimension_semantics=("arbitrary",), vmem_limit_bytes=VMEM_LIMIT),
        name="packwin",
    )(wt)


def _pack_weights(w_in, w_q_b, w_kv_b):
    win = _pack_win(w_in)

    wq = w_q_b.reshape(MLA_Q_RANK, MLA_HEADS, MLA_NOPE + MLA_ROPE)
    wqm = jnp.transpose(wq, (1, 2, 0)).astype(BF16)

    wkv = w_kv_b.reshape(MLA_KV_RANK, MLA_HEADS, MLA_NOPE + MLA_V)
    wk = wkv[..., :MLA_NOPE].reshape(MLA_KV_RANK, MLA_HEADS * MLA_NOPE).astype(BF16)
    wvt = jnp.transpose(wkv[..., MLA_NOPE:], (1, 2, 0)).reshape(MLA_HEADS * MLA_V, MLA_KV_RANK).astype(BF16)
    return win, wqm, wk, wvt


def _rope_table():
    inv = ROPE_THETA ** (-jnp.arange(HALF, dtype=F32) / HALF)
    return inv.reshape(HALF, 1)


def kernel(x, positions, ln_g, w_in, q_a_norm_g, w_q_b, kv_a_norm_g, w_kv_b,
           hg_lower_bounds, hg_norm_g, w_out, final_norm_g):
    B, S, _ = x.shape
    assert ln_g.shape[0] == 1, "single-layer stack"
    win, wqm, wk, wvt = _pack_weights(w_in[0], w_q_b[0], w_kv_b[0])
    pos3 = positions.reshape(B, 1, S)
    q, k, vt, gm, hq, hf, hv, gh = _proj_call(
        x, pos3, ln_g[0:1], win, q_a_norm_g[0:1], wqm, kv_a_norm_g[0:1], wk, wvt,
        hg_lower_bounds, _rope_table())
    ya = _attn_call(q, k, vt, gm)
    yh = _hgrn_call(hf, hq, hv, gh, hg_norm_g[0:1])
    wo = w_out[0].astype(BF16)
    return _out_call(ya, yh, x, wo[:MLA_WIDTH], wo[MLA_WIDTH:], final_norm_g.reshape(1, D_MODEL))
```

```python
import math

import numpy as np
import jax
import jax.numpy as jnp
from jax import lax
from jax.experimental import pallas as pl
from jax.experimental.pallas import tpu as pltpu

F32 = jnp.float32
BF16 = jnp.bfloat16

D_MODEL = 1024
MLA_HEADS = 8
MLA_NOPE = 64
MLA_ROPE = 32
MLA_V = 64
MLA_Q_RANK = 256
MLA_KV_RANK = 128
MLA_WIDTH = MLA_HEADS * MLA_V
HG_HEADS = 4
HG_EXPAND = 128
HG_HEAD_V = 128
HG_WIDTH = HG_HEADS * HG_HEAD_V
HG_FDIM = HG_HEADS * HG_EXPAND
ROPE_THETA = 10000.0
EPS = 1e-6
HALF = MLA_ROPE // 2

LANES = 128
HEAD_SLAB = LANES
NEG_BIG = -1e30
LOG2E = math.log2(math.e)
VT_ROWS = 80

PROJ_TM = 1024
ATT_TQ = 1024
ATT_TK = 256
ATT_NH = 8
ATT_AHEAD = 8
ATT_UNROLL = 4
ATT_CW = 256
HG_CHUNK = 64
HG_SUB = 8
HG_LEVELS = 4
HG_TB = 1024
OUT_TM = 2048
VMEM_LIMIT = 56 * 1024 * 1024

_C_QLAT = 0
_C_KVLAT = _C_QLAT + MLA_Q_RANK
_C_KRM = _C_KVLAT + MLA_KV_RANK
_C_GM = _C_KRM + LANES
_C_HQ = _C_GM + MLA_WIDTH
_C_HF = _C_HQ + HG_FDIM
_C_HI = _C_HF + HG_FDIM
_C_GH = _C_HI + HG_WIDTH
_C_END = _C_GH + HG_WIDTH

_NT = (((1,), (1,)), ((), ()))
_TN = (((0,), (0,)), ((), ()))


def _rms(x, g):
    return x * lax.rsqrt(jnp.mean(x * x, axis=-1, keepdims=True) + EPS) * g


def _silu(x):
    return x * jax.nn.sigmoid(x)


def _swap_rope_halves(slab):
    lane = lax.broadcasted_iota(jnp.int32, slab.shape, 1)
    return jnp.where(lane < HALF, pltpu.roll(slab, LANES - HALF, 1), pltpu.roll(slab, HALF, 1))


def _proj_kernel(x_ref, pos_ref, lng_ref, win_ref, qg_ref, wqt_ref,
                 kvg_ref, wk_ref, wvt_ref, lbraw_ref, rope_ref,
                 q_out, k_out, vt_out, gm_out, hq_out, hf_out, hv_out, gh_out):
    x = x_ref[0]
    tm = x.shape[0]
    h = _rms(x, lng_ref[...]).astype(BF16)

    def in_proj(c0, width):
        return jnp.dot(h, win_ref[:, c0:c0 + width], preferred_element_type=F32)

    lat = in_proj(0, _C_GM)

    ang_t = rope_ref[...] * pos_ref[0].astype(F32)
    cos_t = jnp.cos(ang_t)
    sin_t = jnp.sin(ang_t)
    cos_h = cos_t.T
    sin_h = sin_t.T
    pad = LANES - MLA_ROPE
    cos = jnp.concatenate([cos_h, cos_h, jnp.ones((tm, pad), F32)], axis=1)
    sin = jnp.concatenate([-sin_h, sin_h, jnp.zeros((tm, pad), F32)], axis=1)

    scale = LOG2E / math.sqrt(MLA_NOPE + MLA_ROPE)
    cos_q = jnp.concatenate([cos_t, cos_t], axis=0) * scale
    sin_q = jnp.concatenate([-sin_t, sin_t], axis=0) * scale
    zero_rows = jnp.zeros((LANES - MLA_NOPE - MLA_ROPE, tm), BF16)
    qn = _rms(lat[:, _C_QLAT:_C_QLAT + MLA_Q_RANK], qg_ref[...]).astype(BF16)
    for hd in range(MLA_HEADS):
        qt = lax.dot_general(wqt_ref[hd], qn, _NT, preferred_element_type=F32)
        nope = (qt[:MLA_NOPE] * scale).astype(BF16)
        x12 = qt[MLA_NOPE:]
        x21 = jnp.concatenate([x12[HALF:], x12[:HALF]], axis=0)
        rot = (x12 * cos_q + x21 * sin_q).astype(BF16)
        slab = [nope, rot, zero_rows] if hd % 2 == 0 else [rot, zero_rows, nope]
        q_out[0, hd] = jnp.concatenate(slab, axis=0)
    kvn = _rms(lat[:, _C_KVLAT:_C_KVLAT + MLA_KV_RANK], kvg_ref[...]).astype(BF16)
    kn = jnp.dot(kvn, wk_ref[...], preferred_element_type=F32)
    kr = lat[:, _C_KRM:_C_KRM + LANES]
    kr_odd = kr * cos + _swap_rope_halves(kr) * sin
    kr_even = pltpu.roll(kr_odd, MLA_NOPE, 1)
    lane = lax.broadcasted_iota(jnp.int32, (tm, LANES), 1)
    vt_all = lax.dot_general(wvt_ref[...], kvn, _NT, preferred_element_type=F32).astype(BF16)
    tail = (lax.broadcasted_iota(jnp.int32, (VT_ROWS - MLA_V, tm), 0) == 0).astype(BF16)
    for hd in range(MLA_HEADS):
        col = kn[:, (hd // 2) * LANES:(hd // 2 + 1) * LANES]
        k_slab = jnp.where(lane < MLA_NOPE, col, kr_even) if hd % 2 == 0 else jnp.where(lane >= MLA_NOPE, col, kr_odd)
        k_out[0, hd] = k_slab.astype(BF16)
        vt_out[0, hd] = jnp.concatenate([vt_all[hd * MLA_V:(hd + 1) * MLA_V], tail], axis=0)
    gm_out[0] = _silu(in_proj(_C_GM, MLA_WIDTH))

    a0 = lbraw_ref[0:1, :]
    a1 = lbraw_ref[1:2, :]
    mx = jnp.maximum(a0, a1)
    e0 = jnp.exp(a0 - mx)
    e1 = jnp.exp(a1 - mx)
    lb = e0 / (e0 + e1)
    def put_heads(out, val):
        for hd in range(HG_HEADS):
            out[0, hd] = val[:, hd * LANES:(hd + 1) * LANES]

    put_heads(hq_out, _silu(in_proj(_C_HQ, HG_FDIM)))
    put_heads(hf_out, lb + (1.0 - lb) * jax.nn.sigmoid(in_proj(_C_HF, HG_FDIM)))
    put_heads(hv_out, in_proj(_C_HI, HG_WIDTH).astype(BF16))
    put_heads(gh_out, _silu(in_proj(_C_GH, HG_WIDTH)))


def _proj_call(x, pos3, lng, win, qg, wqm, kvg, wk, wvt, lbraw, rope_tab):
    B, S, _ = x.shape
    tm = PROJ_TM
    grid = (B, S // tm)
    tok = lambda w: pl.BlockSpec((1, tm, w), lambda b, i: (b, i, 0))
    full = lambda a: pl.BlockSpec(a.shape, lambda b, i: (0,) * a.ndim, pipeline_mode=pl.Buffered(1))
    head = pl.BlockSpec((1, MLA_HEADS, tm, HEAD_SLAB), lambda b, i: (b, 0, i, 0))
    out_shape = (
        jax.ShapeDtypeStruct((B, MLA_HEADS, HEAD_SLAB, S), BF16),
        jax.ShapeDtypeStruct((B, MLA_HEADS, S, HEAD_SLAB), BF16),
        jax.ShapeDtypeStruct((B, MLA_HEADS, VT_ROWS, S), BF16),
        jax.ShapeDtypeStruct((B, S, MLA_WIDTH), F32),
        jax.ShapeDtypeStruct((B, HG_HEADS, S, LANES), F32),
        jax.ShapeDtypeStruct((B, HG_HEADS, S, LANES), F32),
        jax.ShapeDtypeStruct((B, HG_HEADS, S, LANES), BF16),
        jax.ShapeDtypeStruct((B, HG_HEADS, S, LANES), F32),
    )
    hg = pl.BlockSpec((1, HG_HEADS, tm, LANES), lambda b, i: (b, 0, i, 0))
    return pl.pallas_call(
        _proj_kernel,
        grid=grid,
        in_specs=[tok(D_MODEL), pl.BlockSpec((1, 1, tm), lambda b, i: (b, 0, i)), full(lng), full(win),
                  full(qg), full(wqm), full(kvg), full(wk), full(wvt), full(lbraw), full(rope_tab)],
        out_specs=(pl.BlockSpec((1, MLA_HEADS, HEAD_SLAB, tm), lambda b, i: (b, 0, 0, i)), head,
                   pl.BlockSpec((1, MLA_HEADS, VT_ROWS, tm), lambda b, i: (b, 0, 0, i)),
                   tok(MLA_WIDTH), hg, hg, hg, hg),
        out_shape=out_shape,
        compiler_params=pltpu.CompilerParams(
            dimension_semantics=("parallel", "parallel"), vmem_limit_bytes=VMEM_LIMIT),
        name="proj",
    )(x, pos3, lng, win, qg, wqm, kvg, wk, wvt, lbraw, rope_tab)


def _attn_kernel(q_ref, k_ref, vt_ref, g_ref, o_ref, s_scr, m_scr, acc_scr):
    qi = pl.program_id(2)
    tq, tk, nh, cw = ATT_TQ, ATT_TK, ATT_NH, ATT_CW
    m_scr[...] = jnp.full(m_scr.shape, NEG_BIG, F32)
    acc_scr[...] = jnp.zeros(acc_scr.shape, F32)

    def scores(h, j, r0, diag):
        k = k_ref[0, h, pl.ds(r0, tk), :]
        q = q_ref[0, h, :, j * cw:(j + 1) * cw]
        s = jnp.dot(k, q, preferred_element_type=F32)
        if diag:
            key = lax.broadcasted_iota(jnp.int32, (tk, cw), 0)
            qry = lax.broadcasted_iota(jnp.int32, (tk, cw), 1)
            s = jnp.where(key <= qry, s, NEG_BIG)
        s_scr[h, j] = s

    def softmax_pv(h, j, r0):
        c = j * cw
        vt = vt_ref[0, h, :, pl.ds(r0, tk)]
        m_old = m_scr[h, :, c:c + cw]
        m_new = jnp.maximum(m_old, jnp.max(s_scr[h, j], axis=0, keepdims=True))
        p = jnp.exp2(s_scr[h, j] - m_new).astype(BF16)
        m_scr[h, :, c:c + cw] = m_new
        pv = jnp.dot(vt, p, preferred_element_type=F32)
        acc_scr[h, j] = jnp.exp2(m_old - m_new) * acc_scr[h, j] + pv

    def step(r0, c0, masked):
        units = [(h, j) for h in range(nh) for j in range(c0 // cw, tq // cw)]
        for u, (h, j) in enumerate(units[:ATT_AHEAD]):
            scores(h, j, r0, masked and j == c0 // cw)
        for u, (h, j) in enumerate(units):
            if u + ATT_AHEAD < len(units):
                h2, j2 = units[u + ATT_AHEAD]
                scores(h2, j2, r0, masked and j2 == c0 // cw)
            softmax_pv(h, j, r0)

    def full_steps(kb, carry):
        for i in range(ATT_UNROLL):
            step(pl.multiple_of((kb * ATT_UNROLL + i) * tk, tk), 0, False)
        return carry

    lax.fori_loop(0, qi * (tq // tk // ATT_UNROLL), full_steps, 0)
    for j in range(tq // tk):
        step(pl.multiple_of(qi * tq + j * tk, tk), j * tk, True)

    def normalized(h):
        acc = jnp.concatenate([acc_scr[h, j] for j in range(tq // cw)], axis=1)
        return acc[:MLA_V] / acc[MLA_V:MLA_V + 1]

    for h in range(0, nh, 2):
        pair = jnp.concatenate([normalized(h), normalized(h + 1)], axis=0).T
        cols = slice(h * MLA_V, (h + 2) * MLA_V)
        o_ref[0, :, cols] = (pair * g_ref[0, :, cols]).astype(BF16)


def _attn_call(qt, k, vt, gm):
    B, H, S, _ = k.shape
    tq, nh = ATT_TQ, ATT_NH
    assert ATT_CW == ATT_TK and (tq // ATT_TK) % ATT_UNROLL == 0
    grid = (B, H // nh, S // tq)
    return pl.pallas_call(
        _attn_kernel,
        grid=grid,
        in_specs=[
            pl.BlockSpec((1, nh, HEAD_SLAB, tq), lambda b, j, i: (b, j, 0, i)),
            pl.BlockSpec((1, nh, S, HEAD_SLAB), lambda b, j, i: (b, j, 0, 0)),
            pl.BlockSpec((1, nh, VT_ROWS, S), lambda b, j, i: (b, j, 0, 0)),
            pl.BlockSpec((1, tq, nh * MLA_V), lambda b, j, i: (b, i, j)),
        ],
        out_specs=pl.BlockSpec((1, tq, nh * MLA_V), lambda b, j, i: (b, i, j)),
        out_shape=jax.ShapeDtypeStruct((B, S, MLA_WIDTH), BF16),
        scratch_shapes=[
            pltpu.VMEM((nh, tq // ATT_CW, ATT_TK, ATT_CW), F32),
            pltpu.VMEM((nh, 1, tq), F32),
            pltpu.VMEM((nh, tq // ATT_CW, VT_ROWS, ATT_CW), F32),
        ],
        compiler_params=pltpu.CompilerParams(
            dimension_semantics=("parallel", "parallel", "arbitrary"),
            vmem_limit_bytes=VMEM_LIMIT),
        name="attn",
    )(qt, k, vt, gm)


def _hgrn_constants():
    C, SUB, NL = HG_CHUNK, HG_SUB, HG_LEVELS
    t = np.arange(C)
    lm = np.zeros((2 * NL, C, C), np.float32)
    masks = np.zeros((NL, C, C), np.float32)
    masks[0] = (t[:, None] // SUB) == (t[None, :] // SUB)
    for l in range(NL):
        bs = SUB << l
        same = (t[:, None] // bs) == (t[None, :] // bs)
        lm[l] = same & (t[None, :] <= t[:, None])
        lm[NL + l] = same & (t[None, :] > t[:, None])
        if l < NL - 1:
            masks[l + 1] = ((t[:, None] // bs) == (t[None, :] // bs) + 1) & ((t[None, :] // bs) % 2 == 0)
    lmat = lm.reshape(2 * NL * C, C)
    lmat = np.concatenate([lmat, lmat], axis=1)
    j = np.arange(C)
    sel = np.zeros((SUB, LANES, C), np.float32)
    sel[:] = (j[None, None, :] % SUB) == np.arange(SUB)[:, None, None]
    return (jnp.asarray(lmat, BF16), jnp.asarray(sel.reshape(SUB * LANES, C), BF16),
            jnp.asarray(masks, F32))


def _hgrn_kernel(f_ref, q_ref, v_ref, g_ref, ng_ref, lmat_ref, sel_ref, mask_ref, o_ref,
                 st_scr, qs_scr, ks_scr, c8_scr, k3_scr, p2_scr, a_scr, ds_scr, oi_scr, dec_scr):
    C, SUB, NL, TB, NH = HG_CHUNK, HG_SUB, HG_LEVELS, HG_TB, HG_HEADS
    NC = TB // C

    @pl.when(pl.program_id(1) == 0)
    def _():
        st_scr[...] = jnp.zeros_like(st_scr)

    lmat = lmat_ref[...]
    nb = C // SUB
    heads = range(NH)

    for c in range(NC):
        rows = slice(c * C, (c + 1) * C)
        blks = slice(c * nb, (c + 1) * nb)
        f = [f_ref[0, h, rows, :] for h in heads]
        g = jnp.log2(jnp.concatenate(f, axis=1))
        g1 = g.astype(BF16)
        g2 = (g - g1.astype(F32)).astype(BF16)
        e = jnp.dot(lmat, jnp.concatenate([g1, g2], axis=0), preferred_element_type=F32)
        for h in heads:
            hs = slice(h * LANES, (h + 1) * LANES)
            q = q_ref[0, h, rows, :]
            k = 1.0 - f[h]
            k3_scr[h, blks] = k.reshape(nb, SUB, LANES)
            for l in range(NL):
                cq = e[l * C:(l + 1) * C, hs]
                ck = e[(NL + l) * C:(NL + l + 1) * C, hs]
                qs_scr[l, h, rows, :] = (q * jnp.exp2(cq)).astype(BF16)
                ks_scr[l, h, rows, :] = (k * jnp.exp2(ck)).astype(BF16)
                if l == 0:
                    c8_scr[h, blks] = cq.reshape(nb, SUB, LANES)
                if l == NL - 1:
                    dec_scr[c:c + 1, hs] = jnp.exp2(cq[C - 1:C, :])

    tt = lax.broadcasted_iota(jnp.int32, (1, SUB, LANES), 1)
    for h in heads:
        q3 = q_ref[0, h].reshape(TB // SUB, SUB, LANES)
        c8 = c8_scr[h]
        for s in range(SUB):
            bcast = pl.ds(s, SUB, stride=0)
            dg = jnp.where(tt >= s, c8 - c8_scr[h, :, bcast, :], NEG_BIG)
            p = q3 * jnp.exp2(dg) * k3_scr[h, :, bcast, :]
            p2_scr[h, s] = p.reshape(TB, LANES).astype(BF16)

    sel_masks = [mask_ref[l] > 0.5 for l in range(NL)]
    for h in heads:
        p2 = jnp.concatenate([p2_scr[h, s] for s in range(SUB)], axis=1)
        ad = jnp.dot(p2, sel_ref[...], preferred_element_type=F32)
        for c in range(NC):
            rows = slice(c * C, (c + 1) * C)
            a = jnp.where(sel_masks[0], ad[rows], 0.0)
            for l in range(NL - 1):
                al = lax.dot_general(qs_scr[l, h, rows, :], ks_scr[l, h, rows, :], _NT,
                                     preferred_element_type=F32)
                a = jnp.where(sel_masks[l + 1], al, a)
            a_scr[h, rows, :] = a.astype(BF16)
    for h in heads:
        for c in range(NC):
            rows = slice(c * C, (c + 1) * C)
            v = v_ref[0, h, rows, :]
            oi_scr[h, rows, :] = jnp.dot(a_scr[h, rows, :], v, preferred_element_type=F32)
            ds_scr[h * NC + c] = lax.dot_general(v, ks_scr[NL - 1, h, rows, :], _TN,
                                                 preferred_element_type=F32)

    for c in range(NC):
        rows = slice(c * C, (c + 1) * C)
        for h in heads:
            hs = slice(h * LANES, (h + 1) * LANES)
            st = st_scr[h]
            o = oi_scr[h, rows, :] + lax.dot_general(qs_scr[NL - 1, h, rows, :], st.astype(BF16), _NT,
                                                     preferred_element_type=F32)
            st_scr[h] = st * dec_scr[c:c + 1, hs] + ds_scr[h * NC + c]
            o = o * lax.rsqrt(jnp.mean(o * o, axis=-1, keepdims=True) + EPS) * ng_ref[:, hs]
            o_ref[0, h, rows, :] = (o * g_ref[0, h, rows, :]).astype(BF16)


def _hgrn_call(hf, hq, hv, gh, ng):
    B, NH, S, _ = hf.shape
    tb, C, SUB, NL = HG_TB, HG_CHUNK, HG_SUB, HG_LEVELS
    assert SUB << (NL - 1) == C and NH == HG_HEADS
    lmat, sel, masks = _hgrn_constants()
    grid = (B, S // tb)
    blk = pl.BlockSpec((1, NH, tb, LANES), lambda b, t: (b, 0, t, 0))
    full = lambda a: pl.BlockSpec(a.shape, lambda b, t: (0,) * a.ndim, pipeline_mode=pl.Buffered(1))
    return pl.pallas_call(
        _hgrn_kernel,
        grid=grid,
        in_specs=[blk, blk, blk, blk, full(ng), full(lmat), full(sel), full(masks)],
        out_specs=blk,
        out_shape=jax.ShapeDtypeStruct((B, NH, S, LANES), BF16),
        scratch_shapes=[
            pltpu.VMEM((NH, HG_HEAD_V, HG_EXPAND), F32),
            pltpu.VMEM((NL, NH, tb, LANES), BF16),
            pltpu.VMEM((NL, NH, tb, LANES), BF16),
            pltpu.VMEM((NH, tb // SUB, SUB, LANES), F32),
            pltpu.VMEM((NH, tb // SUB, SUB, LANES), F32),
            pltpu.VMEM((NH, SUB, tb, LANES), BF16),
            pltpu.VMEM((NH, tb, C), BF16),
            pltpu.VMEM((NH * (tb // C), HG_HEAD_V, HG_EXPAND), F32),
            pltpu.VMEM((NH, tb, LANES), F32),
            pltpu.VMEM((tb // C, NH * LANES), F32),
        ],
        compiler_params=pltpu.CompilerParams(
            dimension_semantics=("parallel", "arbitrary"),
            vmem_limit_bytes=VMEM_LIMIT),
        name="hgrn",
    )(hf, hq, hv, gh, ng, lmat, sel, masks)


def _out_kernel(ya_ref, yh_ref, x_ref, wa_ref, wh_ref, fg_ref, o_ref):
    y = jnp.dot(ya_ref[0], wa_ref[...], preferred_element_type=F32)
    yh = jnp.concatenate([yh_ref[0, h] for h in range(HG_HEADS)], axis=1)
    y = y + jnp.dot(yh, wh_ref[...], preferred_element_type=F32)
    o_ref[0] = _rms(x_ref[0] + y, fg_ref[...])


def _out_call(ya, yh, x, wa, wh, fg):
    B, S, _ = x.shape
    tm = OUT_TM
    grid = (B, S // tm)
    tok = lambda w: pl.BlockSpec((1, tm, w), lambda b, i: (b, i, 0))
    full = lambda a: pl.BlockSpec(a.shape, lambda b, i: (0,) * a.ndim, pipeline_mode=pl.Buffered(1))
    return pl.pallas_call(
        _out_kernel,
        grid=grid,
        in_specs=[tok(MLA_WIDTH), pl.BlockSpec((1, HG_HEADS, tm, LANES), lambda b, i: (b, 0, i, 0)),
                  tok(D_MODEL), full(wa), full(wh), full(fg)],
        out_specs=tok(D_MODEL),
        out_shape=jax.ShapeDtypeStruct((B, S, D_MODEL), F32),
        compiler_params=pltpu.CompilerParams(
            dimension_semantics=("parallel", "parallel"), vmem_limit_bytes=VMEM_LIMIT),
        name="outproj",
    )(ya, yh, x, wa, wh, fg)


def _pack_win_kernel(wt_ref, o_ref):
    j = pl.program_id(0)
    kr_blk = _C_KRM // LANES
    shift = LANES - MLA_ROPE
    start = pl.multiple_of(jnp.where(j <= kr_blk, j * LANES, j * LANES - shift), 8)
    blk = wt_ref[pl.ds(start, LANES), :].T
    lane = lax.broadcasted_iota(jnp.int32, blk.shape, 1)
    blk = jnp.where(jnp.logical_and(j == kr_blk, lane >= MLA_ROPE), 0.0, blk)
    o_ref[...] = blk.astype(BF16)


def _pack_win(w_in):
    rows, cols = w_in.shape
    wt = jnp.transpose(w_in)
    pad_rows = _C_END - (LANES - MLA_ROPE) - cols
    assert pad_rows == 0
    return pl.pallas_call(
        _pack_win_kernel,
        grid=(_C_END // LANES,),
        in_specs=[pl.BlockSpec((cols, rows), lambda j: (0, 0), pipeline_mode=pl.Buffered(1))],
        out_specs=pl.BlockSpec((rows, LANES), lambda j: (0, j)),
        out_shape=jax.ShapeDtypeStruct((rows, _C_END), BF16),
        compiler_params=pltpu.CompilerParams(dimension_semantics=("arbitrary",), vmem_limit_bytes=VMEM_LIMIT),
        name="packwin",
    )(wt)


def _pack_weights(w_in, w_q_b, w_kv_b):
    win = _pack_win(w_in)

    wq = w_q_b.reshape(MLA_Q_RANK, MLA_HEADS, MLA_NOPE + MLA_ROPE)
    wqm = jnp.transpose(wq, (1, 2, 0)).astype(BF16)

    wkv = w_kv_b.reshape(MLA_KV_RANK, MLA_HEADS, MLA_NOPE + MLA_V)
    wk = wkv[..., :MLA_NOPE].reshape(MLA_KV_RANK, MLA_HEADS * MLA_NOPE).astype(BF16)
    wvt = jnp.transpose(wkv[..., MLA_NOPE:], (1, 2, 0)).reshape(MLA_HEADS * MLA_V, MLA_KV_RANK).astype(BF16)
    return win, wqm, wk, wvt


def _rope_table():
    inv = ROPE_THETA ** (-jnp.arange(HALF, dtype=F32) / HALF)
    return inv.reshape(HALF, 1)


def kernel(x, positions, ln_g, w_in, q_a_norm_g, w_q_b, kv_a_norm_g, w_kv_b,
           hg_lower_bounds, hg_norm_g, w_out, final_norm_g):
    B, S, _ = x.shape
    assert ln_g.shape[0] == 1, "single-layer stack"
    win, wqm, wk, wvt = _pack_weights(w_in[0], w_q_b[0], w_kv_b[0])
    pos3 = positions.reshape(B, 1, S)
    q, k, vt, gm, hq, hf, hv, gh = _proj_call(
        x, pos3, ln_g[0:1], win, q_a_norm_g[0:1], wqm, kv_a_norm_g[0:1], wk, wvt,
        hg_lower_bounds, _rope_table())
    ya = _attn_call(q, k, vt, gm)
    yh = _hgrn_call(hf, hq, hv, gh, hg_norm_g[0:1])
    wo = w_out[0].astype(BF16)
    return _out_call(ya, yh, x, wo[:MLA_WIDTH], wo[MLA_WIDTH:], final_norm_g.reshape(1, D_MODEL))
```

```python
import math

import numpy as np
import jax
import jax.numpy as jnp
from jax import lax
from jax.experimental import pallas as pl
from jax.experimental.pallas import tpu as pltpu

F32 = jnp.float32
BF16 = jnp.bfloat16

D_MODEL = 1024
MLA_HEADS = 8
MLA_NOPE = 64
MLA_ROPE = 32
MLA_V = 64
MLA_Q_RANK = 256
MLA_KV_RANK = 128
MLA_WIDTH = MLA_HEADS * MLA_V
HG_HEADS = 4
HG_EXPAND = 128
HG_HEAD_V = 128
HG_WIDTH = HG_HEADS * HG_HEAD_V
HG_FDIM = HG_HEADS * HG_EXPAND
ROPE_THETA = 10000.0
EPS = 1e-6
HALF = MLA_ROPE // 2

LANES = 128
HEAD_SLAB = LANES
NEG_BIG = -1e30
LOG2E = math.log2(math.e)
VT_ROWS = 80

PROJ_TM = 1024
ATT_TQ = 1024
ATT_TK = 256
ATT_NH = 8
ATT_AHEAD = 8
ATT_UNROLL = 4
ATT_CW = 256
HG_CHUNK = 64
HG_SUB = 8
HG_LEVELS = 4
HG_TB = 1024
OUT_TM = 2048
VMEM_LIMIT = 56 * 1024 * 1024

_C_QLAT = 0
_C_KVLAT = _C_QLAT + MLA_Q_RANK
_C_KRM = _C_KVLAT + MLA_KV_RANK
_C_GM = _C_KRM + LANES
_C_HQ = _C_GM + MLA_WIDTH
_C_HF = _C_HQ + HG_FDIM
_C_HI = _C_HF + HG_FDIM
_C_GH = _C_HI + HG_WIDTH
_C_END = _C_GH + HG_WIDTH

_NT = (((1,), (1,)), ((), ()))
_TN = (((0,), (0,)), ((), ()))


def _rms(x, g):
    return x * lax.rsqrt(jnp.mean(x * x, axis=-1, keepdims=True) + EPS) * g


def _silu(x):
    return x * jax.nn.sigmoid(x)


def _swap_rope_halves(slab):
    lane = lax.broadcasted_iota(jnp.int32, slab.shape, 1)
    return jnp.where(lane < HALF, pltpu.roll(slab, LANES - HALF, 1), pltpu.roll(slab, HALF, 1))


def _proj_kernel(x_ref, pos_ref, lng_ref, win_ref, qg_ref, wqt_ref,
                 kvg_ref, wk_ref, wvt_ref, lbraw_ref, rope_ref,
                 q_out, k_out, vt_out, gm_out, hq_out, hf_out, hv_out, gh_out):
    x = x_ref[0]
    tm = x.shape[0]
    h = _rms(x, lng_ref[...]).astype(BF16)

    def in_proj(c0, width):
        return jnp.dot(h, win_ref[:, c0:c0 + width], preferred_element_type=F32)

    lat = in_proj(0, _C_GM)
    gm_out[0] = _silu(in_proj(_C_GM, MLA_WIDTH))

    ang_t = rope_ref[...] * pos_ref[0].astype(F32)
    cos_t = jnp.cos(ang_t)
    sin_t = jnp.sin(ang_t)
    cos_h = cos_t.T
    sin_h = sin_t.T
    pad = LANES - MLA_ROPE
    cos = jnp.concatenate([cos_h, cos_h, jnp.ones((tm, pad), F32)], axis=1)
    sin = jnp.concatenate([-sin_h, sin_h, jnp.zeros((tm, pad), F32)], axis=1)

    scale = LOG2E / math.sqrt(MLA_NOPE + MLA_ROPE)
    cos_q = jnp.concatenate([cos_t, cos_t], axis=0) * scale
    sin_q = jnp.concatenate([-sin_t, sin_t], axis=0) * scale
    zero_rows = jnp.zeros((LANES - MLA_NOPE - MLA_ROPE, tm), BF16)
    qn = _rms(lat[:, _C_QLAT:_C_QLAT + MLA_Q_RANK], qg_ref[...]).astype(BF16)
    for hd in range(MLA_HEADS):
        qt = lax.dot_general(wqt_ref[hd], qn, _NT, preferred_element_type=F32)
        nope = (qt[:MLA_NOPE] * scale).astype(BF16)
        x12 = qt[MLA_NOPE:]
        x21 = jnp.concatenate([x12[HALF:], x12[:HALF]], axis=0)
        rot = (x12 * cos_q + x21 * sin_q).astype(BF16)
        slab = [nope, rot, zero_rows] if hd % 2 == 0 else [rot, zero_rows, nope]
        q_out[0, hd] = jnp.concatenate(slab, axis=0)
    kvn = _rms(lat[:, _C_KVLAT:_C_KVLAT + MLA_KV_RANK], kvg_ref[...]).astype(BF16)
    kn = jnp.dot(kvn, wk_ref[...], preferred_element_type=F32)
    kr = lat[:, _C_KRM:_C_KRM + LANES]
    kr_odd = kr * cos + _swap_rope_halves(kr) * sin
    kr_even = pltpu.roll(kr_odd, MLA_NOPE, 1)
    lane = lax.broadcasted_iota(jnp.int32, (tm, LANES), 1)
    vt_all = lax.dot_general(wvt_ref[...], kvn, _NT, preferred_element_type=F32).astype(BF16)
    tail = (lax.broadcasted_iota(jnp.int32, (VT_ROWS - MLA_V, tm), 0) == 0).astype(BF16)
    for hd in range(MLA_HEADS):
        col = kn[:, (hd // 2) * LANES:(hd // 2 + 1) * LANES]
        k_slab = jnp.where(lane < MLA_NOPE, col, kr_even) if hd % 2 == 0 else jnp.where(lane >= MLA_NOPE, col, kr_odd)
        k_out[0, hd] = k_slab.astype(BF16)
        vt_out[0, hd] = jnp.concatenate([vt_all[hd * MLA_V:(hd + 1) * MLA_V], tail], axis=0)

    a0 = lbraw_ref[0:1, :]
    a1 = lbraw_ref[1:2, :]
    mx = jnp.maximum(a0, a1)
    e0 = jnp.exp(a0 - mx)
    e1 = jnp.exp(a1 - mx)
    lb = e0 / (e0 + e1)
    def put_heads(out, val):
        for hd in range(HG_HEADS):
            out[0, hd] = val[:, hd * LANES:(hd + 1) * LANES]

    put_heads(hq_out, _silu(in_proj(_C_HQ, HG_FDIM)))
    put_heads(hf_out, lb + (1.0 - lb) * jax.nn.sigmoid(in_proj(_C_HF, HG_FDIM)))
    put_heads(hv_out, in_proj(_C_HI, HG_WIDTH).astype(BF16))
    put_heads(gh_out, _silu(in_proj(_C_GH, HG_WIDTH)))


def _proj_call(x, pos3, lng, win, qg, wqm, kvg, wk, wvt, lbraw, rope_tab):
    B, S, _ = x.shape
    tm = PROJ_TM
    grid = (B, S // tm)
    tok = lambda w: pl.BlockSpec((1, tm, w), lambda b, i: (b, i, 0))
    full = lambda a: pl.BlockSpec(a.shape, lambda b, i: (0,) * a.ndim, pipeline_mode=pl.Buffered(1))
    head = pl.BlockSpec((1, MLA_HEADS, tm, HEAD_SLAB), lambda b, i: (b, 0, i, 0))
    out_shape = (
        jax.ShapeDtypeStruct((B, MLA_HEADS, HEAD_SLAB, S), BF16),
        jax.ShapeDtypeStruct((B, MLA_HEADS, S, HEAD_SLAB), BF16),
        jax.ShapeDtypeStruct((B, MLA_HEADS, VT_ROWS, S), BF16),
        jax.ShapeDtypeStruct((B, S, MLA_WIDTH), F32),
        jax.ShapeDtypeStruct((B, HG_HEADS, S, LANES), F32),
        jax.ShapeDtypeStruct((B, HG_HEADS, S, LANES), F32),
        jax.ShapeDtypeStruct((B, HG_HEADS, S, LANES), BF16),
        jax.ShapeDtypeStruct((B, HG_HEADS, S, LANES), F32),
    )
    hg = pl.BlockSpec((1, HG_HEADS, tm, LANES), lambda b, i: (b, 0, i, 0))
    return pl.pallas_call(
        _proj_kernel,
        grid=grid,
        in_specs=[tok(D_MODEL), pl.BlockSpec((1, 1, tm), lambda b, i: (b, 0, i)), full(lng), full(win),
                  full(qg), full(wqm), full(kvg), full(wk), full(wvt), full(lbraw), full(rope_tab)],
        out_specs=(pl.BlockSpec((1, MLA_HEADS, HEAD_SLAB, tm), lambda b, i: (b, 0, 0, i)), head,
                   pl.BlockSpec((1, MLA_HEADS, VT_ROWS, tm), lambda b, i: (b, 0, 0, i)),
                   tok(MLA_WIDTH), hg, hg, hg, hg),
        out_shape=out_shape,
        compiler_params=pltpu.CompilerParams(
            dimension_semantics=("parallel", "parallel"), vmem_limit_bytes=VMEM_LIMIT),
        name="proj",
    )(x, pos3, lng, win, qg, wqm, kvg, wk, wvt, lbraw, rope_tab)


def _attn_kernel(q_ref, k_ref, vt_ref, g_ref, o_ref, s_scr, m_scr, acc_scr):
    qi = pl.program_id(2)
    tq, tk, nh, cw = ATT_TQ, ATT_TK, ATT_NH, ATT_CW
    m_scr[...] = jnp.full(m_scr.shape, NEG_BIG, F32)
    acc_scr[...] = jnp.zeros(acc_scr.shape, F32)

    def scores(h, j, r0, diag):
        k = k_ref[0, h, pl.ds(r0, tk), :]
        q = q_ref[0, h, :, j * cw:(j + 1) * cw]
        s = jnp.dot(k, q, preferred_element_type=F32)
        if diag:
            key = lax.broadcasted_iota(jnp.int32, (tk, cw), 0)
            qry = lax.broadcasted_iota(jnp.int32, (tk, cw), 1)
            s = jnp.where(key <= qry, s, NEG_BIG)
        s_scr[h, j] = s

    def softmax_pv(h, j, r0):
        c = j * cw
        vt = vt_ref[0, h, :, pl.ds(r0, tk)]
        m_old = m_scr[h, :, c:c + cw]
        m_new = jnp.maximum(m_old, jnp.max(s_scr[h, j], axis=0, keepdims=True))
        p = jnp.exp2(s_scr[h, j] - m_new).astype(BF16)
        m_scr[h, :, c:c + cw] = m_new
        pv = jnp.dot(vt, p, preferred_element_type=F32)
        acc_scr[h, j] = jnp.exp2(m_old - m_new) * acc_scr[h, j] + pv

    def step(r0, c0, masked):
        units = [(h, j) for h in range(nh) for j in range(c0 // cw, tq // cw)]
        for u, (h, j) in enumerate(units[:ATT_AHEAD]):
            scores(h, j, r0, masked and j == c0 // cw)
        for u, (h, j) in enumerate(units):
            if u + ATT_AHEAD < len(units):
                h2, j2 = units[u + ATT_AHEAD]
                scores(h2, j2, r0, masked and j2 == c0 // cw)
            softmax_pv(h, j, r0)

    def full_steps(kb, carry):
        for i in range(ATT_UNROLL):
            step(pl.multiple_of((kb * ATT_UNROLL + i) * tk, tk), 0, False)
        return carry

    lax.fori_loop(0, qi * (tq // tk // ATT_UNROLL), full_steps, 0)
    for j in range(tq // tk):
        step(pl.multiple_of(qi * tq + j * tk, tk), j * tk, True)

    def normalized(h):
        acc = jnp.concatenate([acc_scr[h, j] for j in range(tq // cw)], axis=1)
        return acc[:MLA_V] / acc[MLA_V:MLA_V + 1]

    for h in range(0, nh, 2):
        pair = jnp.concatenate([normalized(h), normalized(h + 1)], axis=0).T
        cols = slice(h * MLA_V, (h + 2) * MLA_V)
        o_ref[0, :, cols] = (pair * g_ref[0, :, cols]).astype(BF16)


def _attn_call(qt, k, vt, gm):
    B, H, S, _ = k.shape
    tq, nh = ATT_TQ, ATT_NH
    assert ATT_CW == ATT_TK and (tq // ATT_TK) % ATT_UNROLL == 0
    grid = (B, H // nh, S // tq)
    return pl.pallas_call(
        _attn_kernel,
        grid=grid,
        in_specs=[
            pl.BlockSpec((1, nh, HEAD_SLAB, tq), lambda b, j, i: (b, j, 0, i)),
            pl.BlockSpec((1, nh, S, HEAD_SLAB), lambda b, j, i: (b, j, 0, 0)),
            pl.BlockSpec((1, nh, VT_ROWS, S), lambda b, j, i: (b, j, 0, 0)),
            pl.BlockSpec((1, tq, nh * MLA_V), lambda b, j, i: (b, i, j)),
        ],
        out_specs=pl.BlockSpec((1, tq, nh * MLA_V), lambda b, j, i: (b, i, j)),
        out_shape=jax.ShapeDtypeStruct((B, S, MLA_WIDTH), BF16),
        scratch_shapes=[
            pltpu.VMEM((nh, tq // ATT_CW, ATT_TK, ATT_CW), F32),
            pltpu.VMEM((nh, 1, tq), F32),
            pltpu.VMEM((nh, tq // ATT_CW, VT_ROWS, ATT_CW), F32),
        ],
        compiler_params=pltpu.CompilerParams(
            dimension_semantics=("parallel", "parallel", "arbitrary"),
            vmem_limit_bytes=VMEM_LIMIT),
        name="attn",
    )(qt, k, vt, gm)


def _hgrn_constants():
    C, SUB, NL = HG_CHUNK, HG_SUB, HG_LEVELS
    t = np.arange(C)
    lm = np.zeros((2 * NL, C, C), np.float32)
    masks = np.zeros((NL, C, C), np.float32)
    masks[0] = (t[:, None] // SUB) == (t[None, :] // SUB)
    for l in range(NL):
        bs = SUB << l
        same = (t[:, None] // bs) == (t[None, :] // bs)
        lm[l] = same & (t[None, :] <= t[:, None])
        lm[NL + l] = same & (t[None, :] > t[:, None])
        if l < NL - 1:
            masks[l + 1] = ((t[:, None] // bs) == (t[None, :] // bs) + 1) & ((t[None, :] // bs) % 2 == 0)
    lmat = lm.reshape(2 * NL * C, C)
    lmat = np.concatenate([lmat, lmat], axis=1)
    j = np.arange(C)
    sel = np.zeros((SUB, LANES, C), np.float32)
    sel[:] = (j[None, None, :] % SUB) == np.arange(SUB)[:, None, None]
    return (jnp.asarray(lmat, BF16), jnp.asarray(sel.reshape(SUB * LANES, C), BF16),
            jnp.asarray(masks, F32))


def _hgrn_kernel(f_ref, q_ref, v_ref, g_ref, ng_ref, lmat_ref, sel_ref, mask_ref, o_ref,
                 st_scr, qs_scr, ks_scr, c8_scr, k3_scr, p2_scr, a_scr, ds_scr, oi_scr, dec_scr):
    C, SUB, NL, TB, NH = HG_CHUNK, HG_SUB, HG_LEVELS, HG_TB, HG_HEADS
    NC = TB // C

    @pl.when(pl.program_id(1) == 0)
    def _():
        st_scr[...] = jnp.zeros_like(st_scr)

    lmat = lmat_ref[...]
    nb = C // SUB
    heads = range(NH)

    for c in range(NC):
        rows = slice(c * C, (c + 1) * C)
        blks = slice(c * nb, (c + 1) * nb)
        f = [f_ref[0, h, rows, :] for h in heads]
        g = jnp.log2(jnp.concatenate(f, axis=1))
        g1 = g.astype(BF16)
        g2 = (g - g1.astype(F32)).astype(BF16)
        e = jnp.dot(lmat, jnp.concatenate([g1, g2], axis=0), preferred_element_type=F32)
        for h in heads:
            hs = slice(h * LANES, (h + 1) * LANES)
            q = q_ref[0, h, rows, :]
            k = 1.0 - f[h]
            k3_scr[h, blks] = k.reshape(nb, SUB, LANES)
            for l in range(NL):
                cq = e[l * C:(l + 1) * C, hs]
                ck = e[(NL + l) * C:(NL + l + 1) * C, hs]
                qs_scr[l, h, rows, :] = (q * jnp.exp2(cq)).astype(BF16)
                ks_scr[l, h, rows, :] = (k * jnp.exp2(ck)).astype(BF16)
                if l == 0:
                    c8_scr[h, blks] = cq.reshape(nb, SUB, LANES)
                if l == NL - 1:
                    dec_scr[c:c + 1, hs] = jnp.exp2(cq[C - 1:C, :])

    tt = lax.broadcasted_iota(jnp.int32, (1, SUB, LANES), 1)
    for h in heads:
        q3 = q_ref[0, h].reshape(TB // SUB, SUB, LANES)
        c8 = c8_scr[h]
        for s in range(SUB):
            bcast = pl.ds(s, SUB, stride=0)
            dg = jnp.where(tt >= s, c8 - c8_scr[h, :, bcast, :], NEG_BIG)
            p = q3 * jnp.exp2(dg) * k3_scr[h, :, bcast, :]
            p2_scr[h, s] = p.reshape(TB, LANES).astype(BF16)

    sel_masks = [mask_ref[l] > 0.5 for l in range(NL)]
    for h in heads:
        p2 = jnp.concatenate([p2_scr[h, s] for s in range(SUB)], axis=1)
        ad = jnp.dot(p2, sel_ref[...], preferred_element_type=F32)
        for c in range(NC):
            rows = slice(c * C, (c + 1) * C)
            a = jnp.where(sel_masks[0], ad[rows], 0.0)
            for l in range(NL - 1):
                al = lax.dot_general(qs_scr[l, h, rows, :], ks_scr[l, h, rows, :], _NT,
                                     preferred_element_type=F32)
                a = jnp.where(sel_masks[l + 1], al, a)
            a_scr[h, rows, :] = a.astype(BF16)
    for h in heads:
        for c in range(NC):
            rows = slice(c * C, (c + 1) * C)
            v = v_ref[0, h, rows, :]
            oi_scr[h, rows, :] = jnp.dot(a_scr[h, rows, :], v, preferred_element_type=F32)
            ds_scr[h * NC + c] = lax.dot_general(v, ks_scr[NL - 1, h, rows, :], _TN,
                                                 preferred_element_type=F32)

    for c in range(NC):
        rows = slice(c * C, (c + 1) * C)
        for h in heads:
            hs = slice(h * LANES, (h + 1) * LANES)
            st = st_scr[h]
            o = oi_scr[h, rows, :] + lax.dot_general(qs_scr[NL - 1, h, rows, :], st.astype(BF16), _NT,
                                                     preferred_element_type=F32)
            st_scr[h] = st * dec_scr[c:c + 1, hs] + ds_scr[h * NC + c]
            o = o * lax.rsqrt(jnp.mean(o * o, axis=-1, keepdims=True) + EPS) * ng_ref[:, hs]
            o_ref[0, h, rows, :] = (o * g_ref[0, h, rows, :]).astype(BF16)


def _hgrn_call(hf, hq, hv, gh, ng):
    B, NH, S, _ = hf.shape
    tb, C, SUB, NL = HG_TB, HG_CHUNK, HG_SUB, HG_LEVELS
    assert SUB << (NL - 1) == C and NH == HG_HEADS
    lmat, sel, masks = _hgrn_constants()
    grid = (B, S // tb)
    blk = pl.BlockSpec((1, NH, tb, LANES), lambda b, t: (b, 0, t, 0))
    full = lambda a: pl.BlockSpec(a.shape, lambda b, t: (0,) * a.ndim, pipeline_mode=pl.Buffered(1))
    return pl.pallas_call(
        _hgrn_kernel,
        grid=grid,
        in_specs=[blk, blk, blk, blk, full(ng), full(lmat), full(sel), full(masks)],
        out_specs=blk,
        out_shape=jax.ShapeDtypeStruct((B, NH, S, LANES), BF16),
        scratch_shapes=[
            pltpu.VMEM((NH, HG_HEAD_V, HG_EXPAND), F32),
            pltpu.VMEM((NL, NH, tb, LANES), BF16),
            pltpu.VMEM((NL, NH, tb, LANES), BF16),
            pltpu.VMEM((NH, tb // SUB, SUB, LANES), F32),
            pltpu.VMEM((NH, tb // SUB, SUB, LANES), F32),
            pltpu.VMEM((NH, SUB, tb, LANES), BF16),
            pltpu.VMEM((NH, tb, C), BF16),
            pltpu.VMEM((NH * (tb // C), HG_HEAD_V, HG_EXPAND), F32),
            pltpu.VMEM((NH, tb, LANES), F32),
            pltpu.VMEM((tb // C, NH * LANES), F32),
        ],
        compiler_params=pltpu.CompilerParams(
            dimension_semantics=("parallel", "arbitrary"),
            vmem_limit_bytes=VMEM_LIMIT),
        name="hgrn",
    )(hf, hq, hv, gh, ng, lmat, sel, masks)


def _out_kernel(ya_ref, yh_ref, x_ref, wa_ref, wh_ref, fg_ref, o_ref):
    y = jnp.dot(ya_ref[0], wa_ref[...], preferred_element_type=F32)
    yh = jnp.concatenate([yh_ref[0, h] for h in range(HG_HEADS)], axis=1)
    y = y + jnp.dot(yh, wh_ref[...], preferred_element_type=F32)
    o_ref[0] = _rms(x_ref[0] + y, fg_ref[...])


def _out_call(ya, yh, x, wa, wh, fg):
    B, S, _ = x.shape
    tm = OUT_TM
    grid = (B, S // tm)
    tok = lambda w: pl.BlockSpec((1, tm, w), lambda b, i: (b, i, 0))
    full = lambda a: pl.BlockSpec(a.shape, lambda b, i: (0,) * a.ndim, pipeline_mode=pl.Buffered(1))
    return pl.pallas_call(
        _out_kernel,
        grid=grid,
        in_specs=[tok(MLA_WIDTH), pl.BlockSpec((1, HG_HEADS, tm, LANES), lambda b, i: (b, 0, i, 0)),
                  tok(D_MODEL), full(wa), full(wh), full(fg)],
        out_specs=tok(D_MODEL),
        out_shape=jax.ShapeDtypeStruct((B, S, D_MODEL), F32),
        compiler_params=pltpu.CompilerParams(
            dimension_semantics=("parallel", "parallel"), vmem_limit_bytes=VMEM_LIMIT),
        name="outproj",
    )(ya, yh, x, wa, wh, fg)


def _pack_win_kernel(wt_ref, o_ref):
    j = pl.program_id(0)
    kr_blk = _C_KRM // LANES
    shift = LANES - MLA_ROPE
    start = pl.multiple_of(jnp.where(j <= kr_blk, j * LANES, j * LANES - shift), 8)
    blk = wt_ref[pl.ds(start, LANES), :].T
    lane = lax.broadcasted_iota(jnp.int32, blk.shape, 1)
    blk = jnp.where(jnp.logical_and(j == kr_blk, lane >= MLA_ROPE), 0.0, blk)
    o_ref[...] = blk.astype(BF16)


def _pack_win(w_in):
    rows, cols = w_in.shape
    wt = jnp.transpose(w_in)
    pad_rows = _C_END - (LANES - MLA_ROPE) - cols
    assert pad_rows == 0
    return pl.pallas_call(
        _pack_win_kernel,
        grid=(_C_END // LANES,),
        in_specs=[pl.BlockSpec((cols, rows), lambda j: (0, 0), pipeline_mode=pl.Buffered(1))],
        out_specs=pl.BlockSpec((rows, LANES), lambda j: (0, j)),
        out_shape=jax.ShapeDtypeStruct((rows, _C_END), BF16),
        compiler_params=pltpu.CompilerParams(dimension_semantics=("arbitrary",), vmem_limit_bytes=VMEM_LIMIT),
        name="packwin",
    )(wt)


def _pack_weights(w_in, w_q_b, w_kv_b):
    win = _pack_win(w_in)

    wq = w_q_b.reshape(MLA_Q_RANK, MLA_HEADS, MLA_NOPE + MLA_ROPE)
    wqm = jnp.transpose(wq, (1, 2, 0)).astype(BF16)

    wkv = w_kv_b.reshape(MLA_KV_RANK, MLA_HEADS, MLA_NOPE + MLA_V)
    wk = wkv[..., :MLA_NOPE].reshape(MLA_KV_RANK, MLA_HEADS * MLA_NOPE).astype(BF16)
    wvt = jnp.transpose(wkv[..., MLA_NOPE:], (1, 2, 0)).reshape(MLA_HEADS * MLA_V, MLA_KV_RANK).astype(BF16)
    return win, wqm, wk, wvt


def _rope_table():
    inv = ROPE_THETA ** (-jnp.arange(HALF, dtype=F32) / HALF)
    return inv.reshape(HALF, 1)


def kernel(x, positions, ln_g, w_in, q_a_norm_g, w_q_b, kv_a_norm_g, w_kv_b,
           hg_lower_bounds, hg_norm_g, w_out, final_norm_g):
    B, S, _ = x.shape
    assert ln_g.shape[0] == 1, "single-layer stack"
    win, wqm, wk, wvt = _pack_weights(w_in[0], w_q_b[0], w_kv_b[0])
    pos3 = positions.reshape(B, 1, S)
    q, k, vt, gm, hq, hf, hv, gh = _proj_call(
        x, pos3, ln_g[0:1], win, q_a_norm_g[0:1], wqm, kv_a_norm_g[0:1], wk, wvt,
        hg_lower_bounds, _rope_table())
    ya = _attn_call(q, k, vt, gm)
    yh = _hgrn_call(hf, hq, hv, gh, hg_norm_g[0:1])
    wo = w_out[0].astype(BF16)
    return _out_call(ya, yh, x, wo[:MLA_WIDTH], wo[MLA_WIDTH:], final_norm_g.reshape(1, D_MODEL))
```

```python
import math

import numpy as np
import jax
import jax.numpy as jnp
from jax import lax
from jax.experimental import pallas as pl
from jax.experimental.pallas import tpu as pltpu

F32 = jnp.float32
BF16 = jnp.bfloat16

D_MODEL = 1024
MLA_HEADS = 8
MLA_NOPE = 64
MLA_ROPE = 32
MLA_V = 64
MLA_Q_RANK = 256
MLA_KV_RANK = 128
MLA_WIDTH = MLA_HEADS * MLA_V
HG_HEADS = 4
HG_EXPAND = 128
HG_HEAD_V = 128
HG_WIDTH = HG_HEADS * HG_HEAD_V
HG_FDIM = HG_HEADS * HG_EXPAND
ROPE_THETA = 10000.0
EPS = 1e-6
HALF = MLA_ROPE // 2

LANES = 128
HEAD_SLAB = LANES
NEG_BIG = -1e30
LOG2E = math.log2(math.e)
VT_ROWS = 80

PROJ_TM = 1024
ATT_TQ = 1024
ATT_TK = 256
ATT_NH = 8
ATT_AHEAD = 8
ATT_UNROLL = 4
ATT_CW = 256
HG_CHUNK = 64
HG_SUB = 8
HG_LEVELS = 4
HG_TB = 1024
OUT_TM = 2048
VMEM_LIMIT = 56 * 1024 * 1024

_C_QLAT = 0
_C_KVLAT = _C_QLAT + MLA_Q_RANK
_C_KRM = _C_KVLAT + MLA_KV_RANK
_C_GM = _C_KRM + LANES
_C_HQ = _C_GM + MLA_WIDTH
_C_HF = _C_HQ + HG_FDIM
_C_HI = _C_HF + HG_FDIM
_C_GH = _C_HI + HG_WIDTH
_C_END = _C_GH + HG_WIDTH

_NT = (((1,), (1,)), ((), ()))
_TN = (((0,), (0,)), ((), ()))


def _rms(x, g):
    return x * lax.rsqrt(jnp.mean(x * x, axis=-1, keepdims=True) + EPS) * g


def _silu(x):
    return x * jax.nn.sigmoid(x)


def _swap_rope_halves(slab):
    lane = lax.broadcasted_iota(jnp.int32, slab.shape, 1)
    return jnp.where(lane < HALF, pltpu.roll(slab, LANES - HALF, 1), pltpu.roll(slab, HALF, 1))


def _proj_kernel(x_ref, pos_ref, lng_ref, win_ref, qg_ref, wqt_ref,
                 kvg_ref, wk_ref, wvt_ref, lbraw_ref, rope_ref,
                 q_out, k_out, vt_out, gm_out, hq_out, hf_out, hv_out, gh_out):
    x = x_ref[0]
    tm = x.shape[0]
    h = _rms(x, lng_ref[...]).astype(BF16)

    def in_proj(c0, width):
        return jnp.dot(h, win_ref[:, c0:c0 + width], preferred_element_type=F32)

    lat = in_proj(0, _C_GM)
    gm_out[0] = _silu(in_proj(_C_GM, MLA_WIDTH))

    ang_t = rope_ref[...] * pos_ref[0].astype(F32)
    cos_t = jnp.cos(ang_t)
    sin_t = jnp.sin(ang_t)
    cos_h = cos_t.T
    sin_h = sin_t.T
    pad = LANES - MLA_ROPE
    cos = jnp.concatenate([cos_h, cos_h, jnp.ones((tm, pad), F32)], axis=1)
    sin = jnp.concatenate([-sin_h, sin_h, jnp.zeros((tm, pad), F32)], axis=1)

    scale = LOG2E / math.sqrt(MLA_NOPE + MLA_ROPE)
    cos_q = jnp.concatenate([cos_t, cos_t], axis=0) * scale
    sin_q = jnp.concatenate([-sin_t, sin_t], axis=0) * scale
    zero_rows = jnp.zeros((LANES - MLA_NOPE - MLA_ROPE, tm), BF16)
    qn = _rms(lat[:, _C_QLAT:_C_QLAT + MLA_Q_RANK], qg_ref[...]).astype(BF16)
    for hd in range(MLA_HEADS):
        qt = lax.dot_general(wqt_ref[hd], qn, _NT, preferred_element_type=F32)
        nope = (qt[:MLA_NOPE] * scale).astype(BF16)
        x12 = qt[MLA_NOPE:]
        x21 = jnp.concatenate([x12[HALF:], x12[:HALF]], axis=0)
        rot = (x12 * cos_q + x21 * sin_q).astype(BF16)
        slab = [nope, rot, zero_rows] if hd % 2 == 0 else [rot, zero_rows, nope]
        q_out[0, hd] = jnp.concatenate(slab, axis=0)
    kvn = _rms(lat[:, _C_KVLAT:_C_KVLAT + MLA_KV_RANK], kvg_ref[...]).astype(BF16)
    kn = jnp.dot(kvn, wk_ref[...], preferred_element_type=F32)
    kr = lat[:, _C_KRM:_C_KRM + LANES]
    kr_odd = kr * cos + _swap_rope_halves(kr) * sin
    kr_even = pltpu.roll(kr_odd, MLA_NOPE, 1)
    lane = lax.broadcasted_iota(jnp.int32, (tm, LANES), 1)
    vt_all = lax.dot_general(wvt_ref[...], kvn, _NT, preferred_element_type=F32).astype(BF16)
    tail = (lax.broadcasted_iota(jnp.int32, (VT_ROWS - MLA_V, tm), 0) == 0).astype(BF16)
    for hd in range(MLA_HEADS):
        col = kn[:, (hd // 2) * LANES:(hd // 2 + 1) * LANES]
        k_slab = jnp.where(lane < MLA_NOPE, col, kr_even) if hd % 2 == 0 else jnp.where(lane >= MLA_NOPE, col, kr_odd)
        k_out[0, hd] = k_slab.astype(BF16)
        vt_out[0, hd] = jnp.concatenate([vt_all[hd * MLA_V:(hd + 1) * MLA_V], tail], axis=0)

    a0 = lbraw_ref[0:1, :]
    a1 = lbraw_ref[1:2, :]
    mx = jnp.maximum(a0, a1)
    e0 = jnp.exp(a0 - mx)
    e1 = jnp.exp(a1 - mx)
    lb = e0 / (e0 + e1)
    def put_heads(out, val):
        for hd in range(HG_HEADS):
            out[0, hd] = val[:, hd * LANES:(hd + 1) * LANES]

    put_heads(hq_out, _silu(in_proj(_C_HQ, HG_FDIM)))
    put_heads(hf_out, lb + (1.0 - lb) * jax.nn.sigmoid(in_proj(_C_HF, HG_FDIM)))
    put_heads(gh_out, _silu(in_proj(_C_GH, HG_WIDTH)))
    put_heads(hv_out, in_proj(_C_HI, HG_WIDTH).astype(BF16))


def _proj_call(x, pos3, lng, win, qg, wqm, kvg, wk, wvt, lbraw, rope_tab):
    B, S, _ = x.shape
    tm = PROJ_TM
    grid = (B, S // tm)
    tok = lambda w: pl.BlockSpec((1, tm, w), lambda b, i: (b, i, 0))
    full = lambda a: pl.BlockSpec(a.shape, lambda b, i: (0,) * a.ndim, pipeline_mode=pl.Buffered(1))
    head = pl.BlockSpec((1, MLA_HEADS, tm, HEAD_SLAB), lambda b, i: (b, 0, i, 0))
    out_shape = (
        jax.ShapeDtypeStruct((B, MLA_HEADS, HEAD_SLAB, S), BF16),
        jax.ShapeDtypeStruct((B, MLA_HEADS, S, HEAD_SLAB), BF16),
        jax.ShapeDtypeStruct((B, MLA_HEADS, VT_ROWS, S), BF16),
        jax.ShapeDtypeStruct((B, S, MLA_WIDTH), F32),
        jax.ShapeDtypeStruct((B, HG_HEADS, S, LANES), F32),
        jax.ShapeDtypeStruct((B, HG_HEADS, S, LANES), F32),
        jax.ShapeDtypeStruct((B, HG_HEADS, S, LANES), BF16),
        jax.ShapeDtypeStruct((B, HG_HEADS, S, LANES), F32),
    )
    hg = pl.BlockSpec((1, HG_HEADS, tm, LANES), lambda b, i: (b, 0, i, 0))
    return pl.pallas_call(
        _proj_kernel,
        grid=grid,
        in_specs=[tok(D_MODEL), pl.BlockSpec((1, 1, tm), lambda b, i: (b, 0, i)), full(lng), full(win),
                  full(qg), full(wqm), full(kvg), full(wk), full(wvt), full(lbraw), full(rope_tab)],
        out_specs=(pl.BlockSpec((1, MLA_HEADS, HEAD_SLAB, tm), lambda b, i: (b, 0, 0, i)), head,
                   pl.BlockSpec((1, MLA_HEADS, VT_ROWS, tm), lambda b, i: (b, 0, 0, i)),
                   tok(MLA_WIDTH), hg, hg, hg, hg),
        out_shape=out_shape,
        compiler_params=pltpu.CompilerParams(
            dimension_semantics=("parallel", "parallel"), vmem_limit_bytes=VMEM_LIMIT),
        name="proj",
    )(x, pos3, lng, win, qg, wqm, kvg, wk, wvt, lbraw, rope_tab)


def _attn_kernel(q_ref, k_ref, vt_ref, g_ref, o_ref, s_scr, m_scr, acc_scr):
    qi = pl.program_id(2)
    tq, tk, nh, cw = ATT_TQ, ATT_TK, ATT_NH, ATT_CW
    m_scr[...] = jnp.full(m_scr.shape, NEG_BIG, F32)
    acc_scr[...] = jnp.zeros(acc_scr.shape, F32)

    def scores(h, j, r0, diag):
        k = k_ref[0, h, pl.ds(r0, tk), :]
        q = q_ref[0, h, :, j * cw:(j + 1) * cw]
        s = jnp.dot(k, q, preferred_element_type=F32)
        if diag:
            key = lax.broadcasted_iota(jnp.int32, (tk, cw), 0)
            qry = lax.broadcasted_iota(jnp.int32, (tk, cw), 1)
            s = jnp.where(key <= qry, s, NEG_BIG)
        s_scr[h, j] = s

    def softmax_pv(h, j, r0):
        c = j * cw
        vt = vt_ref[0, h, :, pl.ds(r0, tk)]
        m_old = m_scr[h, :, c:c + cw]
        m_new = jnp.maximum(m_old, jnp.max(s_scr[h, j], axis=0, keepdims=True))
        p = jnp.exp2(s_scr[h, j] - m_new).astype(BF16)
        m_scr[h, :, c:c + cw] = m_new
        pv = jnp.dot(vt, p, preferred_element_type=F32)
        acc_scr[h, j] = jnp.exp2(m_old - m_new) * acc_scr[h, j] + pv

    def step(r0, c0, masked):
        units = [(h, j) for h in range(nh) for j in range(c0 // cw, tq // cw)]
        for u, (h, j) in enumerate(units[:ATT_AHEAD]):
            scores(h, j, r0, masked and j == c0 // cw)
        for u, (h, j) in enumerate(units):
            if u + ATT_AHEAD < len(units):
                h2, j2 = units[u + ATT_AHEAD]
                scores(h2, j2, r0, masked and j2 == c0 // cw)
            softmax_pv(h, j, r0)

    def full_steps(kb, carry):
        for i in range(ATT_UNROLL):
            step(pl.multiple_of((kb * ATT_UNROLL + i) * tk, tk), 0, False)
        return carry

    lax.fori_loop(0, qi * (tq // tk // ATT_UNROLL), full_steps, 0)
    for j in range(tq // tk):
        step(pl.multiple_of(qi * tq + j * tk, tk), j * tk, True)

    def normalized(h):
        acc = jnp.concatenate([acc_scr[h, j] for j in range(tq // cw)], axis=1)
        return acc[:MLA_V] / acc[MLA_V:MLA_V + 1]

    for h in range(0, nh, 2):
        pair = jnp.concatenate([normalized(h), normalized(h + 1)], axis=0).T
        cols = slice(h * MLA_V, (h + 2) * MLA_V)
        o_ref[0, :, cols] = (pair * g_ref[0, :, cols]).astype(BF16)


def _attn_call(qt, k, vt, gm):
    B, H, S, _ = k.shape
    tq, nh = ATT_TQ, ATT_NH
    assert ATT_CW == ATT_TK and (tq // ATT_TK) % ATT_UNROLL == 0
    grid = (B, H // nh, S // tq)
    return pl.pallas_call(
        _attn_kernel,
        grid=grid,
        in_specs=[
            pl.BlockSpec((1, nh, HEAD_SLAB, tq), lambda b, j, i: (b, j, 0, i)),
            pl.BlockSpec((1, nh, S, HEAD_SLAB), lambda b, j, i: (b, j, 0, 0)),
            pl.BlockSpec((1, nh, VT_ROWS, S), lambda b, j, i: (b, j, 0, 0)),
            pl.BlockSpec((1, tq, nh * MLA_V), lambda b, j, i: (b, i, j)),
        ],
        out_specs=pl.BlockSpec((1, tq, nh * MLA_V), lambda b, j, i: (b, i, j)),
        out_shape=jax.ShapeDtypeStruct((B, S, MLA_WIDTH), BF16),
        scratch_shapes=[
            pltpu.VMEM((nh, tq // ATT_CW, ATT_TK, ATT_CW), F32),
            pltpu.VMEM((nh, 1, tq), F32),
            pltpu.VMEM((nh, tq // ATT_CW, VT_ROWS, ATT_CW), F32),
        ],
        compiler_params=pltpu.CompilerParams(
            dimension_semantics=("parallel", "parallel", "arbitrary"),
            vmem_limit_bytes=VMEM_LIMIT),
        name="attn",
    )(qt, k, vt, gm)


def _hgrn_constants():
    C, SUB, NL = HG_CHUNK, HG_SUB, HG_LEVELS
    t = np.arange(C)
    lm = np.zeros((2 * NL, C, C), np.float32)
    masks = np.zeros((NL, C, C), np.float32)
    masks[0] = (t[:, None] // SUB) == (t[None, :] // SUB)
    for l in range(NL):
        bs = SUB << l
        same = (t[:, None] // bs) == (t[None, :] // bs)
        lm[l] = same & (t[None, :] <= t[:, None])
        lm[NL + l] = same & (t[None, :] > t[:, None])
        if l < NL - 1:
            masks[l + 1] = ((t[:, None] // bs) == (t[None, :] // bs) + 1) & ((t[None, :] // bs) % 2 == 0)
    lmat = lm.reshape(2 * NL * C, C)
    lmat = np.concatenate([lmat, lmat], axis=1)
    j = np.arange(C)
    sel = np.zeros((SUB, LANES, C), np.float32)
    sel[:] = (j[None, None, :] % SUB) == np.arange(SUB)[:, None, None]
    return (jnp.asarray(lmat, BF16), jnp.asarray(sel.reshape(SUB * LANES, C), BF16),
            jnp.asarray(masks, F32))


def _hgrn_kernel(f_ref, q_ref, v_ref, g_ref, ng_ref, lmat_ref, sel_ref, mask_ref, o_ref,
                 st_scr, qs_scr, ks_scr, c8_scr, k3_scr, p2_scr, a_scr, ds_scr, oi_scr, dec_scr):
    C, SUB, NL, TB, NH = HG_CHUNK, HG_SUB, HG_LEVELS, HG_TB, HG_HEADS
    NC = TB // C

    @pl.when(pl.program_id(1) == 0)
    def _():
        st_scr[...] = jnp.zeros_like(st_scr)

    lmat = lmat_ref[...]
    nb = C // SUB
    heads = range(NH)

    for c in range(NC):
        rows = slice(c * C, (c + 1) * C)
        blks = slice(c * nb, (c + 1) * nb)
        f = [f_ref[0, h, rows, :] for h in heads]
        g = jnp.log2(jnp.concatenate(f, axis=1))
        g1 = g.astype(BF16)
        g2 = (g - g1.astype(F32)).astype(BF16)
        e = jnp.dot(lmat, jnp.concatenate([g1, g2], axis=0), preferred_element_type=F32)
        for h in heads:
            hs = slice(h * LANES, (h + 1) * LANES)
            q = q_ref[0, h, rows, :]
            k = 1.0 - f[h]
            k3_scr[h, blks] = k.reshape(nb, SUB, LANES)
            for l in range(NL):
                cq = e[l * C:(l + 1) * C, hs]
                ck = e[(NL + l) * C:(NL + l + 1) * C, hs]
                qs_scr[l, h, rows, :] = (q * jnp.exp2(cq)).astype(BF16)
                ks_scr[l, h, rows, :] = (k * jnp.exp2(ck)).astype(BF16)
                if l == 0:
                    c8_scr[h, blks] = cq.reshape(nb, SUB, LANES)
                if l == NL - 1:
                    dec_scr[c:c + 1, hs] = jnp.exp2(cq[C - 1:C, :])

    tt = lax.broadcasted_iota(jnp.int32, (1, SUB, LANES), 1)
    for h in heads:
        q3 = q_ref[0, h].reshape(TB // SUB, SUB, LANES)
        c8 = c8_scr[h]
        for s in range(SUB):
            bcast = pl.ds(s, SUB, stride=0)
            dg = jnp.where(tt >= s, c8 - c8_scr[h, :, bcast, :], NEG_BIG)
            p = q3 * jnp.exp2(dg) * k3_scr[h, :, bcast, :]
            p2_scr[h, s] = p.reshape(TB, LANES).astype(BF16)

    sel_masks = [mask_ref[l] > 0.5 for l in range(NL)]
    for h in heads:
        p2 = jnp.concatenate([p2_scr[h, s] for s in range(SUB)], axis=1)
        ad = jnp.dot(p2, sel_ref[...], preferred_element_type=F32)
        for c in range(NC):
            rows = slice(c * C, (c + 1) * C)
            a = jnp.where(sel_masks[0], ad[rows], 0.0)
            for l in range(NL - 1):
                al = lax.dot_general(qs_scr[l, h, rows, :], ks_scr[l, h, rows, :], _NT,
                                     preferred_element_type=F32)
                a = jnp.where(sel_masks[l + 1], al, a)
            a_scr[h, rows, :] = a.astype(BF16)
    for h in heads:
        for c in range(NC):
            rows = slice(c * C, (c + 1) * C)
            v = v_ref[0, h, rows, :]
            oi_scr[h, rows, :] = jnp.dot(a_scr[h, rows, :], v, preferred_element_type=F32)
            ds_scr[h * NC + c] = lax.dot_general(v, ks_scr[NL - 1, h, rows, :], _TN,
                                                 preferred_element_type=F32)

    for c in range(NC):
        rows = slice(c * C, (c + 1) * C)
        for h in heads:
            hs = slice(h * LANES, (h + 1) * LANES)
            st = st_scr[h]
            o = oi_scr[h, rows, :] + lax.dot_general(qs_scr[NL - 1, h, rows, :], st.astype(BF16), _NT,
                                                     preferred_element_type=F32)
            st_scr[h] = st * dec_scr[c:c + 1, hs] + ds_scr[h * NC + c]
            o = o * lax.rsqrt(jnp.mean(o * o, axis=-1, keepdims=True) + EPS) * ng_ref[:, hs]
            o_ref[0, h, rows, :] = (o * g_ref[0, h, rows, :]).astype(BF16)


def _hgrn_call(hf, hq, hv, gh, ng):
    B, NH, S, _ = hf.shape
    tb, C, SUB, NL = HG_TB, HG_CHUNK, HG_SUB, HG_LEVELS
    assert SUB << (NL - 1) == C and NH == HG_HEADS
    lmat, sel, masks = _hgrn_constants()
    grid = (B, S // tb)
    blk = pl.BlockSpec((1, NH, tb, LANES), lambda b, t: (b, 0, t, 0))
    full = lambda a: pl.BlockSpec(a.shape, lambda b, t: (0,) * a.ndim, pipeline_mode=pl.Buffered(1))
    return pl.pallas_call(
        _hgrn_kernel,
        grid=grid,
        in_specs=[blk, blk, blk, blk, full(ng), full(lmat), full(sel), full(masks)],
        out_specs=blk,
        out_shape=jax.ShapeDtypeStruct((B, NH, S, LANES), BF16),
        scratch_shapes=[
            pltpu.VMEM((NH, HG_HEAD_V, HG_EXPAND), F32),
            pltpu.VMEM((NL, NH, tb, LANES), BF16),
            pltpu.VMEM((NL, NH, tb, LANES), BF16),
            pltpu.VMEM((NH, tb // SUB, SUB, LANES), F32),
            pltpu.VMEM((NH, tb // SUB, SUB, LANES), F32),
            pltpu.VMEM((NH, SUB, tb, LANES), BF16),
            pltpu.VMEM((NH, tb, C), BF16),
            pltpu.VMEM((NH * (tb // C), HG_HEAD_V, HG_EXPAND), F32),
            pltpu.VMEM((NH, tb, LANES), F32),
            pltpu.VMEM((tb // C, NH * LANES), F32),
        ],
        compiler_params=pltpu.CompilerParams(
            dimension_semantics=("parallel", "arbitrary"),
            vmem_limit_bytes=VMEM_LIMIT),
        name="hgrn",
    )(hf, hq, hv, gh, ng, lmat, sel, masks)


def _out_kernel(ya_ref, yh_ref, x_ref, wa_ref, wh_ref, fg_ref, o_ref):
    y = jnp.dot(ya_ref[0], wa_ref[...], preferred_element_type=F32)
    yh = jnp.concatenate([yh_ref[0, h] for h in range(HG_HEADS)], axis=1)
    y = y + jnp.dot(yh, wh_ref[...], preferred_element_type=F32)
    o_ref[0] = _rms(x_ref[0] + y, fg_ref[...])


def _out_call(ya, yh, x, wa, wh, fg):
    B, S, _ = x.shape
    tm = OUT_TM
    grid = (B, S // tm)
    tok = lambda w: pl.BlockSpec((1, tm, w), lambda b, i: (b, i, 0))
    full = lambda a: pl.BlockSpec(a.shape, lambda b, i: (0,) * a.ndim, pipeline_mode=pl.Buffered(1))
    return pl.pallas_call(
        _out_kernel,
        grid=grid,
        in_specs=[tok(MLA_WIDTH), pl.BlockSpec((1, HG_HEADS, tm, LANES), lambda b, i: (b, 0, i, 0)),
                  tok(D_MODEL), full(wa), full(wh), full(fg)],
        out_specs=tok(D_MODEL),
        out_shape=jax.ShapeDtypeStruct((B, S, D_MODEL), F32),
        compiler_params=pltpu.CompilerParams(
            dimension_semantics=("parallel", "parallel"), vmem_limit_bytes=VMEM_LIMIT),
        name="outproj",
    )(ya, yh, x, wa, wh, fg)


def _pack_win_kernel(wt_ref, o_ref):
    j = pl.program_id(0)
    kr_blk = _C_KRM // LANES
    shift = LANES - MLA_ROPE
    start = pl.multiple_of(jnp.where(j <= kr_blk, j * LANES, j * LANES - shift), 8)
    blk = wt_ref[pl.ds(start, LANES), :].T
    lane = lax.broadcasted_iota(jnp.int32, blk.shape, 1)
    blk = jnp.where(jnp.logical_and(j == kr_blk, lane >= MLA_ROPE), 0.0, blk)
    o_ref[...] = blk.astype(BF16)


def _pack_win(w_in):
    rows, cols = w_in.shape
    wt = jnp.transpose(w_in)
    pad_rows = _C_END - (LANES - MLA_ROPE) - cols
    assert pad_rows == 0
    return pl.pallas_call(
        _pack_win_kernel,
        grid=(_C_END // LANES,),
        in_specs=[pl.BlockSpec((cols, rows), lambda j: (0, 0), pipeline_mode=pl.Buffered(1))],
        out_specs=pl.BlockSpec((rows, LANES), lambda j: (0, j)),
        out_shape=jax.ShapeDtypeStruct((rows, _C_END), BF16),
        compiler_params=pltpu.CompilerParams(dimension_semantics=("arbitrary",), vmem_limit_bytes=VMEM_LIMIT),
        name="packwin",
    )(wt)


def _pack_weights(w_in, w_q_b, w_kv_b):
    win = _pack_win(w_in)

    wq = w_q_b.reshape(MLA_Q_RANK, MLA_HEADS, MLA_NOPE + MLA_ROPE)
    wqm = jnp.transpose(wq, (1, 2, 0)).astype(BF16)

    wkv = w_kv_b.reshape(MLA_KV_RANK, MLA_HEADS, MLA_NOPE + MLA_V)
    wk = wkv[..., :MLA_NOPE].reshape(MLA_KV_RANK, MLA_HEADS * MLA_NOPE).astype(BF16)
    wvt = jnp.transpose(wkv[..., MLA_NOPE:], (1, 2, 0)).reshape(MLA_HEADS * MLA_V, MLA_KV_RANK).astype(BF16)
    return win, wqm, wk, wvt


def _rope_table():
    inv = ROPE_THETA ** (-jnp.arange(HALF, dtype=F32) / HALF)
    return inv.reshape(HALF, 1)


def kernel(x, positions, ln_g, w_in, q_a_norm_g, w_q_b, kv_a_norm_g, w_kv_b,
           hg_lower_bounds, hg_norm_g, w_out, final_norm_g):
    B, S, _ = x.shape
    assert ln_g.shape[0] == 1, "single-layer stack"
    win, wqm, wk, wvt = _pack_weights(w_in[0], w_q_b[0], w_kv_b[0])
    pos3 = positions.reshape(B, 1, S)
    q, k, vt, gm, hq, hf, hv, gh = _proj_call(
        x, pos3, ln_g[0:1], win, q_a_norm_g[0:1], wqm, kv_a_norm_g[0:1], wk, wvt,
        hg_lower_bounds, _rope_table())
    ya = _attn_call(q, k, vt, gm)
    yh = _hgrn_call(hf, hq, hv, gh, hg_norm_g[0:1])
    wo = w_out[0].astype(BF16)
    return _out_call(ya, yh, x, wo[:MLA_WIDTH], wo[MLA_WIDTH:], final_norm_g.reshape(1, D_MODEL))
```

```python
import math

import numpy as np
import jax
import jax.numpy as jnp
from jax import lax
from jax.experimental import pallas as pl
from jax.experimental.pallas import tpu as pltpu

F32 = jnp.float32
BF16 = jnp.bfloat16

D_MODEL = 1024
MLA_HEADS = 8
MLA_NOPE = 64
MLA_ROPE = 32
MLA_V = 64
MLA_Q_RANK = 256
MLA_KV_RANK = 128
MLA_WIDTH = MLA_HEADS * MLA_V
HG_HEADS = 4
HG_EXPAND = 128
HG_HEAD_V = 128
HG_WIDTH = HG_HEADS * HG_HEAD_V
HG_FDIM = HG_HEADS * HG_EXPAND
ROPE_THETA = 10000.0
EPS = 1e-6
HALF = MLA_ROPE // 2

LANES = 128
HEAD_SLAB = LANES
NEG_BIG = -1e30
LOG2E = math.log2(math.e)
VT_ROWS = 80

PROJ_TM = 1024
ATT_TQ = 1024
ATT_TK = 256
ATT_NH = 8
ATT_AHEAD = 8
ATT_UNROLL = 4
ATT_CW = 256
HG_CHUNK = 64
HG_SUB = 8
HG_LEVELS = 4
HG_TB = 1024
OUT_TM = 1024
VMEM_LIMIT = 56 * 1024 * 1024

_C_QLAT = 0
_C_KVLAT = _C_QLAT + MLA_Q_RANK
_C_KRM = _C_KVLAT + MLA_KV_RANK
_C_GM = _C_KRM + LANES
_C_HQ = _C_GM + MLA_WIDTH
_C_HF = _C_HQ + HG_FDIM
_C_HI = _C_HF + HG_FDIM
_C_GH = _C_HI + HG_WIDTH
_C_END = _C_GH + HG_WIDTH

_NT = (((1,), (1,)), ((), ()))
_TN = (((0,), (0,)), ((), ()))


def _rms(x, g):
    return x * lax.rsqrt(jnp.mean(x * x, axis=-1, keepdims=True) + EPS) * g


def _silu(x):
    return x * jax.nn.sigmoid(x)


def _swap_rope_halves(slab):
    lane = lax.broadcasted_iota(jnp.int32, slab.shape, 1)
    return jnp.where(lane < HALF, pltpu.roll(slab, LANES - HALF, 1), pltpu.roll(slab, HALF, 1))


def _proj_kernel(x_ref, pos_ref, lng_ref, win_ref, qg_ref, wqt_ref,
                 kvg_ref, wk_ref, wvt_ref, lbraw_ref, rope_ref,
                 q_out, k_out, vt_out, gm_out, hq_out, hf_out, hv_out, gh_out):
    x = x_ref[0]
    tm = x.shape[0]
    h = _rms(x, lng_ref[...]).astype(BF16)

    def in_proj(c0, width):
        return jnp.dot(h, win_ref[:, c0:c0 + width], preferred_element_type=F32)

    lat = in_proj(0, _C_GM)
    gm_out[0] = _silu(in_proj(_C_GM, MLA_WIDTH))

    ang_t = rope_ref[...] * pos_ref[0].astype(F32)
    cos_t = jnp.cos(ang_t)
    sin_t = jnp.sin(ang_t)
    cos_h = cos_t.T
    sin_h = sin_t.T
    pad = LANES - MLA_ROPE
    cos = jnp.concatenate([cos_h, cos_h, jnp.ones((tm, pad), F32)], axis=1)
    sin = jnp.concatenate([-sin_h, sin_h, jnp.zeros((tm, pad), F32)], axis=1)

    scale = LOG2E / math.sqrt(MLA_NOPE + MLA_ROPE)
    cos_q = jnp.concatenate([cos_t, cos_t], axis=0) * scale
    sin_q = jnp.concatenate([-sin_t, sin_t], axis=0) * scale
    zero_rows = jnp.zeros((LANES - MLA_NOPE - MLA_ROPE, tm), BF16)
    qn = _rms(lat[:, _C_QLAT:_C_QLAT + MLA_Q_RANK], qg_ref[...]).astype(BF16)
    for hd in range(MLA_HEADS):
        qt = lax.dot_general(wqt_ref[hd], qn, _NT, preferred_element_type=F32)
        nope = (qt[:MLA_NOPE] * scale).astype(BF16)
        x12 = qt[MLA_NOPE:]
        x21 = jnp.concatenate([x12[HALF:], x12[:HALF]], axis=0)
        rot = (x12 * cos_q + x21 * sin_q).astype(BF16)
        slab = [nope, rot, zero_rows] if hd % 2 == 0 else [rot, zero_rows, nope]
        q_out[0, hd] = jnp.concatenate(slab, axis=0)
    kvn = _rms(lat[:, _C_KVLAT:_C_KVLAT + MLA_KV_RANK], kvg_ref[...]).astype(BF16)
    kn = jnp.dot(kvn, wk_ref[...], preferred_element_type=F32)
    kr = lat[:, _C_KRM:_C_KRM + LANES]
    kr_odd = kr * cos + _swap_rope_halves(kr) * sin
    kr_even = pltpu.roll(kr_odd, MLA_NOPE, 1)
    lane = lax.broadcasted_iota(jnp.int32, (tm, LANES), 1)
    vt_all = lax.dot_general(wvt_ref[...], kvn, _NT, preferred_element_type=F32).astype(BF16)
    tail = (lax.broadcasted_iota(jnp.int32, (VT_ROWS - MLA_V, tm), 0) == 0).astype(BF16)
    for hd in range(MLA_HEADS):
        col = kn[:, (hd // 2) * LANES:(hd // 2 + 1) * LANES]
        k_slab = jnp.where(lane < MLA_NOPE, col, kr_even) if hd % 2 == 0 else jnp.where(lane >= MLA_NOPE, col, kr_odd)
        k_out[0, hd] = k_slab.astype(BF16)
        vt_out[0, hd] = jnp.concatenate([vt_all[hd * MLA_V:(hd + 1) * MLA_V], tail], axis=0)

    a0 = lbraw_ref[0:1, :]
    a1 = lbraw_ref[1:2, :]
    mx = jnp.maximum(a0, a1)
    e0 = jnp.exp(a0 - mx)
    e1 = jnp.exp(a1 - mx)
    lb = e0 / (e0 + e1)
    def put_heads(out, val):
        for hd in range(HG_HEADS):
            out[0, hd] = val[:, hd * LANES:(hd + 1) * LANES]

    put_heads(hq_out, _silu(in_proj(_C_HQ, HG_FDIM)))
    put_heads(hf_out, lb + (1.0 - lb) * jax.nn.sigmoid(in_proj(_C_HF, HG_FDIM)))
    put_heads(hv_out, in_proj(_C_HI, HG_WIDTH).astype(BF16))
    put_heads(gh_out, _silu(in_proj(_C_GH, HG_WIDTH)))


def _proj_call(x, pos3, lng, win, qg, wqm, kvg, wk, wvt, lbraw, rope_tab):
    B, S, _ = x.shape
    tm = PROJ_TM
    grid = (B, S // tm)
    tok = lambda w: pl.BlockSpec((1, tm, w), lambda b, i: (b, i, 0))
    full = lambda a: pl.BlockSpec(a.shape, lambda b, i: (0,) * a.ndim, pipeline_mode=pl.Buffered(1))
    head = pl.BlockSpec((1, MLA_HEADS, tm, HEAD_SLAB), lambda b, i: (b, 0, i, 0))
    out_shape = (
        jax.ShapeDtypeStruct((B, MLA_HEADS, HEAD_SLAB, S), BF16),
        jax.ShapeDtypeStruct((B, MLA_HEADS, S, HEAD_SLAB), BF16),
        jax.ShapeDtypeStruct((B, MLA_HEADS, VT_ROWS, S), BF16),
        jax.ShapeDtypeStruct((B, S, MLA_WIDTH), F32),
        jax.ShapeDtypeStruct((B, HG_HEADS, S, LANES), F32),
        jax.ShapeDtypeStruct((B, HG_HEADS, S, LANES), F32),
        jax.ShapeDtypeStruct((B, HG_HEADS, S, LANES), BF16),
        jax.ShapeDtypeStruct((B, HG_HEADS, S, LANES), F32),
    )
    hg = pl.BlockSpec((1, HG_HEADS, tm, LANES), lambda b, i: (b, 0, i, 0))
    return pl.pallas_call(
        _proj_kernel,
        grid=grid,
        in_specs=[tok(D_MODEL), pl.BlockSpec((1, 1, tm), lambda b, i: (b, 0, i)), full(lng), full(win),
                  full(qg), full(wqm), full(kvg), full(wk), full(wvt), full(lbraw), full(rope_tab)],
        out_specs=(pl.BlockSpec((1, MLA_HEADS, HEAD_SLAB, tm), lambda b, i: (b, 0, 0, i)), head,
                   pl.BlockSpec((1, MLA_HEADS, VT_ROWS, tm), lambda b, i: (b, 0, 0, i)),
                   tok(MLA_WIDTH), hg, hg, hg, hg),
        out_shape=out_shape,
        compiler_params=pltpu.CompilerParams(
            dimension_semantics=("parallel", "parallel"), vmem_limit_bytes=VMEM_LIMIT),
        name="proj",
    )(x, pos3, lng, win, qg, wqm, kvg, wk, wvt, lbraw, rope_tab)


def _attn_kernel(q_ref, k_ref, vt_ref, g_ref, o_ref, s_scr, m_scr, acc_scr):
    qi = pl.program_id(2)
    tq, tk, nh, cw = ATT_TQ, ATT_TK, ATT_NH, ATT_CW
    m_scr[...] = jnp.full(m_scr.shape, NEG_BIG, F32)
    acc_scr[...] = jnp.zeros(acc_scr.shape, F32)

    def scores(h, j, r0, diag):
        k = k_ref[0, h, pl.ds(r0, tk), :]
        q = q_ref[0, h, :, j * cw:(j + 1) * cw]
        s = jnp.dot(k, q, preferred_element_type=F32)
        if diag:
            key = lax.broadcasted_iota(jnp.int32, (tk, cw), 0)
            qry = lax.broadcasted_iota(jnp.int32, (tk, cw), 1)
            s = jnp.where(key <= qry, s, NEG_BIG)
        s_scr[h, j] = s

    def softmax_pv(h, j, r0):
        c = j * cw
        vt = vt_ref[0, h, :, pl.ds(r0, tk)]
        m_old = m_scr[h, :, c:c + cw]
        m_new = jnp.maximum(m_old, jnp.max(s_scr[h, j], axis=0, keepdims=True))
        p = jnp.exp2(s_scr[h, j] - m_new).astype(BF16)
        m_scr[h, :, c:c + cw] = m_new
        pv = jnp.dot(vt, p, preferred_element_type=F32)
        acc_scr[h, j] = jnp.exp2(m_old - m_new) * acc_scr[h, j] + pv

    def step(r0, c0, masked):
        units = [(h, j) for h in range(nh) for j in range(c0 // cw, tq // cw)]
        for u, (h, j) in enumerate(units[:ATT_AHEAD]):
            scores(h, j, r0, masked and j == c0 // cw)
        for u, (h, j) in enumerate(units):
            if u + ATT_AHEAD < len(units):
                h2, j2 = units[u + ATT_AHEAD]
                scores(h2, j2, r0, masked and j2 == c0 // cw)
            softmax_pv(h, j, r0)

    def full_steps(kb, carry):
        for i in range(ATT_UNROLL):
            step(pl.multiple_of((kb * ATT_UNROLL + i) * tk, tk), 0, False)
        return carry

    lax.fori_loop(0, qi * (tq // tk // ATT_UNROLL), full_steps, 0)
    for j in range(tq // tk):
        step(pl.multiple_of(qi * tq + j * tk, tk), j * tk, True)

    def normalized(h):
        acc = jnp.concatenate([acc_scr[h, j] for j in range(tq // cw)], axis=1)
        return acc[:MLA_V] / acc[MLA_V:MLA_V + 1]

    for h in range(0, nh, 2):
        pair = jnp.concatenate([normalized(h), normalized(h + 1)], axis=0).T
        cols = slice(h * MLA_V, (h + 2) * MLA_V)
        o_ref[0, :, cols] = (pair * g_ref[0, :, cols]).astype(BF16)


def _attn_call(qt, k, vt, gm):
    B, H, S, _ = k.shape
    tq, nh = ATT_TQ, ATT_NH
    assert ATT_CW == ATT_TK and (tq // ATT_TK) % ATT_UNROLL == 0
    grid = (B, H // nh, S // tq)
    return pl.pallas_call(
        _attn_kernel,
        grid=grid,
        in_specs=[
            pl.BlockSpec((1, nh, HEAD_SLAB, tq), lambda b, j, i: (b, j, 0, i)),
            pl.BlockSpec((1, nh, S, HEAD_SLAB), lambda b, j, i: (b, j, 0, 0)),
            pl.BlockSpec((1, nh, VT_ROWS, S), lambda b, j, i: (b, j, 0, 0)),
            pl.BlockSpec((1, tq, nh * MLA_V), lambda b, j, i: (b, i, j)),
        ],
        out_specs=pl.BlockSpec((1, tq, nh * MLA_V), lambda b, j, i: (b, i, j)),
        out_shape=jax.ShapeDtypeStruct((B, S, MLA_WIDTH), BF16),
        scratch_shapes=[
            pltpu.VMEM((nh, tq // ATT_CW, ATT_TK, ATT_CW), F32),
            pltpu.VMEM((nh, 1, tq), F32),
            pltpu.VMEM((nh, tq // ATT_CW, VT_ROWS, ATT_CW), F32),
        ],
        compiler_params=pltpu.CompilerParams(
            dimension_semantics=("parallel", "parallel", "arbitrary"),
            vmem_limit_bytes=VMEM_LIMIT),
        name="attn",
    )(qt, k, vt, gm)


def _hgrn_constants():
    C, SUB, NL = HG_CHUNK, HG_SUB, HG_LEVELS
    t = np.arange(C)
    lm = np.zeros((2 * NL, C, C), np.float32)
    masks = np.zeros((NL, C, C), np.float32)
    masks[0] = (t[:, None] // SUB) == (t[None, :] // SUB)
    for l in range(NL):
        bs = SUB << l
        same = (t[:, None] // bs) == (t[None, :] // bs)
        lm[l] = same & (t[None, :] <= t[:, None])
        lm[NL + l] = same & (t[None, :] > t[:, None])
        if l < NL - 1:
            masks[l + 1] = ((t[:, None] // bs) == (t[None, :] // bs) + 1) & ((t[None, :] // bs) % 2 == 0)
    lmat = lm.reshape(2 * NL * C, C)
    lmat = np.concatenate([lmat, lmat], axis=1)
    j = np.arange(C)
    sel = np.zeros((SUB, LANES, C), np.float32)
    sel[:] = (j[None, None, :] % SUB) == np.arange(SUB)[:, None, None]
    return (jnp.asarray(lmat, BF16), jnp.asarray(sel.reshape(SUB * LANES, C), BF16),
            jnp.asarray(masks, F32))


def _hgrn_kernel(f_ref, q_ref, v_ref, g_ref, ng_ref, lmat_ref, sel_ref, mask_ref, o_ref,
                 st_scr, qs_scr, ks_scr, c8_scr, k3_scr, p2_scr, a_scr, ds_scr, oi_scr, dec_scr):
    C, SUB, NL, TB, NH = HG_CHUNK, HG_SUB, HG_LEVELS, HG_TB, HG_HEADS
    NC = TB // C

    @pl.when(pl.program_id(1) == 0)
    def _():
        st_scr[...] = jnp.zeros_like(st_scr)

    lmat = lmat_ref[...]
    nb = C // SUB
    heads = range(NH)

    for c in range(NC):
        rows = slice(c * C, (c + 1) * C)
        blks = slice(c * nb, (c + 1) * nb)
        f = [f_ref[0, h, rows, :] for h in heads]
        g = jnp.log2(jnp.concatenate(f, axis=1))
        g1 = g.astype(BF16)
        g2 = (g - g1.astype(F32)).astype(BF16)
        e = jnp.dot(lmat, jnp.concatenate([g1, g2], axis=0), preferred_element_type=F32)
        for h in heads:
            hs = slice(h * LANES, (h + 1) * LANES)
            q = q_ref[0, h, rows, :]
            k = 1.0 - f[h]
            k3_scr[h, blks] = k.reshape(nb, SUB, LANES)
            for l in range(NL):
                cq = e[l * C:(l + 1) * C, hs]
                ck = e[(NL + l) * C:(NL + l + 1) * C, hs]
                qs_scr[l, h, rows, :] = (q * jnp.exp2(cq)).astype(BF16)
                ks_scr[l, h, rows, :] = (k * jnp.exp2(ck)).astype(BF16)
                if l == 0:
                    c8_scr[h, blks] = cq.reshape(nb, SUB, LANES)
                if l == NL - 1:
                    dec_scr[c:c + 1, hs] = jnp.exp2(cq[C - 1:C, :])

    tt = lax.broadcasted_iota(jnp.int32, (1, SUB, LANES), 1)
    for h in heads:
        q3 = q_ref[0, h].reshape(TB // SUB, SUB, LANES)
        c8 = c8_scr[h]
        for s in range(SUB):
            bcast = pl.ds(s, SUB, stride=0)
            dg = jnp.where(tt >= s, c8 - c8_scr[h, :, bcast, :], NEG_BIG)
            p = q3 * jnp.exp2(dg) * k3_scr[h, :, bcast, :]
            p2_scr[h, s] = p.reshape(TB, LANES).astype(BF16)

    sel_masks = [mask_ref[l] > 0.5 for l in range(NL)]
    for h in heads:
        p2 = jnp.concatenate([p2_scr[h, s] for s in range(SUB)], axis=1)
        ad = jnp.dot(p2, sel_ref[...], preferred_element_type=F32)
        for c in range(NC):
            rows = slice(c * C, (c + 1) * C)
            a = jnp.where(sel_masks[0], ad[rows], 0.0)
            for l in range(NL - 1):
                al = lax.dot_general(qs_scr[l, h, rows, :], ks_scr[l, h, rows, :], _NT,
                                     preferred_element_type=F32)
                a = jnp.where(sel_masks[l + 1], al, a)
            a_scr[h, rows, :] = a.astype(BF16)
    for h in heads:
        for c in range(NC):
            rows = slice(c * C, (c + 1) * C)
            v = v_ref[0, h, rows, :]
            oi_scr[h, rows, :] = jnp.dot(a_scr[h, rows, :], v, preferred_element_type=F32)
            ds_scr[h * NC + c] = lax.dot_general(v, ks_scr[NL - 1, h, rows, :], _TN,
                                                 preferred_element_type=F32)

    for c in range(NC):
        rows = slice(c * C, (c + 1) * C)
        for h in heads:
            hs = slice(h * LANES, (h + 1) * LANES)
            st = st_scr[h]
            o = oi_scr[h, rows, :] + lax.dot_general(qs_scr[NL - 1, h, rows, :], st.astype(BF16), _NT,
                                                     preferred_element_type=F32)
            st_scr[h] = st * dec_scr[c:c + 1, hs] + ds_scr[h * NC + c]
            o = o * lax.rsqrt(jnp.mean(o * o, axis=-1, keepdims=True) + EPS) * ng_ref[:, hs]
            o_ref[0, h, rows, :] = (o * g_ref[0, h, rows, :]).astype(BF16)


def _hgrn_call(hf, hq, hv, gh, ng):
    B, NH, S, _ = hf.shape
    tb, C, SUB, NL = HG_TB, HG_CHUNK, HG_SUB, HG_LEVELS
    assert SUB << (NL - 1) == C and NH == HG_HEADS
    lmat, sel, masks = _hgrn_constants()
    grid = (B, S // tb)
    blk = pl.BlockSpec((1, NH, tb, LANES), lambda b, t: (b, 0, t, 0))
    full = lambda a: pl.BlockSpec(a.shape, lambda b, t: (0,) * a.ndim, pipeline_mode=pl.Buffered(1))
    return pl.pallas_call(
        _hgrn_kernel,
        grid=grid,
        in_specs=[blk, blk, blk, blk, full(ng), full(lmat), full(sel), full(masks)],
        out_specs=blk,
        out_shape=jax.ShapeDtypeStruct((B, NH, S, LANES), BF16),
        scratch_shapes=[
            pltpu.VMEM((NH, HG_HEAD_V, HG_EXPAND), F32),
            pltpu.VMEM((NL, NH, tb, LANES), BF16),
            pltpu.VMEM((NL, NH, tb, LANES), BF16),
            pltpu.VMEM((NH, tb // SUB, SUB, LANES), F32),
            pltpu.VMEM((NH, tb // SUB, SUB, LANES), F32),
            pltpu.VMEM((NH, SUB, tb, LANES), BF16),
            pltpu.VMEM((NH, tb, C), BF16),
            pltpu.VMEM((NH * (tb // C), HG_HEAD_V, HG_EXPAND), F32),
            pltpu.VMEM((NH, tb, LANES), F32),
            pltpu.VMEM((tb // C, NH * LANES), F32),
        ],
        compiler_params=pltpu.CompilerParams(
            dimension_semantics=("parallel", "arbitrary"),
            vmem_limit_bytes=VMEM_LIMIT),
        name="hgrn",
    )(hf, hq, hv, gh, ng, lmat, sel, masks)


def _out_kernel(ya_ref, yh_ref, x_ref, wa_ref, wh_ref, fg_ref, o_ref):
    y = jnp.dot(ya_ref[0], wa_ref[...], preferred_element_type=F32)
    yh = jnp.concatenate([yh_ref[0, h] for h in range(HG_HEADS)], axis=1)
    y = y + jnp.dot(yh, wh_ref[...], preferred_element_type=F32)
    o_ref[0] = _rms(x_ref[0] + y, fg_ref[...])


def _out_call(ya, yh, x, wa, wh, fg):
    B, S, _ = x.shape
    tm = OUT_TM
    tok = lambda w: pl.BlockSpec((1, tm, w), lambda b, i: (b, i, 0), pipeline_mode=pl.Buffered(3))
    yh_spec = pl.BlockSpec((1, HG_HEADS, tm, LANES), lambda b, i: (b, 0, i, 0), pipeline_mode=pl.Buffered(3))
    out_spec = pl.BlockSpec((1, tm, D_MODEL), lambda b, i: (b, i, 0))

    def outer(ya_hbm, yh_hbm, x_hbm, wa_ref, wh_ref, fg_ref, o_hbm):
        def body(ya_ref, yh_ref, x_ref, o_ref):
            _out_kernel(ya_ref, yh_ref, x_ref, wa_ref, wh_ref, fg_ref, o_ref)

        pltpu.emit_pipeline(body, grid=(B, S // tm), in_specs=[tok(MLA_WIDTH), yh_spec, tok(D_MODEL)],
                            out_specs=[out_spec])(ya_hbm, yh_hbm, x_hbm, o_hbm)

    hbm = pl.BlockSpec(memory_space=pl.ANY)
    vmem = pl.BlockSpec(memory_space=pltpu.VMEM)
    return pl.pallas_call(
        outer,
        in_specs=[hbm, hbm, hbm, vmem, vmem, vmem],
        out_specs=hbm,
        out_shape=jax.ShapeDtypeStruct((B, S, D_MODEL), F32),
        compiler_params=pltpu.CompilerParams(vmem_limit_bytes=VMEM_LIMIT),
        name="outproj",
    )(ya, yh, x, wa, wh, fg)


def _pack_win_kernel(wt_ref, o_ref):
    j = pl.program_id(0)
    kr_blk = _C_KRM // LANES
    shift = LANES - MLA_ROPE
    start = pl.multiple_of(jnp.where(j <= kr_blk, j * LANES, j * LANES - shift), 8)
    blk = wt_ref[pl.ds(start, LANES), :].T
    lane = lax.broadcasted_iota(jnp.int32, blk.shape, 1)
    blk = jnp.where(jnp.logical_and(j == kr_blk, lane >= MLA_ROPE), 0.0, blk)
    o_ref[...] = blk.astype(BF16)


def _pack_win(w_in):
    rows, cols = w_in.shape
    wt = jnp.transpose(w_in)
    pad_rows = _C_END - (LANES - MLA_ROPE) - cols
    assert pad_rows == 0
    return pl.pallas_call(
        _pack_win_kernel,
        grid=(_C_END // LANES,),
        in_specs=[pl.BlockSpec((cols, rows), lambda j: (0, 0), pipeline_mode=pl.Buffered(1))],
        out_specs=pl.BlockSpec((rows, LANES), lambda j: (0, j)),
        out_shape=jax.ShapeDtypeStruct((rows, _C_END), BF16),
        compiler_params=pltpu.CompilerParams(dimension_semantics=("arbitrary",), vmem_limit_bytes=VMEM_LIMIT),
        name="packwin",
    )(wt)


def _pack_weights(w_in, w_q_b, w_kv_b):
    win = _pack_win(w_in)

    wq = w_q_b.reshape(MLA_Q_RANK, MLA_HEADS, MLA_NOPE + MLA_ROPE)
    wqm = jnp.transpose(wq, (1, 2, 0)).astype(BF16)

    wkv = w_kv_b.reshape(MLA_KV_RANK, MLA_HEADS, MLA_NOPE + MLA_V)
    wk = wkv[..., :MLA_NOPE].reshape(MLA_KV_RANK, MLA_HEADS * MLA_NOPE).astype(BF16)
    wvt = jnp.transpose(wkv[..., MLA_NOPE:], (1, 2, 0)).reshape(MLA_HEADS * MLA_V, MLA_KV_RANK).astype(BF16)
    return win, wqm, wk, wvt


def _rope_table():
    inv = ROPE_THETA ** (-jnp.arange(HALF, dtype=F32) / HALF)
    return inv.reshape(HALF, 1)


def kernel(x, positions, ln_g, w_in, q_a_norm_g, w_q_b, kv_a_norm_g, w_kv_b,
           hg_lower_bounds, hg_norm_g, w_out, final_norm_g):
    B, S, _ = x.shape
    assert ln_g.shape[0] == 1, "single-layer stack"
    win, wqm, wk, wvt = _pack_weights(w_in[0], w_q_b[0], w_kv_b[0])
    pos3 = positions.reshape(B, 1, S)
    q, k, vt, gm, hq, hf, hv, gh = _proj_call(
        x, pos3, ln_g[0:1], win, q_a_norm_g[0:1], wqm, kv_a_norm_g[0:1], wk, wvt,
        hg_lower_bounds, _rope_table())
    ya = _attn_call(q, k, vt, gm)
    yh = _hgrn_call(hf, hq, hv, gh, hg_norm_g[0:1])
    wo = w_out[0].astype(BF16)
    return _out_call(ya, yh, x, wo[:MLA_WIDTH], wo[MLA_WIDTH:], final_norm_g.reshape(1, D_MODEL))
```
